```python
import math
import jax, jax.numpy as jnp
from jax import lax
import numpy as np

D_MODEL = 2048
BATCH = 4
SEQ = 2048
DEPTH = 4
DEC_BATCH = 32
DEC_SEQ = 8
PAST_LEN = 16384
PAGE_SIZE = 128

N_META = 16
D_MIX = D_MODEL
D_ATTN = D_MIX // 2
D_REC = D_MIX - D_ATTN
HEAD_DIM = 64
N_HEADS = D_ATTN // HEAD_DIM
N_KV = 4
GROUP = N_HEADS // N_KV
KV_DIM = N_KV * HEAD_DIM
WINDOW = 128
ATT_BLOCK = 128
N_BUCKETS = 32
MAX_DISTANCE = 128
REC_HEAD_DIM = 128
N_REC_HEADS = D_REC // REC_HEAD_DIM
REC_CHUNK = 64
EPS = 1e-6
D_IN = 2 * D_ATTN + 2 * KV_DIM + 4 * D_REC
SPLITS = (D_ATTN, D_ATTN + KV_DIM, D_ATTN + 2 * KV_DIM, 2 * D_ATTN + 2 * KV_DIM,
          2 * D_ATTN + 2 * KV_DIM + D_REC, 2 * D_ATTN + 2 * KV_DIM + 2 * D_REC,
          2 * D_ATTN + 2 * KV_DIM + 3 * D_REC)

kernel_name = 'hymba_swa_sink_hgrn2_decoder_step'


def rmsnorm(x, w):
    xf = x.astype(jnp.float32)
    y = xf * lax.rsqrt(jnp.mean(xf * xf, axis=-1, keepdims=True) + EPS)
    return y.astype(x.dtype) * w


def t5_bucket(dist):
    max_exact = N_BUCKETS // 2
    d = jnp.maximum(dist, 0)
    df = jnp.maximum(d, 1).astype(jnp.float32)
    large = max_exact + (jnp.log(df / max_exact) / math.log(MAX_DISTANCE / max_exact)
                         * (N_BUCKETS - max_exact)).astype(jnp.int32)
    large = jnp.minimum(large, N_BUCKETS - 1)
    return jnp.where(d < max_exact, d, large)


def sink_softmax(s, sink):
    sk = sink.astype(jnp.float32)[:, :, None, None]
    m = jnp.maximum(jnp.max(s, axis=-1, keepdims=True), sk)
    e = jnp.exp(s - m)
    return e / (jnp.sum(e, axis=-1, keepdims=True) + jnp.exp(sk - m))


def project(h, norm_w, w_in, lb):
    B, T = h.shape[0], h.shape[1]
    z = rmsnorm(h, norm_w) @ w_in
    qa, ka, va, ga, qr, fr, ir, gr = jnp.split(z, SPLITS, axis=-1)
    qa = qa.reshape(B, T, N_HEADS, HEAD_DIM)
    ka = ka.reshape(B, T, N_KV, HEAD_DIM)
    va = va.reshape(B, T, N_KV, HEAD_DIM)
    fr32 = fr.astype(jnp.float32)
    lb32 = lb.astype(jnp.float32)
    logf = jnp.log(lb32 + (1.0 - lb32) * jax.nn.sigmoid(fr32)).reshape(B, T, N_REC_HEADS, REC_HEAD_DIM)
    kr = ((1.0 - lb32) * jax.nn.sigmoid(-fr32)).reshape(B, T, N_REC_HEADS, REC_HEAD_DIM)
    qr = jax.nn.silu(qr).reshape(B, T, N_REC_HEADS, REC_HEAD_DIM)
    ir = ir.reshape(B, T, N_REC_HEADS, REC_HEAD_DIM)
    return qa, ka, va, ga, qr, logf, kr, ir, gr


def swa_prompt(q, k, v, sinks, bias_d):
    B, L = q.shape[0], q.shape[1]
    pad = (-L) % ATT_BLOCK
    padf = lambda a: jnp.pad(a, ((0, 0), (pad, 0), (0, 0), (0, 0)))
    nb = (L + pad) // ATT_BLOCK
    qb = padf(q).reshape(B, nb, ATT_BLOCK, N_KV, GROUP, HEAD_DIM)
    kb = padf(k).reshape(B, nb, ATT_BLOCK, N_KV, HEAD_DIM)
    vb = padf(v).reshape(B, nb, ATT_BLOCK, N_KV, HEAD_DIM)
    prev = lambda a: jnp.concatenate([jnp.zeros_like(a[:, :1]), a[:, :-1]], axis=1)
    kk = jnp.concatenate([prev(kb), kb], axis=2)
    vv = jnp.concatenate([prev(vb), vb], axis=2)
    s = jnp.einsum('bnqkgd,bnskd->bnkgqs', qb, kk,
                   preferred_element_type=jnp.float32) * (HEAD_DIM ** -0.5)
    qi = jnp.arange(ATT_BLOCK)
    sj = jnp.arange(2 * ATT_BLOCK)
    dist = ATT_BLOCK + qi[:, None] - sj[None, :]
    kpos = (jnp.arange(nb)[:, None] - 1) * ATT_BLOCK + sj[None, :] - pad
    valid = ((dist >= 0) & (dist < WINDOW))[None] & (kpos >= 0)[:, None, :]
    bias = bias_d[:, jnp.clip(dist, 0, WINDOW - 1)].reshape(N_KV, GROUP, ATT_BLOCK, 2 * ATT_BLOCK)
    s = jnp.where(valid[None, :, None, None], s + bias, -jnp.inf)
    p = sink_softmax(s, sinks.reshape(N_KV, GROUP))
    o = jnp.einsum('bnkgqs,bnskd->bnqkgd', p.astype(v.dtype), vv)
    return o.reshape(B, nb * ATT_BLOCK, N_HEADS, HEAD_DIM)[:, pad:]


def swa_sample(q, k_new, v_new, k_buf, v_buf, sinks, bias_d):
    Bd, T = q.shape[0], q.shape[1]
    W = k_buf.shape[1]
    kk = jnp.concatenate([k_buf.astype(k_new.dtype), k_new], axis=1)
    vv = jnp.concatenate([v_buf.astype(v_new.dtype), v_new], axis=1)
    qg = q.reshape(Bd, T, N_KV, GROUP, HEAD_DIM)
    s = jnp.einsum('btkgd,bskd->bkgts', qg, kk,
                   preferred_element_type=jnp.float32) * (HEAD_DIM ** -0.5)
    kpos = jnp.arange(W + T) - W
    dist = jnp.arange(T)[:, None] - kpos[None, :]
    valid = (dist >= 0) & (dist < WINDOW)
    bias = bias_d[:, jnp.clip(dist, 0, WINDOW - 1)].reshape(N_KV, GROUP, T, W + T)
    s = jnp.where(valid, s + bias, -jnp.inf)
    p = sink_softmax(s, sinks.reshape(N_KV, GROUP))
    o = jnp.einsum('bkgts,bskd->btkgd', p.astype(v_new.dtype), vv).reshape(Bd, T, N_HEADS, HEAD_DIM)
    return o, kk[:, -W:], vv[:, -W:]


def hgrn2_chunked(q, logf, k, v, s0):
    B, T = q.shape[0], q.shape[1]
    C = min(REC_CHUNK, T)
    pad = (-T) % C
    padf = lambda a: jnp.pad(a.astype(jnp.float32), ((0, 0), (pad, 0), (0, 0), (0, 0)))
    nc = (T + pad) // C
    to_chunks = lambda a: padf(a).reshape(B, nc, C, N_REC_HEADS, a.shape[-1]).transpose(1, 0, 3, 2, 4)
    qs, gs, ks, vs = to_chunks(q), to_chunks(logf), to_chunks(k), to_chunks(v)
    tril = jnp.tril(jnp.ones((C, C), dtype=bool))

    def step(S, inp):
        qc, gc, kc, vc = inp
        A = jnp.cumsum(gc, axis=2)
        A_last = A[:, :, -1:]
        diff = A[:, :, :, None, :] - A[:, :, None, :, :]
        decay = jnp.exp(jnp.where(tril[:, :, None], diff, -jnp.inf))
        attn = jnp.einsum('bhtd,bhtsd,bhsd->bhts', qc, decay, kc)
        o = jnp.einsum('bhts,bhsv->bhtv', attn, vc) + jnp.einsum('bhtd,bhdv->bhtv', qc * jnp.exp(A), S)
        S_new = jnp.exp(A_last)[:, :, 0, :, None] * S + jnp.einsum('bhsd,bhsv->bhdv', kc * jnp.exp(A_last - A), vc)
        return S_new, o

    s_fin, o = lax.scan(step, s0.astype(jnp.float32), (qs, gs, ks, vs))
    o = o.transpose(1, 0, 3, 2, 4).reshape(B, nc * C, N_REC_HEADS, REC_HEAD_DIM)[:, pad:]
    return o.astype(v.dtype), s_fin


def mix_out(h, oa, ga, orec, gr, rec_norm_w, w_out):
    B, T = h.shape[0], h.shape[1]
    ya = oa.reshape(B, T, D_ATTN) * jax.nn.silu(ga)
    of = orec.astype(jnp.float32)
    on = (of * lax.rsqrt(jnp.mean(of * of, axis=-1, keepdims=True) + EPS)).astype(h.dtype)
    yr = (on * rec_norm_w.reshape(N_REC_HEADS, REC_HEAD_DIM)).reshape(B, T, D_REC) * jax.nn.silu(gr)
    return h + jnp.concatenate([ya, yr], axis=-1) @ w_out


def setup_inputs(seed: int = 0) -> dict:
    key = jax.random.key(seed)
    ks = jax.random.split(key, 14)
    nrm = jax.random.normal
    win = min(WINDOW, PAST_LEN)
    return {
        'x_prompt': nrm(ks[0], (BATCH, SEQ, D_MODEL), jnp.float32),
        'x_sample': nrm(ks[1], (DEC_BATCH, DEC_SEQ, D_MODEL), jnp.float32),
        'cache_k': nrm(ks[2], (DEPTH, DEC_BATCH, win, N_KV, HEAD_DIM), jnp.float32),
        'cache_v': nrm(ks[3], (DEPTH, DEC_BATCH, win, N_KV, HEAD_DIM), jnp.float32),
        'state_h': 0.5 * nrm(ks[4], (DEPTH, DEC_BATCH, N_REC_HEADS, REC_HEAD_DIM, REC_HEAD_DIM), jnp.float32),
        'meta_tokens': nrm(ks[5], (N_META, D_MODEL), jnp.float32),
        'w_in': nrm(ks[6], (DEPTH, D_MODEL, D_IN), jnp.float32) * D_MODEL ** -0.5,
        'w_out': nrm(ks[7], (DEPTH, D_MIX, D_MODEL), jnp.float32) * D_MIX ** -0.5,
        'norm_w': 1.0 + 0.01 * nrm(ks[8], (DEPTH, D_MODEL), jnp.float32),
        'final_norm_w': 1.0 + 0.01 * nrm(ks[9], (D_MODEL,), jnp.float32),
        'attn_sinks': 0.5 * nrm(ks[10], (DEPTH, N_HEADS), jnp.float32),
        'rel_bias_table': 0.5 * nrm(ks[11], (N_BUCKETS, N_HEADS), jnp.float32),
        'hgrn_lb_logits': 0.5 * nrm(ks[12], (DEPTH, D_REC), jnp.float32),
        'hgrn_norm_w': 1.0 + 0.01 * nrm(ks[13], (DEPTH, D_REC), jnp.float32),
    }


def reference(x_prompt, x_sample, cache_k, cache_v, state_h, meta_tokens, w_in, w_out, norm_w,
              final_norm_w, attn_sinks, rel_bias_table, hgrn_lb_logits, hgrn_norm_w):
    B = x_prompt.shape[0]
    meta = jnp.broadcast_to(meta_tokens[None].astype(x_prompt.dtype), (B, N_META, D_MODEL))
    hp = jnp.concatenate([meta, x_prompt], axis=1)
    hs = x_sample
    bias_d = rel_bias_table[t5_bucket(jnp.arange(WINDOW))].T.astype(jnp.float32)
    pl = jax.nn.softmax(hgrn_lb_logits.astype(jnp.float32), axis=0)
    lb_all = jnp.cumsum(pl, axis=0) - pl[0:1]
    win = cache_k.shape[2]
    pk, pv, ps, sk, sv, ss = [], [], [], [], [], []
    for l in range(DEPTH):
        qa, ka, va, ga, qr, logf, kr, ir, gr = project(hp, norm_w[l], w_in[l], lb_all[l])
        oa = swa_prompt(qa, ka, va, attn_sinks[l], bias_d)
        s0 = jnp.zeros((B, N_REC_HEADS, REC_HEAD_DIM, REC_HEAD_DIM), jnp.float32)
        orec, sp = hgrn2_chunked(qr, logf, kr, ir, s0)
        hp = mix_out(hp, oa, ga, orec, gr, hgrn_norm_w[l], w_out[l])
        pk.append(ka[:, -win:])
        pv.append(va[:, -win:])
        ps.append(sp.astype(state_h.dtype))
        qa, ka, va, ga, qr, logf, kr, ir, gr = project(hs, norm_w[l], w_in[l], lb_all[l])
        oa, kbuf, vbuf = swa_sample(qa, ka, va, cache_k[l], cache_v[l], attn_sinks[l], bias_d)
        orec, s_s = hgrn2_chunked(qr, logf, kr, ir, state_h[l])
        hs = mix_out(hs, oa, ga, orec, gr, hgrn_norm_w[l], w_out[l])
        sk.append(kbuf)
        sv.append(vbuf)
        ss.append(s_s.astype(state_h.dtype))
    y_prompt = rmsnorm(hp[:, N_META:], final_norm_w)
    y_sample = rmsnorm(hs, final_norm_w)
    return (y_prompt, y_sample, jnp.stack(pk), jnp.stack(pv), jnp.stack(ps),
            jnp.stack(sk), jnp.stack(sv), jnp.stack(ss))
```

```python
import functools
import math

import jax
import jax.numpy as jnp
from jax import lax
from jax.experimental import pallas as pl
from jax.experimental.pallas import tpu as pltpu

D_MODEL = 2048
BATCH = 4
SEQ = 2048
DEPTH = 4
DEC_BATCH = 32
DEC_SEQ = 8
N_META = 16
D_ATTN = 1024
D_REC = 1024
HEAD_DIM = 64
N_HEADS = 16
N_KV = 4
GROUP = 4
KV_DIM = 256
WINDOW = 128
N_BUCKETS = 32
MAX_DISTANCE = 128
REC_HEAD_DIM = 128
N_REC_HEADS = 8
EPS = 1e-6
D_IN = 2 * D_ATTN + 2 * KV_DIM + 4 * D_REC

OFF_QA = 0
OFF_KA = D_ATTN
OFF_VA = OFF_KA + KV_DIM
OFF_GA = OFF_VA + KV_DIM
OFF_QR = OFF_GA + D_ATTN
OFF_FR = OFF_QR + D_REC
OFF_IR = OFF_FR + D_REC
OFF_GR = OFF_IR + D_REC

TILE = 128
CHUNK = 64
HALF = CHUNK // 2
QBLK = 128
N_TILES = SEQ // TILE + 1
PAD = TILE - N_META
PROJ_COLS = 512
NEG = -1e30
N_DEC = DEC_BATCH * DEC_SEQ
VMEM_LIMIT = 60 * 1024 * 1024

F32 = jnp.float32
BF16 = jnp.bfloat16


def _sigmoid_pair(x):
    e = jnp.exp(-jnp.abs(x))
    inv = 1.0 / (1.0 + e)
    pos = x >= 0
    return jnp.where(pos, inv, e * inv), jnp.where(pos, e * inv, inv)


def _silu(x):
    s, _ = _sigmoid_pair(x)
    return x * s


def _cumsum_rows(x):
    n = x.shape[0]
    row = lax.broadcasted_iota(jnp.int32, x.shape, 0)
    k = 1
    while k < n:
        x = x + jnp.where(row >= k, pltpu.roll(x, k, axis=0), 0.0)
        k *= 2
    return x


def _rmsnorm_rows(x, w):
    ms = jnp.mean(x * x, axis=-1, keepdims=True)
    return (x * lax.rsqrt(ms + EPS)) * w


def _dot_nt(a, b):
    return lax.dot_general(a, b, (((1,), (1,)), ((), ())), preferred_element_type=F32)


def _dot_tn(a, b):
    return lax.dot_general(a, b, (((0,), (0,)), ((), ())), preferred_element_type=F32)


def _dot(a, b):
    return jnp.dot(a, b, preferred_element_type=F32)


def _gate_features(fr, lb, oml):
    sig, sneg = _sigmoid_pair(fr)
    logf = jnp.log(lb + oml * sig)
    return logf, oml * sneg


def _prompt_layer_kernel(sink_ref, hp_ref, win_ref, wout_ref, normw_ref, lb_ref, oml_ref, recw_ref,
                         bias_ref, hout_ref, pk_ref, pv_ref, ps_ref,
                         xn_ref, z_ref, kv_ref, st_ref, y_ref):
    j = pl.program_id(1)

    @pl.when(j == 0)
    def _():
        kv_ref[0:QBLK, :] = jnp.zeros((QBLK, 2 * KV_DIM), BF16)
        st_ref[...] = jnp.zeros(st_ref.shape, F32)

    x = hp_ref[0]
    xn_ref[...] = _rmsnorm_rows(x, normw_ref[...]).astype(BF16)

    for c0 in range(0, D_IN, PROJ_COLS):
        z_ref[:, c0:c0 + PROJ_COLS] = _dot(xn_ref[...], win_ref[:, c0:c0 + PROJ_COLS])

    kv_ref[QBLK:QBLK + TILE, :] = z_ref[:, OFF_KA:OFF_KA + 2 * KV_DIM].astype(BF16)

    c_tile = j * TILE - PAD

    for qb in range(TILE // QBLK):
        r0 = qb * QBLK
        kpos = c_tile + (r0 - QBLK) + lax.broadcasted_iota(jnp.int32, (1, 2 * QBLK), 1)
        kvalid = kpos >= 0
        for kh in range(N_KV):
            kk = kv_ref[r0:r0 + 2 * QBLK, kh * HEAD_DIM:(kh + 1) * HEAD_DIM]
            vv = kv_ref[r0:r0 + 2 * QBLK, KV_DIM + kh * HEAD_DIM:KV_DIM + (kh + 1) * HEAD_DIM]
            q4 = jnp.concatenate(
                [z_ref[r0:r0 + QBLK, (kh * GROUP + g) * HEAD_DIM:(kh * GROUP + g + 1) * HEAD_DIM]
                 for g in range(GROUP)], axis=0)
            s = _dot_nt((q4 * (HEAD_DIM ** -0.5)).astype(BF16), kk)
            s = s + bias_ref[kh * GROUP * QBLK:(kh + 1) * GROUP * QBLK, :]
            s = jnp.where(kvalid, s, NEG)
            es, dens = [], []
            for g in range(GROUP):
                sink = sink_ref[kh * GROUP + g]
                sg = s[g * QBLK:(g + 1) * QBLK]
                m = jnp.maximum(jnp.max(sg, axis=-1, keepdims=True), sink)
                e = jnp.exp(sg - m)
                dens.append(jnp.sum(e, axis=-1, keepdims=True) + jnp.exp(sink - m))
                es.append(e.astype(BF16))
            o4 = _dot(jnp.concatenate(es, axis=0), vv)
            outs = []
            for g in range(GROUP):
                h = kh * GROUP + g
                og = o4[g * QBLK:(g + 1) * QBLK] / dens[g]
                ga = z_ref[r0:r0 + QBLK, OFF_GA + h * HEAD_DIM:OFF_GA + (h + 1) * HEAD_DIM]
                outs.append(og * _silu(ga))
            y_ref[r0:r0 + QBLK, kh * GROUP * HEAD_DIM:(kh + 1) * GROUP * HEAD_DIM] = (
                jnp.concatenate(outs, axis=-1).astype(BF16))

    row = lax.broadcasted_iota(jnp.int32, (CHUNK, CHUNK), 0)
    col = lax.broadcasted_iota(jnp.int32, (CHUNK, CHUNK), 1)
    causal = col <= row
    for ci in range(TILE // CHUNK):
        r0 = ci * CHUNK
        rpos = c_tile + r0 + lax.broadcasted_iota(jnp.int32, (CHUNK, 1), 0)
        rvalid = rpos >= 0
        for hh in range(N_REC_HEADS):
            cs = hh * REC_HEAD_DIM
            lb = lb_ref[:, cs:cs + REC_HEAD_DIM]
            oml = oml_ref[:, cs:cs + REC_HEAD_DIM]
            logf, kr = _gate_features(z_ref[r0:r0 + CHUNK, OFF_FR + cs:OFF_FR + cs + REC_HEAD_DIM], lb, oml)
            q = _silu(z_ref[r0:r0 + CHUNK, OFF_QR + cs:OFF_QR + cs + REC_HEAD_DIM])
            v = jnp.where(rvalid, z_ref[r0:r0 + CHUNK, OFF_IR + cs:OFF_IR + cs + REC_HEAD_DIM], 0.0)
            vb = v.astype(BF16)
            a = _cumsum_rows(logf)
            a_mid = a[HALF - 1:HALF, :]
            a_end = a[CHUNK - 1:CHUNK, :]
            qh = (q * jnp.exp(a - a_mid)).astype(BF16)
            kh_ = (kr * jnp.exp(a_mid - a)).astype(BF16)
            att = jnp.where(causal, _dot_nt(qh, kh_), 0.0)
            qt = (q * jnp.exp(a)).astype(BF16)
            kt = (kr * jnp.exp(a_end - a)).astype(BF16)
            st = st_ref[hh]
            o = _dot(att.astype(BF16), vb) + _dot_nt(qt, st.astype(BF16))
            st_ref[hh] = st * jnp.exp(a_end) + _dot_tn(vb, kt)
            on = o * lax.rsqrt(jnp.mean(o * o, axis=-1, keepdims=True) + EPS)
            gr = z_ref[r0:r0 + CHUNK, OFF_GR + cs:OFF_GR + cs + REC_HEAD_DIM]
            yr = (on * recw_ref[:, cs:cs + REC_HEAD_DIM]) * _silu(gr)
            y_ref[r0:r0 + CHUNK, D_ATTN + cs:D_ATTN + cs + REC_HEAD_DIM] = yr.astype(BF16)

    hout_ref[0] = x + _dot(y_ref[...], wout_ref[...])

    kv_ref[0:QBLK, :] = kv_ref[TILE:TILE + QBLK, :]

    @pl.when(j == N_TILES - 1)
    def _():
        pk_ref[0] = z_ref[TILE - WINDOW:TILE, OFF_KA:OFF_KA + KV_DIM]
        pv_ref[0] = z_ref[TILE - WINDOW:TILE, OFF_VA:OFF_VA + KV_DIM]
        ps_ref[0] = st_ref[...]


def _resident(shape):
    nd = len(shape)
    return pl.BlockSpec(shape, lambda b, j: (0,) * nd, pipeline_mode=pl.Buffered(1))


def _prompt_layer(hp, w_in, w_out, norm_w, lb, oml, rec_w, sinks, bias):
    lp = N_TILES * TILE
    return pl.pallas_call(
        _prompt_layer_kernel,
        grid=(BATCH, N_TILES),
        in_specs=[
            pl.BlockSpec(memory_space=pltpu.SMEM),
            pl.BlockSpec((1, TILE, D_MODEL), lambda b, j: (b, j, 0)),
            _resident((D_MODEL, D_IN)),
            _resident((D_MODEL, D_MODEL)),
            _resident((1, D_MODEL)),
            _resident((1, D_REC)),
            _resident((1, D_REC)),
            _resident((1, D_REC)),
            _resident((N_HEADS * QBLK, 2 * QBLK)),
        ],
        out_specs=[
            pl.BlockSpec((1, TILE, D_MODEL), lambda b, j: (b, j, 0)),
            pl.BlockSpec((1, WINDOW, KV_DIM), lambda b, j: (b, 0, 0)),
            pl.BlockSpec((1, WINDOW, KV_DIM), lambda b, j: (b, 0, 0)),
            pl.BlockSpec((1, N_REC_HEADS, REC_HEAD_DIM, REC_HEAD_DIM), lambda b, j: (b, 0, 0, 0)),
        ],
        out_shape=[
            jax.ShapeDtypeStruct((BATCH, lp, D_MODEL), F32),
            jax.ShapeDtypeStruct((BATCH, WINDOW, KV_DIM), F32),
            jax.ShapeDtypeStruct((BATCH, WINDOW, KV_DIM), F32),
            jax.ShapeDtypeStruct((BATCH, N_REC_HEADS, REC_HEAD_DIM, REC_HEAD_DIM), F32),
        ],
        scratch_shapes=[
            pltpu.VMEM((TILE, D_MODEL), BF16),
            pltpu.VMEM((TILE, D_IN), F32),
            pltpu.VMEM((QBLK + TILE, 2 * KV_DIM), BF16),
            pltpu.VMEM((N_REC_HEADS, REC_HEAD_DIM, REC_HEAD_DIM), F32),
            pltpu.VMEM((TILE, D_MODEL), BF16),
        ],
        compiler_params=pltpu.CompilerParams(
            dimension_semantics=("arbitrary", "arbitrary"), vmem_limit_bytes=VMEM_LIMIT),
        name="prompt_layer",
    )(sinks, hp, w_in, w_out, norm_w, lb, oml, rec_w, bias)


def _dec_proj_kernel(hs_ref, normw_ref, win_ref, z_ref):
    xn = _rmsnorm_rows(hs_ref[...], normw_ref[...]).astype(BF16)
    z_ref[...] = _dot(xn, win_ref[...])


def _dec_proj(hs, norm_w, w_in):
    return pl.pallas_call(
        _dec_proj_kernel,
        grid=(D_IN // PROJ_COLS,),
        in_specs=[
            pl.BlockSpec((N_DEC, D_MODEL), lambda n: (0, 0)),
            pl.BlockSpec((1, D_MODEL), lambda n: (0, 0)),
            pl.BlockSpec((D_MODEL, PROJ_COLS), lambda n: (0, n)),
        ],
        out_specs=pl.BlockSpec((N_DEC, PROJ_COLS), lambda n: (0, n)),
        out_shape=jax.ShapeDtypeStruct((N_DEC, D_IN), F32),
        compiler_params=pltpu.CompilerParams(dimension_semantics=("arbitrary",)),
        name="dec_proj",
    )(hs, norm_w, w_in)


def _dec_out_kernel(hs_ref, y_ref, wout_ref, o_ref):
    o_ref[...] = hs_ref[...] + _dot(y_ref[...], wout_ref[...])


def _dec_out(hs, y, w_out):
    return pl.pallas_call(
        _dec_out_kernel,
        grid=(D_MODEL // PROJ_COLS,),
        in_specs=[
            pl.BlockSpec((N_DEC, PROJ_COLS), lambda n: (0, n)),
            pl.BlockSpec((N_DEC, D_MODEL), lambda n: (0, 0)),
            pl.BlockSpec((D_MODEL, PROJ_COLS), lambda n: (0, n)),
        ],
        out_specs=pl.BlockSpec((N_DEC, PROJ_COLS), lambda n: (0, n)),
        out_shape=jax.ShapeDtypeStruct((N_DEC, D_MODEL), F32),
        compiler_params=pltpu.CompilerParams(dimension_semantics=("arbitrary",)),
        name="dec_out",
    )(hs, y, w_out)


def _dec_mix_kernel(sink_ref, z_ref, ck_ref, cv_ref, s0_ref, lb_ref, oml_ref, recw_ref, biasc_ref, biasn_ref,
                    y_ref, nk_ref, nv_ref, ns_ref):
    t = DEC_SEQ
    k_new = z_ref[:, OFF_KA:OFF_KA + KV_DIM]
    v_new = z_ref[:, OFF_VA:OFF_VA + KV_DIM]
    nk_ref[0, 0:WINDOW - t, :] = ck_ref[0, t:WINDOW, :]
    nk_ref[0, WINDOW - t:WINDOW, :] = k_new
    nv_ref[0, 0:WINDOW - t, :] = cv_ref[0, t:WINDOW, :]
    nv_ref[0, WINDOW - t:WINDOW, :] = v_new

    for kh in range(N_KV):
        hs_ = slice(kh * HEAD_DIM, (kh + 1) * HEAD_DIM)
        kc = ck_ref[0, :, hs_].astype(BF16)
        vc = cv_ref[0, :, hs_].astype(BF16)
        kn = k_new[:, hs_]
        vn = v_new[:, hs_]
        q4 = jnp.concatenate(
            [z_ref[:, (kh * GROUP + g) * HEAD_DIM:(kh * GROUP + g + 1) * HEAD_DIM] for g in range(GROUP)],
            axis=0) * (HEAD_DIM ** -0.5)
        rows = slice(kh * GROUP * t, (kh + 1) * GROUP * t)
        sc = _dot_nt(q4.astype(BF16), kc) + biasc_ref[rows, :]
        sn = _dot_nt(q4, kn) + biasn_ref[rows, :]
        outs = []
        for g in range(GROUP):
            h = kh * GROUP + g
            sink = sink_ref[h]
            scg = sc[g * t:(g + 1) * t]
            sng = sn[g * t:(g + 1) * t]
            m = jnp.maximum(jnp.maximum(jnp.max(scg, axis=-1, keepdims=True),
                                        jnp.max(sng, axis=-1, keepdims=True)), sink)
            ec = jnp.exp(scg - m)
            en = jnp.exp(sng - m)
            den = (jnp.sum(ec, axis=-1, keepdims=True) + jnp.sum(en, axis=-1, keepdims=True)
                   + jnp.exp(sink - m))
            og = (_dot(ec, vc.astype(F32)) + _dot(en, vn)) / den
            ga = z_ref[:, OFF_GA + h * HEAD_DIM:OFF_GA + (h + 1) * HEAD_DIM]
            outs.append(og * _silu(ga))
        y_ref[:, kh * GROUP * HEAD_DIM:(kh + 1) * GROUP * HEAD_DIM] = (
            jnp.concatenate(outs, axis=-1).astype(BF16))

    row = lax.broadcasted_iota(jnp.int32, (t, t), 0)
    col = lax.broadcasted_iota(jnp.int32, (t, t), 1)
    causal = col <= row
    for hh in range(N_REC_HEADS):
        cs = hh * REC_HEAD_DIM
        lb = lb_ref[:, cs:cs + REC_HEAD_DIM]
        oml = oml_ref[:, cs:cs + REC_HEAD_DIM]
        logf, kr = _gate_features(z_ref[:, OFF_FR + cs:OFF_FR + cs + REC_HEAD_DIM], lb, oml)
        q = _silu(z_ref[:, OFF_QR + cs:OFF_QR + cs + REC_HEAD_DIM])
        v = z_ref[:, OFF_IR + cs:OFF_IR + cs + REC_HEAD_DIM]
        a = _cumsum_rows(logf)
        a_mid = a[t // 2 - 1:t // 2, :]
        a_end = a[t - 1:t, :]
        att = jnp.where(causal, _dot_nt(q * jnp.exp(a - a_mid), kr * jnp.exp(a_mid - a)), 0.0)
        qt = q * jnp.exp(a)
        kt = kr * jnp.exp(a_end - a)
        s0 = s0_ref[0, hh]
        o = _dot(att, v) + _dot(qt, s0)
        ktd = jnp.concatenate([kt, jnp.exp(a_end), jnp.zeros((t - 1, REC_HEAD_DIM), F32)], axis=0).T
        ns_ref[0, hh] = s0 * ktd[:, t:t + 1] + _dot(ktd[:, 0:t], v)
        on = o * lax.rsqrt(jnp.mean(o * o, axis=-1, keepdims=True) + EPS)
        gr = z_ref[:, OFF_GR + cs:OFF_GR + cs + REC_HEAD_DIM]
        yr = (on * recw_ref[:, cs:cs + REC_HEAD_DIM]) * _silu(gr)
        y_ref[:, D_ATTN + cs:D_ATTN + cs + REC_HEAD_DIM] = yr.astype(BF16)


def _dec_mix(z, ck, cv, s0, lb, oml, rec_w, sinks, bias_c, bias_n):
    t = DEC_SEQ
    const2 = lambda s: (0, 0)
    return pl.pallas_call(
        _dec_mix_kernel,
        grid=(DEC_BATCH,),
        in_specs=[
            pl.BlockSpec(memory_space=pltpu.SMEM),
            pl.BlockSpec((t, D_IN), lambda s: (s, 0)),
            pl.BlockSpec((1, WINDOW, KV_DIM), lambda s: (s, 0, 0)),
            pl.BlockSpec((1, WINDOW, KV_DIM), lambda s: (s, 0, 0)),
            pl.BlockSpec((1, N_REC_HEADS, REC_HEAD_DIM, REC_HEAD_DIM), lambda s: (s, 0, 0, 0)),
            pl.BlockSpec((1, D_REC), const2),
            pl.BlockSpec((1, D_REC), const2),
            pl.BlockSpec((1, D_REC), const2),
            pl.BlockSpec((N_HEADS * t, WINDOW), const2),
            pl.BlockSpec((N_HEADS * t, t), const2),
        ],
        out_specs=[
            pl.BlockSpec((t, D_MODEL), lambda s: (s, 0)),
            pl.BlockSpec((1, WINDOW, KV_DIM), lambda s: (s, 0, 0)),
            pl.BlockSpec((1, WINDOW, KV_DIM), lambda s: (s, 0, 0)),
            pl.BlockSpec((1, N_REC_HEADS, REC_HEAD_DIM, REC_HEAD_DIM), lambda s: (s, 0, 0, 0)),
        ],
        out_shape=[
            jax.ShapeDtypeStruct((N_DEC, D_MODEL), BF16),
            jax.ShapeDtypeStruct((DEC_BATCH, WINDOW, KV_DIM), F32),
            jax.ShapeDtypeStruct((DEC_BATCH, WINDOW, KV_DIM), F32),
            jax.ShapeDtypeStruct((DEC_BATCH, N_REC_HEADS, REC_HEAD_DIM, REC_HEAD_DIM), F32),
        ],
        compiler_params=pltpu.CompilerParams(dimension_semantics=("arbitrary",)),
        name="dec_mix",
    )(sinks, z, ck, cv, s0, lb, oml, rec_w, bias_c, bias_n)


def _final_norm_kernel(x_ref, w_ref, o_ref):
    o_ref[...] = _rmsnorm_rows(x_ref[...], w_ref[...])


def _final_norm_prompt(hp, w):
    return pl.pallas_call(
        _final_norm_kernel,
        grid=(BATCH, SEQ // TILE),
        in_specs=[pl.BlockSpec((1, TILE, D_MODEL), lambda b, j: (b, j + 1, 0)),
                  pl.BlockSpec((1, 1, D_MODEL), lambda b, j: (0, 0, 0))],
        out_specs=pl.BlockSpec((1, TILE, D_MODEL), lambda b, j: (b, j, 0)),
        out_shape=jax.ShapeDtypeStruct((BATCH, SEQ, D_MODEL), F32),
        compiler_params=pltpu.CompilerParams(dimension_semantics=("arbitrary", "arbitrary")),
        name="final_norm_prompt",
    )(hp, w.reshape(1, 1, D_MODEL))


def _final_norm_dec(hs, w):
    return pl.pallas_call(
        _final_norm_kernel,
        out_shape=jax.ShapeDtypeStruct((N_DEC, D_MODEL), F32),
        name="final_norm_dec",
    )(hs, w.reshape(1, D_MODEL))


def _t5_bucket(dist):
    max_exact = N_BUCKETS // 2
    d = jnp.maximum(dist, 0)
    df = jnp.maximum(d, 1).astype(F32)
    large = max_exact + (jnp.log(df / max_exact) / math.log(MAX_DISTANCE / max_exact)
                         * (N_BUCKETS - max_exact)).astype(jnp.int32)
    large = jnp.minimum(large, N_BUCKETS - 1)
    return jnp.where(d < max_exact, d, large)


def _bias_tables(rel_bias_table):
    bias_d = rel_bias_table[_t5_bucket(jnp.arange(WINDOW))].T.astype(F32)

    def expand(dist):
        valid = (dist >= 0) & (dist < WINDOW)
        tab = bias_d[:, jnp.clip(dist, 0, WINDOW - 1)]
        return jnp.where(valid[None], tab, NEG)

    qi = jnp.arange(QBLK)[:, None]
    prompt = expand(QBLK + qi - jnp.arange(2 * QBLK)[None, :]).reshape(N_HEADS * QBLK, 2 * QBLK)
    ti = jnp.arange(DEC_SEQ)[:, None]
    dec = expand(ti - (jnp.arange(WINDOW + DEC_SEQ)[None, :] - WINDOW)).reshape(
        N_HEADS * DEC_SEQ, WINDOW + DEC_SEQ)
    return prompt, dec[:, :WINDOW], dec[:, WINDOW:]


def kernel(x_prompt, x_sample, cache_k, cache_v, state_h, meta_tokens, w_in, w_out, norm_w, final_norm_w,
           attn_sinks, rel_bias_table, hgrn_lb_logits, hgrn_norm_w):
    w_in_b = w_in.astype(BF16)
    w_out_b = w_out.astype(BF16)
    pl_ = jax.nn.softmax(hgrn_lb_logits.astype(F32), axis=0)
    lb_all = jnp.cumsum(pl_, axis=0) - pl_[0:1]
    bias_p, bias_dc, bias_dn = _bias_tables(rel_bias_table)

    meta = jnp.broadcast_to(meta_tokens[None].astype(F32), (BATCH, N_META, D_MODEL))
    hp = jnp.concatenate([jnp.zeros((BATCH, PAD, D_MODEL), F32), meta, x_prompt], axis=1)
    hs = x_sample.reshape(N_DEC, D_MODEL)
    ck = cache_k.reshape(DEPTH, DEC_BATCH, WINDOW, KV_DIM)
    cv = cache_v.reshape(DEPTH, DEC_BATCH, WINDOW, KV_DIM)

    pk, pv, ps, sk, sv, ss = [], [], [], [], [], []
    for l in range(DEPTH):
        nw = norm_w[l].reshape(1, D_MODEL)
        lb = lb_all[l].reshape(1, D_REC)
        oml = 1.0 - lb
        rw = hgrn_norm_w[l].reshape(1, D_REC)
        sinks = attn_sinks[l].astype(F32)

        hp, k_l, v_l, s_l = _prompt_layer(hp, w_in_b[l], w_out_b[l], nw, lb, oml, rw, sinks, bias_p)
        pk.append(k_l)
        pv.append(v_l)
        ps.append(s_l)

        z = _dec_proj(hs, nw, w_in_b[l])
        y, nk, nv, ns = _dec_mix(z, ck[l], cv[l], state_h[l], lb, oml, rw, sinks, bias_dc, bias_dn)
        hs = _dec_out(hs, y, w_out_b[l])
        sk.append(nk)
        sv.append(nv)
        ss.append(ns)

    y_prompt = _final_norm_prompt(hp, final_norm_w)
    y_sample = _final_norm_dec(hs, final_norm_w).reshape(DEC_BATCH, DEC_SEQ, D_MODEL)
    kv_shape = (DEPTH, -1, WINDOW, N_KV, HEAD_DIM)
    return (y_prompt, y_sample,
            jnp.stack(pk).reshape(kv_shape), jnp.stack(pv).reshape(kv_shape),
            jnp.swapaxes(jnp.stack(ps), -1, -2),
            jnp.stack(sk).reshape(kv_shape), jnp.stack(sv).reshape(kv_shape), jnp.stack(ss))
```

```python
import functools
import math

import jax
import jax.numpy as jnp
from jax import lax
from jax.experimental import pallas as pl
from jax.experimental.pallas import tpu as pltpu

D_MODEL = 2048
BATCH = 4
SEQ = 2048
DEPTH = 4
DEC_BATCH = 32
DEC_SEQ = 8
N_META = 16
D_ATTN = 1024
D_REC = 1024
HEAD_DIM = 64
N_HEADS = 16
N_KV = 4
GROUP = 4
KV_DIM = 256
WINDOW = 128
N_BUCKETS = 32
MAX_DISTANCE = 128
REC_HEAD_DIM = 128
N_REC_HEADS = 8
EPS = 1e-6
D_IN = 2 * D_ATTN + 2 * KV_DIM + 4 * D_REC

OFF_QA = 0
OFF_KA = D_ATTN
OFF_VA = OFF_KA + KV_DIM
OFF_GA = OFF_VA + KV_DIM
OFF_QR = OFF_GA + D_ATTN
OFF_FR = OFF_QR + D_REC
OFF_IR = OFF_FR + D_REC
OFF_GR = OFF_IR + D_REC

TILE = 128
CHUNK = 64
HALF = CHUNK // 2
N_TILES = SEQ // TILE + 1
PAD = TILE - N_META
PROJ_COLS = 512
NEG = -1e30
N_DEC = DEC_BATCH * DEC_SEQ
VMEM_LIMIT = 60 * 1024 * 1024

F32 = jnp.float32
BF16 = jnp.bfloat16


def _sigmoid_pair(x):
    e = jnp.exp(-jnp.abs(x))
    inv = 1.0 / (1.0 + e)
    pos = x >= 0
    return jnp.where(pos, inv, e * inv), jnp.where(pos, e * inv, inv)


def _silu(x):
    s, _ = _sigmoid_pair(x)
    return x * s


def _cumsum_rows(x):
    n = x.shape[0]
    row = lax.broadcasted_iota(jnp.int32, x.shape, 0)
    k = 1
    while k < n:
        x = x + jnp.where(row >= k, pltpu.roll(x, k, axis=0), 0.0)
        k *= 2
    return x


def _rmsnorm_rows(x, w):
    ms = jnp.mean(x * x, axis=-1, keepdims=True)
    return (x * lax.rsqrt(ms + EPS)) * w


def _dot_nt(a, b):
    return lax.dot_general(a, b, (((1,), (1,)), ((), ())), preferred_element_type=F32)


def _dot_tn(a, b):
    return lax.dot_general(a, b, (((0,), (0,)), ((), ())), preferred_element_type=F32)


def _dot(a, b):
    return jnp.dot(a, b, preferred_element_type=F32)


def _gate_features(fr, lb, oml):
    sig, sneg = _sigmoid_pair(fr)
    logf = jnp.log(lb + oml * sig)
    return logf, oml * sneg


def _prompt_layer_kernel(layer, final, sink_ref, hmeta_ref, hmain_ref, win_ref, wout_ref, normw_ref, lb_ref,
                         oml_ref, recw_ref, bias_ref, finw_ref, *refs):
    if final:
        hmain_out, pk_ref, pv_ref, ps_ref, xn_ref, z_ref, kv_ref, st_ref, y_ref = refs
    else:
        hmeta_out, hmain_out, pk_ref, pv_ref, ps_ref, xn_ref, z_ref, kv_ref, st_ref, y_ref = refs
    j = pl.program_id(1)

    @pl.when(j == 0)
    def _():
        kv_ref[0:TILE, :] = jnp.zeros((TILE, 2 * KV_DIM), BF16)
        st_ref[...] = jnp.zeros(st_ref.shape, F32)

    def tile_input():
        return jnp.where(j == 0, hmeta_ref[0], hmain_ref[0])

    xn_ref[...] = _rmsnorm_rows(tile_input(), normw_ref[...]).astype(BF16)

    for c0 in range(0, D_IN, PROJ_COLS):
        z_ref[:, c0:c0 + PROJ_COLS] = _dot(xn_ref[...], win_ref[:, c0:c0 + PROJ_COLS])

    kv_ref[TILE:2 * TILE, :] = z_ref[:, OFF_KA:OFF_KA + 2 * KV_DIM].astype(BF16)

    c_tile = j * TILE - PAD

    kpos = c_tile - TILE + lax.broadcasted_iota(jnp.int32, (1, 2 * TILE), 1)
    kvalid = kpos >= 0
    for kh in range(N_KV):
        kk = kv_ref[:, kh * HEAD_DIM:(kh + 1) * HEAD_DIM]
        vv = kv_ref[:, KV_DIM + kh * HEAD_DIM:KV_DIM + (kh + 1) * HEAD_DIM]
        q4 = jnp.concatenate(
            [z_ref[:, (kh * GROUP + g) * HEAD_DIM:(kh * GROUP + g + 1) * HEAD_DIM] for g in range(GROUP)],
            axis=0)
        s = _dot_nt((q4 * (HEAD_DIM ** -0.5)).astype(BF16), kk)
        s = s + bias_ref[kh * GROUP * TILE:(kh + 1) * GROUP * TILE, :]
        s = jnp.where(kvalid, s, NEG)
        es, dens = [], []
        for g in range(GROUP):
            sink = sink_ref[layer, kh * GROUP + g]
            sg = s[g * TILE:(g + 1) * TILE]
            m = jnp.maximum(jnp.max(sg, axis=-1, keepdims=True), sink)
            e = jnp.exp(sg - m)
            dens.append(jnp.sum(e, axis=-1, keepdims=True) + jnp.exp(sink - m))
            es.append(e.astype(BF16))
        o4 = _dot(jnp.concatenate(es, axis=0), vv)
        outs = []
        for g in range(GROUP):
            h = kh * GROUP + g
            og = o4[g * TILE:(g + 1) * TILE] / dens[g]
            ga = z_ref[:, OFF_GA + h * HEAD_DIM:OFF_GA + (h + 1) * HEAD_DIM]
            outs.append(og * _silu(ga))
        y_ref[:, kh * GROUP * HEAD_DIM:(kh + 1) * GROUP * HEAD_DIM] = (
            jnp.concatenate(outs, axis=-1).astype(BF16))

    row = lax.broadcasted_iota(jnp.int32, (CHUNK, CHUNK), 0)
    col = lax.broadcasted_iota(jnp.int32, (CHUNK, CHUNK), 1)
    causal = col <= row
    for ci in range(TILE // CHUNK):
        r0 = ci * CHUNK
        rpos = c_tile + r0 + lax.broadcasted_iota(jnp.int32, (CHUNK, 1), 0)
        rvalid = rpos >= 0
        for hh in range(N_REC_HEADS):
            cs = hh * REC_HEAD_DIM
            lb = lb_ref[:, cs:cs + REC_HEAD_DIM]
            oml = oml_ref[:, cs:cs + REC_HEAD_DIM]
            logf, kr = _gate_features(z_ref[r0:r0 + CHUNK, OFF_FR + cs:OFF_FR + cs + REC_HEAD_DIM], lb, oml)
            q = _silu(z_ref[r0:r0 + CHUNK, OFF_QR + cs:OFF_QR + cs + REC_HEAD_DIM])
            v = jnp.where(rvalid, z_ref[r0:r0 + CHUNK, OFF_IR + cs:OFF_IR + cs + REC_HEAD_DIM], 0.0)
            vb = v.astype(BF16)
            a = _cumsum_rows(logf)
            a_mid = a[HALF - 1:HALF, :]
            a_end = a[CHUNK - 1:CHUNK, :]
            qh = (q * jnp.exp(a - a_mid)).astype(BF16)
            kh_ = (kr * jnp.exp(a_mid - a)).astype(BF16)
            att = jnp.where(causal, _dot_nt(qh, kh_), 0.0)
            qt = (q * jnp.exp(a)).astype(BF16)
            kt = (kr * jnp.exp(a_end - a)).astype(BF16)
            st = st_ref[hh]
            o = _dot(att.astype(BF16), vb) + _dot_nt(qt, st.astype(BF16))
            st_ref[hh] = st * jnp.exp(a_end) + _dot_tn(vb, kt)
            on = o * lax.rsqrt(jnp.mean(o * o, axis=-1, keepdims=True) + EPS)
            gr = z_ref[r0:r0 + CHUNK, OFF_GR + cs:OFF_GR + cs + REC_HEAD_DIM]
            yr = (on * recw_ref[:, cs:cs + REC_HEAD_DIM]) * _silu(gr)
            y_ref[r0:r0 + CHUNK, D_ATTN + cs:D_ATTN + cs + REC_HEAD_DIM] = yr.astype(BF16)

    h_new = tile_input() + _dot(y_ref[...], wout_ref[...])

    kv_ref[0:TILE, :] = kv_ref[TILE:2 * TILE, :]

    if final:
        @pl.when(j > 0)
        def _():
            hmain_out[0] = _rmsnorm_rows(h_new, finw_ref[...])
    else:
        @pl.when(j == 0)
        def _():
            hmeta_out[0] = h_new

        @pl.when(j > 0)
        def _():
            hmain_out[0] = h_new

    @pl.when(j == N_TILES - 1)
    def _():
        pk_ref[0] = z_ref[TILE - WINDOW:TILE, OFF_KA:OFF_KA + KV_DIM]
        pv_ref[0] = z_ref[TILE - WINDOW:TILE, OFF_VA:OFF_VA + KV_DIM]
        ps_ref[0] = st_ref[...]


def _layer_block(shape, layer):
    nd = len(shape)
    return pl.BlockSpec((None,) + tuple(shape), lambda b, j: (layer,) + (0,) * nd, pipeline_mode=pl.Buffered(1))


def _const_block(shape):
    nd = len(shape)
    return pl.BlockSpec(tuple(shape), lambda b, j: (0,) * nd, pipeline_mode=pl.Buffered(1))


def _prompt_layer(layer, h_meta, h_main, w_in, w_out, norm_w, lb, oml, rec_w, sinks, bias, fin_w):
    final = layer == DEPTH - 1
    main_spec = pl.BlockSpec((1, TILE, D_MODEL), lambda b, j: (b, jnp.maximum(j - 1, 0), 0))
    meta_spec = pl.BlockSpec((1, TILE, D_MODEL), lambda b, j: (b, 0, 0))
    kv_spec = pl.BlockSpec((1, WINDOW, KV_DIM), lambda b, j: (b, 0, 0))
    st_spec = pl.BlockSpec((1, N_REC_HEADS, REC_HEAD_DIM, REC_HEAD_DIM), lambda b, j: (b, 0, 0, 0))
    main_shape = jax.ShapeDtypeStruct((BATCH, SEQ, D_MODEL), F32)
    meta_shape = jax.ShapeDtypeStruct((BATCH, TILE, D_MODEL), F32)
    tail_shapes = [
        jax.ShapeDtypeStruct((BATCH, WINDOW, KV_DIM), F32),
        jax.ShapeDtypeStruct((BATCH, WINDOW, KV_DIM), F32),
        jax.ShapeDtypeStruct((BATCH, N_REC_HEADS, REC_HEAD_DIM, REC_HEAD_DIM), F32),
    ]
    out_specs = ([] if final else [meta_spec]) + [main_spec, kv_spec, kv_spec, st_spec]
    out_shape = ([] if final else [meta_shape]) + [main_shape] + tail_shapes
    return pl.pallas_call(
        functools.partial(_prompt_layer_kernel, layer, final),
        grid=(BATCH, N_TILES),
        in_specs=[
            pl.BlockSpec(memory_space=pltpu.SMEM),
            meta_spec,
            main_spec,
            _layer_block((D_MODEL, D_IN), layer),
            _layer_block((D_MODEL, D_MODEL), layer),
            _layer_block((1, D_MODEL), layer),
            _layer_block((1, D_REC), layer),
            _layer_block((1, D_REC), layer),
            _layer_block((1, D_REC), layer),
            _const_block((N_HEADS * TILE, 2 * TILE)),
            _const_block((1, D_MODEL)),
        ],
        out_specs=out_specs,
        out_shape=out_shape,
        scratch_shapes=[
            pltpu.VMEM((TILE, D_MODEL), BF16),
            pltpu.VMEM((TILE, D_IN), F32),
            pltpu.VMEM((2 * TILE, 2 * KV_DIM), BF16),
            pltpu.VMEM((N_REC_HEADS, REC_HEAD_DIM, REC_HEAD_DIM), F32),
            pltpu.VMEM((TILE, D_MODEL), BF16),
        ],
        compiler_params=pltpu.CompilerParams(
            dimension_semantics=("arbitrary", "arbitrary"), vmem_limit_bytes=VMEM_LIMIT),
        name="prompt_layer",
    )(sinks, h_meta, h_main, w_in, w_out, norm_w, lb, oml, rec_w, bias, fin_w)


def _dec_proj_kernel(hs_ref, normw_ref, win_ref, z_ref):
    xn = _rmsnorm_rows(hs_ref[...], normw_ref[...]).astype(BF16)
    z_ref[...] = _dot(xn, win_ref[...])


def _dec_proj(layer, hs, norm_w, w_in):
    return pl.pallas_call(
        _dec_proj_kernel,
        grid=(D_IN // PROJ_COLS,),
        in_specs=[
            pl.BlockSpec((N_DEC, D_MODEL), lambda n: (0, 0)),
            pl.BlockSpec((None, 1, D_MODEL), lambda n: (layer, 0, 0)),
            pl.BlockSpec((None, D_MODEL, PROJ_COLS), lambda n: (layer, 0, n)),
        ],
        out_specs=pl.BlockSpec((N_DEC, PROJ_COLS), lambda n: (0, n)),
        out_shape=jax.ShapeDtypeStruct((N_DEC, D_IN), F32),
        compiler_params=pltpu.CompilerParams(dimension_semantics=("arbitrary",)),
        name="dec_proj",
    )(hs, norm_w, w_in)


def _dec_out_kernel(hs_ref, y_ref, wout_ref, o_ref):
    o_ref[...] = hs_ref[...] + _dot(y_ref[...], wout_ref[...])


def _dec_out(layer, hs, y, w_out):
    return pl.pallas_call(
        _dec_out_kernel,
        grid=(D_MODEL // PROJ_COLS,),
        in_specs=[
            pl.BlockSpec((N_DEC, PROJ_COLS), lambda n: (0, n)),
            pl.BlockSpec((N_DEC, D_MODEL), lambda n: (0, 0)),
            pl.BlockSpec((None, D_MODEL, PROJ_COLS), lambda n: (layer, 0, n)),
        ],
        out_specs=pl.BlockSpec((N_DEC, PROJ_COLS), lambda n: (0, n)),
        out_shape=jax.ShapeDtypeStruct((N_DEC, D_MODEL), F32),
        compiler_params=pltpu.CompilerParams(dimension_semantics=("arbitrary",)),
        name="dec_out",
    )(hs, y, w_out)


def _dec_mix_kernel(layer, sink_ref, z_ref, ck_ref, cv_ref, s0_ref, lb_ref, oml_ref, recw_ref, biasc_ref,
                    biasn_ref, y_ref, nk_ref, nv_ref, ns_ref):
    t = DEC_SEQ
    k_new = z_ref[:, OFF_KA:OFF_KA + KV_DIM]
    v_new = z_ref[:, OFF_VA:OFF_VA + KV_DIM]
    nk_ref[0, 0:WINDOW - t, :] = ck_ref[0, t:WINDOW, :]
    nk_ref[0, WINDOW - t:WINDOW, :] = k_new
    nv_ref[0, 0:WINDOW - t, :] = cv_ref[0, t:WINDOW, :]
    nv_ref[0, WINDOW - t:WINDOW, :] = v_new

    for kh in range(N_KV):
        hs_ = slice(kh * HEAD_DIM, (kh + 1) * HEAD_DIM)
        kc = ck_ref[0, :, hs_].astype(BF16)
        vc = cv_ref[0, :, hs_].astype(BF16)
        kn = k_new[:, hs_]
        vn = v_new[:, hs_]
        q4 = jnp.concatenate(
            [z_ref[:, (kh * GROUP + g) * HEAD_DIM:(kh * GROUP + g + 1) * HEAD_DIM] for g in range(GROUP)],
            axis=0) * (HEAD_DIM ** -0.5)
        rows = slice(kh * GROUP * t, (kh + 1) * GROUP * t)
        sc = _dot_nt(q4.astype(BF16), kc) + biasc_ref[rows, :]
        sn = _dot_nt(q4, kn) + biasn_ref[rows, :]
        outs = []
        for g in range(GROUP):
            h = kh * GROUP + g
            sink = sink_ref[layer, h]
            scg = sc[g * t:(g + 1) * t]
            sng = sn[g * t:(g + 1) * t]
            m = jnp.maximum(jnp.maximum(jnp.max(scg, axis=-1, keepdims=True),
                                        jnp.max(sng, axis=-1, keepdims=True)), sink)
            ec = jnp.exp(scg - m)
            en = jnp.exp(sng - m)
            den = (jnp.sum(ec, axis=-1, keepdims=True) + jnp.sum(en, axis=-1, keepdims=True)
                   + jnp.exp(sink - m))
            og = (_dot(ec, vc.astype(F32)) + _dot(en, vn)) / den
            ga = z_ref[:, OFF_GA + h * HEAD_DIM:OFF_GA + (h + 1) * HEAD_DIM]
            outs.append(og * _silu(ga))
        y_ref[:, kh * GROUP * HEAD_DIM:(kh + 1) * GROUP * HEAD_DIM] = (
            jnp.concatenate(outs, axis=-1).astype(BF16))

    row = lax.broadcasted_iota(jnp.int32, (t, t), 0)
    col = lax.broadcasted_iota(jnp.int32, (t, t), 1)
    causal = col <= row
    for hh in range(N_REC_HEADS):
        cs = hh * REC_HEAD_DIM
        lb = lb_ref[:, cs:cs + REC_HEAD_DIM]
        oml = oml_ref[:, cs:cs + REC_HEAD_DIM]
        logf, kr = _gate_features(z_ref[:, OFF_FR + cs:OFF_FR + cs + REC_HEAD_DIM], lb, oml)
        q = _silu(z_ref[:, OFF_QR + cs:OFF_QR + cs + REC_HEAD_DIM])
        v = z_ref[:, OFF_IR + cs:OFF_IR + cs + REC_HEAD_DIM]
        a = _cumsum_rows(logf)
        a_mid = a[t // 2 - 1:t // 2, :]
        a_end = a[t - 1:t, :]
        att = jnp.where(causal, _dot_nt(q * jnp.exp(a - a_mid), kr * jnp.exp(a_mid - a)), 0.0)
        qt = q * jnp.exp(a)
        kt = kr * jnp.exp(a_end - a)
        s0 = s0_ref[0, hh]
        o = _dot(att, v) + _dot(qt, s0)
        ktd = jnp.concatenate([kt, jnp.exp(a_end), jnp.zeros((t - 1, REC_HEAD_DIM), F32)], axis=0).T
        ns_ref[0, hh] = s0 * ktd[:, t:t + 1] + _dot(ktd[:, 0:t], v)
        on = o * lax.rsqrt(jnp.mean(o * o, axis=-1, keepdims=True) + EPS)
        gr = z_ref[:, OFF_GR + cs:OFF_GR + cs + REC_HEAD_DIM]
        yr = (on * recw_ref[:, cs:cs + REC_HEAD_DIM]) * _silu(gr)
        y_ref[:, D_ATTN + cs:D_ATTN + cs + REC_HEAD_DIM] = yr.astype(BF16)


def _dec_mix(layer, z, ck, cv, s0, lb, oml, rec_w, sinks, bias_c, bias_n):
    t = DEC_SEQ
    const2 = lambda s: (0, 0)
    per_layer = lambda s: (layer, 0, 0)
    kv_in = pl.BlockSpec((None, 1, WINDOW, KV_DIM), lambda s: (layer, s, 0, 0))
    st_in = pl.BlockSpec((None, 1, N_REC_HEADS, REC_HEAD_DIM, REC_HEAD_DIM), lambda s: (layer, s, 0, 0, 0))
    return pl.pallas_call(
        functools.partial(_dec_mix_kernel, layer),
        grid=(DEC_BATCH,),
        in_specs=[
            pl.BlockSpec(memory_space=pltpu.SMEM),
            pl.BlockSpec((t, D_IN), lambda s: (s, 0)),
            kv_in,
            kv_in,
            st_in,
            pl.BlockSpec((None, 1, D_REC), per_layer),
            pl.BlockSpec((None, 1, D_REC), per_layer),
            pl.BlockSpec((None, 1, D_REC), per_layer),
            pl.BlockSpec((N_HEADS * t, WINDOW), const2),
            pl.BlockSpec((N_HEADS * t, t), const2),
        ],
        out_specs=[
            pl.BlockSpec((t, D_MODEL), lambda s: (s, 0)),
            pl.BlockSpec((1, WINDOW, KV_DIM), lambda s: (s, 0, 0)),
            pl.BlockSpec((1, WINDOW, KV_DIM), lambda s: (s, 0, 0)),
            pl.BlockSpec((1, N_REC_HEADS, REC_HEAD_DIM, REC_HEAD_DIM), lambda s: (s, 0, 0, 0)),
        ],
        out_shape=[
            jax.ShapeDtypeStruct((N_DEC, D_MODEL), BF16),
            jax.ShapeDtypeStruct((DEC_BATCH, WINDOW, KV_DIM), F32),
            jax.ShapeDtypeStruct((DEC_BATCH, WINDOW, KV_DIM), F32),
            jax.ShapeDtypeStruct((DEC_BATCH, N_REC_HEADS, REC_HEAD_DIM, REC_HEAD_DIM), F32),
        ],
        compiler_params=pltpu.CompilerParams(dimension_semantics=("arbitrary",)),
        name="dec_mix",
    )(sinks, z, ck, cv, s0, lb, oml, rec_w, bias_c, bias_n)


def _final_norm_kernel(x_ref, w_ref, o_ref):
    o_ref[...] = _rmsnorm_rows(x_ref[...], w_ref[...])


def _final_norm_dec(hs, w):
    return pl.pallas_call(
        _final_norm_kernel,
        out_shape=jax.ShapeDtypeStruct((N_DEC, D_MODEL), F32),
        name="final_norm_dec",
    )(hs, w)


def _t5_bucket(dist):
    max_exact = N_BUCKETS // 2
    d = jnp.maximum(dist, 0)
    df = jnp.maximum(d, 1).astype(F32)
    large = max_exact + (jnp.log(df / max_exact) / math.log(MAX_DISTANCE / max_exact)
                         * (N_BUCKETS - max_exact)).astype(jnp.int32)
    large = jnp.minimum(large, N_BUCKETS - 1)
    return jnp.where(d < max_exact, d, large)


def _bias_tables(rel_bias_table):
    bias_d = rel_bias_table[_t5_bucket(jnp.arange(WINDOW))].T.astype(F32)
    period = 3 * WINDOW + 1
    u = jnp.concatenate([jnp.full((N_HEADS, 1), NEG, F32), bias_d[:, ::-1],
                         jnp.full((N_HEADS, period - 1 - WINDOW), NEG, F32)], axis=1)
    rows = jnp.tile(u, (1, WINDOW))[:, :WINDOW * (period - 1)].reshape(N_HEADS, WINDOW, period - 1)
    prompt = rows[:, :, :2 * WINDOW].reshape(N_HEADS * WINDOW, 2 * WINDOW)
    dec = rows[:, :DEC_SEQ, :WINDOW + DEC_SEQ].reshape(N_HEADS * DEC_SEQ, WINDOW + DEC_SEQ)
    return prompt, dec[:, :WINDOW], dec[:, WINDOW:]


def kernel(x_prompt, x_sample, cache_k, cache_v, state_h, meta_tokens, w_in, w_out, norm_w, final_norm_w,
           attn_sinks, rel_bias_table, hgrn_lb_logits, hgrn_norm_w):
    w_in_b = w_in.astype(BF16)
    w_out_b = w_out.astype(BF16)
    pl_ = jax.nn.softmax(hgrn_lb_logits.astype(F32), axis=0)
    lb = (jnp.cumsum(pl_, axis=0) - pl_[0:1]).reshape(DEPTH, 1, D_REC)
    oml = 1.0 - lb
    bias_p, bias_dc, bias_dn = _bias_tables(rel_bias_table)
    nw = norm_w.astype(F32).reshape(DEPTH, 1, D_MODEL)
    rw = hgrn_norm_w.astype(F32).reshape(DEPTH, 1, D_REC)
    fw = final_norm_w.astype(F32).reshape(1, D_MODEL)
    sinks = attn_sinks.astype(F32)

    meta_tile = jnp.concatenate([jnp.zeros((PAD, D_MODEL), F32), meta_tokens.astype(F32)], axis=0)
    h_meta = jnp.broadcast_to(meta_tile[None], (BATCH, TILE, D_MODEL))
    h_main = x_prompt
    hs = x_sample.reshape(N_DEC, D_MODEL)
    ck = cache_k.reshape(DEPTH, DEC_BATCH, WINDOW, KV_DIM)
    cv = cache_v.reshape(DEPTH, DEC_BATCH, WINDOW, KV_DIM)

    pk, pv, ps, sk, sv, ss = [], [], [], [], [], []
    for l in range(DEPTH):
        outs = _prompt_layer(l, h_meta, h_main, w_in_b, w_out_b, nw, lb, oml, rw, sinks, bias_p, fw)
        if l < DEPTH - 1:
            h_meta, h_main, k_l, v_l, s_l = outs
        else:
            h_main, k_l, v_l, s_l = outs
        pk.append(k_l)
        pv.append(v_l)
        ps.append(s_l)

        z = _dec_proj(l, hs, nw, w_in_b)
        y, nk, nv, ns = _dec_mix(l, z, ck, cv, state_h, lb, oml, rw, sinks, bias_dc, bias_dn)
        hs = _dec_out(l, hs, y, w_out_b)
        sk.append(nk)
        sv.append(nv)
        ss.append(ns)

    y_prompt = h_main
    y_sample = _final_norm_dec(hs, fw).reshape(DEC_BATCH, DEC_SEQ, D_MODEL)
    kv_shape = (DEPTH, -1, WINDOW, N_KV, HEAD_DIM)
    return (y_prompt, y_sample,
            jnp.stack(pk).reshape(kv_shape), jnp.stack(pv).reshape(kv_shape),
            jnp.swapaxes(jnp.stack(ps), -1, -2),
            jnp.stack(sk).reshape(kv_shape), jnp.stack(sv).reshape(kv_shape), jnp.stack(ss))
```

```python
import functools
import math

import jax
import jax.numpy as jnp
from jax import lax
from jax.experimental import pallas as pl
from jax.experimental.pallas import tpu as pltpu

D_MODEL = 2048
BATCH = 4
SEQ = 2048
DEPTH = 4
DEC_BATCH = 32
DEC_SEQ = 8
N_META = 16
D_ATTN = 1024
D_REC = 1024
HEAD_DIM = 64
N_HEADS = 16
N_KV = 4
GROUP = 4
KV_DIM = 256
WINDOW = 128
N_BUCKETS = 32
MAX_DISTANCE = 128
REC_HEAD_DIM = 128
N_REC_HEADS = 8
EPS = 1e-6
D_IN = 2 * D_ATTN + 2 * KV_DIM + 4 * D_REC

OFF_QA = 0
OFF_KA = D_ATTN
OFF_VA = OFF_KA + KV_DIM
OFF_GA = OFF_VA + KV_DIM
OFF_QR = OFF_GA + D_ATTN
OFF_FR = OFF_QR + D_REC
OFF_IR = OFF_FR + D_REC
OFF_GR = OFF_IR + D_REC

TILE = 128
CHUNK = 64
HALF = CHUNK // 2
ROW_TILES = SEQ // TILE + 1
N_TILES = BATCH * ROW_TILES
PAD = TILE - N_META
PADDED_SEQ = ROW_TILES * TILE
PROJ_COLS = 512
NEG = -1e30
N_DEC = DEC_BATCH * DEC_SEQ
VMEM_LIMIT = 60 * 1024 * 1024

F32 = jnp.float32
BF16 = jnp.bfloat16


def _sigmoid_pair(x):
    e = jnp.exp(-jnp.abs(x))
    inv = 1.0 / (1.0 + e)
    pos = x >= 0
    return jnp.where(pos, inv, e * inv), jnp.where(pos, e * inv, inv)


def _silu(x):
    s, _ = _sigmoid_pair(x)
    return x * s


def _cumsum_rows(x):
    n = x.shape[0]
    row = lax.broadcasted_iota(jnp.int32, x.shape, 0)
    k = 1
    while k < n:
        x = x + jnp.where(row >= k, pltpu.roll(x, k, axis=0), 0.0)
        k *= 2
    return x


def _rmsnorm_rows(x, w):
    ms = jnp.mean(x * x, axis=-1, keepdims=True)
    return (x * lax.rsqrt(ms + EPS)) * w


def _dot_nt(a, b):
    return lax.dot_general(a, b, (((1,), (1,)), ((), ())), preferred_element_type=F32)


def _dot_tn(a, b):
    return lax.dot_general(a, b, (((0,), (0,)), ((), ())), preferred_element_type=F32)


def _dot(a, b):
    return jnp.dot(a, b, preferred_element_type=F32)


def _gate_features(fr, lb, oml):
    sig, sneg = _sigmoid_pair(fr)
    logf = jnp.log(lb + oml * sig)
    return logf, oml * sneg


def _proj_tile(s):
    return jnp.minimum(s, N_TILES - 1)


def _mix_tile(s):
    return jnp.maximum(s - 1, 0)


def _attention_group(layer, kh, z_ref, kv_ref, y_ref, bias_ref, sink_ref, c_tile):
    kpos = c_tile - TILE + lax.broadcasted_iota(jnp.int32, (1, 2 * TILE), 1)
    kvalid = kpos >= 0
    kk = kv_ref[:, kh * HEAD_DIM:(kh + 1) * HEAD_DIM]
    vv = kv_ref[:, KV_DIM + kh * HEAD_DIM:KV_DIM + (kh + 1) * HEAD_DIM]
    q4 = jnp.concatenate(
        [z_ref[:, (kh * GROUP + g) * HEAD_DIM:(kh * GROUP + g + 1) * HEAD_DIM] for g in range(GROUP)],
        axis=0)
    s = _dot_nt((q4 * (HEAD_DIM ** -0.5)).astype(BF16), kk)
    yield
    s = s + bias_ref[kh * GROUP * TILE:(kh + 1) * GROUP * TILE, :]
    s = jnp.where(kvalid, s, NEG)
    es, dens = [], []
    for g in range(GROUP):
        sink = sink_ref[layer, kh * GROUP + g]
        sg = s[g * TILE:(g + 1) * TILE]
        m = jnp.maximum(jnp.max(sg, axis=-1, keepdims=True), sink)
        e = jnp.exp(sg - m)
        dens.append(jnp.sum(e, axis=-1, keepdims=True) + jnp.exp(sink - m))
        es.append(e.astype(BF16))
    o4 = _dot(jnp.concatenate(es, axis=0), vv)
    yield
    outs = []
    for g in range(GROUP):
        h = kh * GROUP + g
        og = o4[g * TILE:(g + 1) * TILE] / dens[g]
        ga = z_ref[:, OFF_GA + h * HEAD_DIM:OFF_GA + (h + 1) * HEAD_DIM]
        outs.append(og * _silu(ga))
    y_ref[:, kh * GROUP * HEAD_DIM:(kh + 1) * GROUP * HEAD_DIM] = (
        jnp.concatenate(outs, axis=-1).astype(BF16))


def _hgrn2_unit(ci, hh, z_ref, st_ref, y_ref, lb_ref, oml_ref, recw_ref, c_tile):
    row = lax.broadcasted_iota(jnp.int32, (CHUNK, CHUNK), 0)
    col = lax.broadcasted_iota(jnp.int32, (CHUNK, CHUNK), 1)
    causal = col <= row
    r0 = ci * CHUNK
    rpos = c_tile + r0 + lax.broadcasted_iota(jnp.int32, (CHUNK, 1), 0)
    rvalid = rpos >= 0
    cs = hh * REC_HEAD_DIM
    lb = lb_ref[:, cs:cs + REC_HEAD_DIM]
    oml = oml_ref[:, cs:cs + REC_HEAD_DIM]
    logf, kr = _gate_features(z_ref[r0:r0 + CHUNK, OFF_FR + cs:OFF_FR + cs + REC_HEAD_DIM], lb, oml)
    q = _silu(z_ref[r0:r0 + CHUNK, OFF_QR + cs:OFF_QR + cs + REC_HEAD_DIM])
    v = jnp.where(rvalid, z_ref[r0:r0 + CHUNK, OFF_IR + cs:OFF_IR + cs + REC_HEAD_DIM], 0.0)
    vb = v.astype(BF16)
    a = _cumsum_rows(logf)
    a_mid = a[HALF - 1:HALF, :]
    a_end = a[CHUNK - 1:CHUNK, :]
    qh = (q * jnp.exp(a - a_mid)).astype(BF16)
    kh_ = (kr * jnp.exp(a_mid - a)).astype(BF16)
    scores = _dot_nt(qh, kh_)
    qt = (q * jnp.exp(a)).astype(BF16)
    kt = (kr * jnp.exp(a_end - a)).astype(BF16)
    st = st_ref[hh]
    o_state = _dot_nt(qt, st.astype(BF16))
    yield
    att = jnp.where(causal, scores, 0.0)
    o = _dot(att.astype(BF16), vb) + o_state
    st_ref[hh] = st * jnp.exp(a_end) + _dot_tn(vb, kt)
    yield
    on = o * lax.rsqrt(jnp.mean(o * o, axis=-1, keepdims=True) + EPS)
    gr = z_ref[r0:r0 + CHUNK, OFF_GR + cs:OFF_GR + cs + REC_HEAD_DIM]
    yr = (on * recw_ref[:, cs:cs + REC_HEAD_DIM]) * _silu(gr)
    y_ref[r0:r0 + CHUNK, D_ATTN + cs:D_ATTN + cs + REC_HEAD_DIM] = yr.astype(BF16)


def _prompt_layer_kernel(layer, first, final, sink_ref, *refs):
    if first:
        meta_ref, xmain_ref = refs[:2]
        refs = refs[2:]
    else:
        hin_ref = refs[0]
        refs = refs[1:]
    (win_ref, wout_ref, normw_ref, lb_ref, oml_ref, recw_ref, bias_ref, finw_ref,
     hout_ref, pk_ref, pv_ref, ps_ref,
     xn_ref, znew_ref, zold_ref, xprev_ref, hacc_ref, kv_ref, st_ref, y_ref) = refs
    s = pl.program_id(0)
    j_proj = lax.rem(_proj_tile(s), ROW_TILES)
    j_mix = lax.rem(_mix_tile(s), ROW_TILES)

    @pl.when(s == 0)
    def _():
        znew_ref[...] = jnp.zeros(znew_ref.shape, F32)
        xprev_ref[...] = jnp.zeros(xprev_ref.shape, F32)

    @pl.when(j_mix == 0)
    def _():
        kv_ref[0:TILE, :] = jnp.zeros((TILE, 2 * KV_DIM), BF16)
        st_ref[...] = jnp.zeros(st_ref.shape, F32)

    def tile_input():
        if first:
            return jnp.where(j_proj == 0, meta_ref[...], xmain_ref[0])
        return hin_ref[0]

    for c0 in range(0, D_IN, PROJ_COLS):
        zold_ref[:, c0:c0 + PROJ_COLS] = znew_ref[:, c0:c0 + PROJ_COLS]
    xn_ref[...] = _rmsnorm_rows(tile_input(), normw_ref[...]).astype(BF16)
    kv_ref[TILE:2 * TILE, :] = zold_ref[:, OFF_KA:OFF_KA + 2 * KV_DIM].astype(BF16)
    c_tile = j_mix * TILE - PAD

    def project(k):
        cols = slice(k * PROJ_COLS, (k + 1) * PROJ_COLS)
        znew_ref[:, cols] = _dot(xn_ref[...], win_ref[:, cols])

    half_n = D_MODEL // 2

    def out_attn(n):
        cols = slice(n * half_n, (n + 1) * half_n)
        hacc_ref[:, cols] = xprev_ref[:, cols] + _dot(y_ref[:, :D_ATTN], wout_ref[:D_ATTN, cols])

    def out_rec(n):
        cols = slice(n * half_n, (n + 1) * half_n)
        h_new = hacc_ref[:, cols] + _dot(y_ref[:, D_ATTN:], wout_ref[D_ATTN:, cols])
        if final:
            hacc_ref[:, cols] = h_new
        else:
            hout_ref[0, :, cols] = h_new

    units = [_attention_group(layer, kh, zold_ref, kv_ref, y_ref, bias_ref, sink_ref, c_tile)
             for kh in range(N_KV)]
    units += [_hgrn2_unit(u // N_REC_HEADS, u % N_REC_HEADS, zold_ref, st_ref, y_ref, lb_ref, oml_ref,
                          recw_ref, c_tile)
              for u in range((TILE // CHUNK) * N_REC_HEADS)]
    n_units = len(units)
    n_proj = D_IN // PROJ_COLS
    fillers = [functools.partial(project, k) for k in range(n_proj)]
    attn_done = N_KV + 2
    fillers.insert(attn_done, functools.partial(out_attn, 0))
    fillers.insert(attn_done + 2, functools.partial(out_attn, 1))
    for i in range(n_units + 2):
        if i == attn_done:
            kv_ref[0:TILE, :] = kv_ref[TILE:2 * TILE, :]
        if i < n_units:
            next(units[i])
        if fillers and i % 3 != 2:
            fillers.pop(0)()
        if 0 <= i - 1 < n_units:
            next(units[i - 1])
        if 0 <= i - 2 < n_units:
            next(units[i - 2], None)
    while fillers:
        fillers.pop(0)()
    out_rec(0)
    out_rec(1)

    if final:
        @pl.when(j_mix > 0)
        def _():
            hout_ref[0] = _rmsnorm_rows(hacc_ref[...], finw_ref[...])

    @pl.when(j_mix == ROW_TILES - 1)
    def _():
        pk_ref[0] = zold_ref[TILE - WINDOW:TILE, OFF_KA:OFF_KA + KV_DIM]
        pv_ref[0] = zold_ref[TILE - WINDOW:TILE, OFF_VA:OFF_VA + KV_DIM]
        ps_ref[0] = st_ref[...]

    xprev_ref[...] = tile_input()


def _layer_block(shape, layer):
    nd = len(shape)
    return pl.BlockSpec((None,) + tuple(shape), lambda s: (layer,) + (0,) * nd, pipeline_mode=pl.Buffered(1))


def _const_block(shape):
    nd = len(shape)
    return pl.BlockSpec(tuple(shape), lambda s: (0,) * nd, pipeline_mode=pl.Buffered(1))


def _prompt_layer(layer, h_in, w_in, w_out, norm_w, lb, oml, rec_w, sinks, bias, fin_w):
    first = layer == 0
    final = layer == DEPTH - 1
    seq_of = lambda t: lax.div(t, ROW_TILES)
    row_of = lambda t: lax.rem(t, ROW_TILES)
    padded_in = pl.BlockSpec((1, TILE, D_MODEL), lambda s: (seq_of(_proj_tile(s)), row_of(_proj_tile(s)), 0))
    main_in = pl.BlockSpec(
        (1, TILE, D_MODEL), lambda s: (seq_of(_proj_tile(s)), jnp.maximum(row_of(_proj_tile(s)) - 1, 0), 0))
    padded_out = pl.BlockSpec((1, TILE, D_MODEL), lambda s: (seq_of(_mix_tile(s)), row_of(_mix_tile(s)), 0))
    main_out = pl.BlockSpec(
        (1, TILE, D_MODEL), lambda s: (seq_of(_mix_tile(s)), jnp.maximum(row_of(_mix_tile(s)) - 1, 0), 0))
    kv_spec = pl.BlockSpec((1, WINDOW, KV_DIM), lambda s: (seq_of(_mix_tile(s)), 0, 0))
    st_spec = pl.BlockSpec((1, N_REC_HEADS, REC_HEAD_DIM, REC_HEAD_DIM), lambda s: (seq_of(_mix_tile(s)), 0, 0, 0))
    if first:
        h_specs = [_const_block((TILE, D_MODEL)), main_in]
        h_args = list(h_in)
    else:
        h_specs = [padded_in]
        h_args = [h_in]
    if final:
        h_out_spec, h_out_shape = main_out, jax.ShapeDtypeStruct((BATCH, SEQ, D_MODEL), F32)
    else:
        h_out_spec, h_out_shape = padded_out, jax.ShapeDtypeStruct((BATCH, PADDED_SEQ, D_MODEL), F32)
    return pl.pallas_call(
        functools.partial(_prompt_layer_kernel, layer, first, final),
        grid=(N_TILES + 1,),
        in_specs=[pl.BlockSpec(memory_space=pltpu.SMEM)] + h_specs + [
            _layer_block((D_MODEL, D_IN), layer),
            _layer_block((D_MODEL, D_MODEL), layer),
            _layer_block((1, D_MODEL), layer),
            _layer_block((1, D_REC), layer),
            _layer_block((1, D_REC), layer),
            _layer_block((1, D_REC), layer),
            _const_block((N_HEADS * TILE, 2 * TILE)),
            _const_block((1, D_MODEL)),
        ],
        out_specs=[h_out_spec, kv_spec, kv_spec, st_spec],
        out_shape=[
            h_out_shape,
            jax.ShapeDtypeStruct((BATCH, WINDOW, KV_DIM), F32),
            jax.ShapeDtypeStruct((BATCH, WINDOW, KV_DIM), F32),
            jax.ShapeDtypeStruct((BATCH, N_REC_HEADS, REC_HEAD_DIM, REC_HEAD_DIM), F32),
        ],
        scratch_shapes=[
            pltpu.VMEM((TILE, D_MODEL), BF16),
            pltpu.VMEM((TILE, D_IN), F32),
            pltpu.VMEM((TILE, D_IN), F32),
            pltpu.VMEM((TILE, D_MODEL), F32),
            pltpu.VMEM((TILE, D_MODEL), F32),
            pltpu.VMEM((2 * TILE, 2 * KV_DIM), BF16),
            pltpu.VMEM((N_REC_HEADS, REC_HEAD_DIM, REC_HEAD_DIM), F32),
            pltpu.VMEM((TILE, D_MODEL), BF16),
        ],
        compiler_params=pltpu.CompilerParams(
            dimension_semantics=("arbitrary",), vmem_limit_bytes=VMEM_LIMIT),
        name="prompt_layer",
    )(sinks, *h_args, w_in, w_out, norm_w, lb, oml, rec_w, bias, fin_w)


def _dec_proj_kernel(hs_ref, normw_ref, win_ref, z_ref):
    xn = _rmsnorm_rows(hs_ref[...], normw_ref[...]).astype(BF16)
    z_ref[...] = _dot(xn, win_ref[...])


def _dec_proj(layer, hs, norm_w, w_in):
    return pl.pallas_call(
        _dec_proj_kernel,
        grid=(D_IN // PROJ_COLS,),
        in_specs=[
            pl.BlockSpec((N_DEC, D_MODEL), lambda n: (0, 0)),
            pl.BlockSpec((None, 1, D_MODEL), lambda n: (layer, 0, 0)),
            pl.BlockSpec((None, D_MODEL, PROJ_COLS), lambda n: (layer, 0, n)),
        ],
        out_specs=pl.BlockSpec((N_DEC, PROJ_COLS), lambda n: (0, n)),
        out_shape=jax.ShapeDtypeStruct((N_DEC, D_IN), F32),
        compiler_params=pltpu.CompilerParams(dimension_semantics=("arbitrary",)),
        name="dec_proj",
    )(hs, norm_w, w_in)


def _dec_out_kernel(hs_ref, y_ref, wout_ref, o_ref):
    o_ref[...] = hs_ref[...] + _dot(y_ref[...], wout_ref[...])


def _dec_out(layer, hs, y, w_out):
    return pl.pallas_call(
        _dec_out_kernel,
        grid=(D_MODEL // PROJ_COLS,),
        in_specs=[
            pl.BlockSpec((N_DEC, PROJ_COLS), lambda n: (0, n)),
            pl.BlockSpec((N_DEC, D_MODEL), lambda n: (0, 0)),
            pl.BlockSpec((None, D_MODEL, PROJ_COLS), lambda n: (layer, 0, n)),
        ],
        out_specs=pl.BlockSpec((N_DEC, PROJ_COLS), lambda n: (0, n)),
        out_shape=jax.ShapeDtypeStruct((N_DEC, D_MODEL), F32),
        compiler_params=pltpu.CompilerParams(dimension_semantics=("arbitrary",)),
        name="dec_out",
    )(hs, y, w_out)


def _dec_mix_kernel(layer, sink_ref, z_ref, ck_ref, cv_ref, s0_ref, lb_ref, oml_ref, recw_ref, biasc_ref,
                    biasn_ref, y_ref, nk_ref, nv_ref, ns_ref):
    t = DEC_SEQ
    k_new = z_ref[:, OFF_KA:OFF_KA + KV_DIM]
    v_new = z_ref[:, OFF_VA:OFF_VA + KV_DIM]
    nk_ref[0, 0:WINDOW - t, :] = ck_ref[0, t:WINDOW, :]
    nk_ref[0, WINDOW - t:WINDOW, :] = k_new
    nv_ref[0, 0:WINDOW - t, :] = cv_ref[0, t:WINDOW, :]
    nv_ref[0, WINDOW - t:WINDOW, :] = v_new

    for kh in range(N_KV):
        hs_ = slice(kh * HEAD_DIM, (kh + 1) * HEAD_DIM)
        kc = ck_ref[0, :, hs_].astype(BF16)
        vc = cv_ref[0, :, hs_].astype(BF16)
        kn = k_new[:, hs_]
        vn = v_new[:, hs_]
        q4 = jnp.concatenate(
            [z_ref[:, (kh * GROUP + g) * HEAD_DIM:(kh * GROUP + g + 1) * HEAD_DIM] for g in range(GROUP)],
            axis=0) * (HEAD_DIM ** -0.5)
        rows = slice(kh * GROUP * t, (kh + 1) * GROUP * t)
        sc = _dot_nt(q4.astype(BF16), kc) + biasc_ref[rows, :]
        sn = _dot_nt(q4, kn) + biasn_ref[rows, :]
        outs = []
        for g in range(GROUP):
            h = kh * GROUP + g
            sink = sink_ref[layer, h]
            scg = sc[g * t:(g + 1) * t]
            sng = sn[g * t:(g + 1) * t]
            m = jnp.maximum(jnp.maximum(jnp.max(scg, axis=-1, keepdims=True),
                                        jnp.max(sng, axis=-1, keepdims=True)), sink)
            ec = jnp.exp(scg - m)
            en = jnp.exp(sng - m)
            den = (jnp.sum(ec, axis=-1, keepdims=True) + jnp.sum(en, axis=-1, keepdims=True)
                   + jnp.exp(sink - m))
            og = (_dot(ec, vc.astype(F32)) + _dot(en, vn)) / den
            ga = z_ref[:, OFF_GA + h * HEAD_DIM:OFF_GA + (h + 1) * HEAD_DIM]
            outs.append(og * _silu(ga))
        y_ref[:, kh * GROUP * HEAD_DIM:(kh + 1) * GROUP * HEAD_DIM] = (
            jnp.concatenate(outs, axis=-1).astype(BF16))

    row = lax.broadcasted_iota(jnp.int32, (t, t), 0)
    col = lax.broadcasted_iota(jnp.int32, (t, t), 1)
    causal = col <= row
    for hh in range(N_REC_HEADS):
        cs = hh * REC_HEAD_DIM
        lb = lb_ref[:, cs:cs + REC_HEAD_DIM]
        oml = oml_ref[:, cs:cs + REC_HEAD_DIM]
        logf, kr = _gate_features(z_ref[:, OFF_FR + cs:OFF_FR + cs + REC_HEAD_DIM], lb, oml)
        q = _silu(z_ref[:, OFF_QR + cs:OFF_QR + cs + REC_HEAD_DIM])
        v = z_ref[:, OFF_IR + cs:OFF_IR + cs + REC_HEAD_DIM]
        a = _cumsum_rows(logf)
        a_mid = a[t // 2 - 1:t // 2, :]
        a_end = a[t - 1:t, :]
        att = jnp.where(causal, _dot_nt(q * jnp.exp(a - a_mid), kr * jnp.exp(a_mid - a)), 0.0)
        qt = q * jnp.exp(a)
        kt = kr * jnp.exp(a_end - a)
        s0 = s0_ref[0, hh]
        o = _dot(att, v) + _dot(qt, s0)
        ktd = jnp.concatenate([kt, jnp.exp(a_end), jnp.zeros((t - 1, REC_HEAD_DIM), F32)], axis=0).T
        ns_ref[0, hh] = s0 * ktd[:, t:t + 1] + _dot(ktd[:, 0:t], v)
        on = o * lax.rsqrt(jnp.mean(o * o, axis=-1, keepdims=True) + EPS)
        gr = z_ref[:, OFF_GR + cs:OFF_GR + cs + REC_HEAD_DIM]
        yr = (on * recw_ref[:, cs:cs + REC_HEAD_DIM]) * _silu(gr)
        y_ref[:, D_ATTN + cs:D_ATTN + cs + REC_HEAD_DIM] = yr.astype(BF16)


def _dec_mix(layer, z, ck, cv, s0, lb, oml, rec_w, sinks, bias_c, bias_n):
    t = DEC_SEQ
    const2 = lambda s: (0, 0)
    per_layer = lambda s: (layer, 0, 0)
    kv_in = pl.BlockSpec((None, 1, WINDOW, KV_DIM), lambda s: (layer, s, 0, 0))
    st_in = pl.BlockSpec((None, 1, N_REC_HEADS, REC_HEAD_DIM, REC_HEAD_DIM), lambda s: (layer, s, 0, 0, 0))
    return pl.pallas_call(
        functools.partial(_dec_mix_kernel, layer),
        grid=(DEC_BATCH,),
        in_specs=[
            pl.BlockSpec(memory_space=pltpu.SMEM),
            pl.BlockSpec((t, D_IN), lambda s: (s, 0)),
            kv_in,
            kv_in,
            st_in,
            pl.BlockSpec((None, 1, D_REC), per_layer),
            pl.BlockSpec((None, 1, D_REC), per_layer),
            pl.BlockSpec((None, 1, D_REC), per_layer),
            pl.BlockSpec((N_HEADS * t, WINDOW), const2),
            pl.BlockSpec((N_HEADS * t, t), const2),
        ],
        out_specs=[
            pl.BlockSpec((t, D_MODEL), lambda s: (s, 0)),
            pl.BlockSpec((1, WINDOW, KV_DIM), lambda s: (s, 0, 0)),
            pl.BlockSpec((1, WINDOW, KV_DIM), lambda s: (s, 0, 0)),
            pl.BlockSpec((1, N_REC_HEADS, REC_HEAD_DIM, REC_HEAD_DIM), lambda s: (s, 0, 0, 0)),
        ],
        out_shape=[
            jax.ShapeDtypeStruct((N_DEC, D_MODEL), BF16),
            jax.ShapeDtypeStruct((DEC_BATCH, WINDOW, KV_DIM), F32),
            jax.ShapeDtypeStruct((DEC_BATCH, WINDOW, KV_DIM), F32),
            jax.ShapeDtypeStruct((DEC_BATCH, N_REC_HEADS, REC_HEAD_DIM, REC_HEAD_DIM), F32),
        ],
        compiler_params=pltpu.CompilerParams(dimension_semantics=("arbitrary",)),
        name="dec_mix",
    )(sinks, z, ck, cv, s0, lb, oml, rec_w, bias_c, bias_n)


def _final_norm_kernel(x_ref, w_ref, o_ref):
    o_ref[...] = _rmsnorm_rows(x_ref[...], w_ref[...])


def _final_norm_dec(hs, w):
    return pl.pallas_call(
        _final_norm_kernel,
        out_shape=jax.ShapeDtypeStruct((N_DEC, D_MODEL), F32),
        name="final_norm_dec",
    )(hs, w)


def _t5_bucket(dist):
    max_exact = N_BUCKETS // 2
    d = jnp.maximum(dist, 0)
    df = jnp.maximum(d, 1).astype(F32)
    large = max_exact + (jnp.log(df / max_exact) / math.log(MAX_DISTANCE / max_exact)
                         * (N_BUCKETS - max_exact)).astype(jnp.int32)
    large = jnp.minimum(large, N_BUCKETS - 1)
    return jnp.where(d < max_exact, d, large)


def _bias_tables(rel_bias_table):
    bias_d = rel_bias_table[_t5_bucket(jnp.arange(WINDOW))].T.astype(F32)
    period = 3 * WINDOW + 1
    u = jnp.concatenate([jnp.full((N_HEADS, 1), NEG, F32), bias_d[:, ::-1],
                         jnp.full((N_HEADS, period - 1 - WINDOW), NEG, F32)], axis=1)
    rows = jnp.tile(u, (1, WINDOW))[:, :WINDOW * (period - 1)].reshape(N_HEADS, WINDOW, period - 1)
    prompt = rows[:, :, :2 * WINDOW].reshape(N_HEADS * WINDOW, 2 * WINDOW)
    dec = rows[:, :DEC_SEQ, :WINDOW + DEC_SEQ].reshape(N_HEADS * DEC_SEQ, WINDOW + DEC_SEQ)
    return prompt, dec[:, :WINDOW], dec[:, WINDOW:]


def kernel(x_prompt, x_sample, cache_k, cache_v, state_h, meta_tokens, w_in, w_out, norm_w, final_norm_w,
           attn_sinks, rel_bias_table, hgrn_lb_logits, hgrn_norm_w):
    w_in_b = w_in.astype(BF16)
    w_out_b = w_out.astype(BF16)
    pl_ = jax.nn.softmax(hgrn_lb_logits.astype(F32), axis=0)
    lb = (jnp.cumsum(pl_, axis=0) - pl_[0:1]).reshape(DEPTH, 1, D_REC)
    oml = 1.0 - lb
    bias_p, bias_dc, bias_dn = _bias_tables(rel_bias_table)
    nw = norm_w.astype(F32).reshape(DEPTH, 1, D_MODEL)
    rw = hgrn_norm_w.astype(F32).reshape(DEPTH, 1, D_REC)
    fw = final_norm_w.astype(F32).reshape(1, D_MODEL)
    sinks = attn_sinks.astype(F32)

    meta_tile = jnp.concatenate([jnp.zeros((PAD, D_MODEL), F32), meta_tokens.astype(F32)], axis=0)
    hp = (meta_tile, x_prompt)
    hs = x_sample.reshape(N_DEC, D_MODEL)
    ck = cache_k.reshape(DEPTH, DEC_BATCH, WINDOW, KV_DIM)
    cv = cache_v.reshape(DEPTH, DEC_BATCH, WINDOW, KV_DIM)

    pk, pv, ps, sk, sv, ss = [], [], [], [], [], []
    for l in range(DEPTH):
        hp, k_l, v_l, s_l = _prompt_layer(l, hp, w_in_b, w_out_b, nw, lb, oml, rw, sinks, bias_p, fw)
        pk.append(k_l)
        pv.append(v_l)
        ps.append(s_l)

        z = _dec_proj(l, hs, nw, w_in_b)
        y, nk, nv, ns = _dec_mix(l, z, ck, cv, state_h, lb, oml, rw, sinks, bias_dc, bias_dn)
        hs = _dec_out(l, hs, y, w_out_b)
        sk.append(nk)
        sv.append(nv)
        ss.append(ns)

    y_prompt = hp
    y_sample = _final_norm_dec(hs, fw).reshape(DEC_BATCH, DEC_SEQ, D_MODEL)
    kv_shape = (DEPTH, -1, WINDOW, N_KV, HEAD_DIM)
    return (y_prompt, y_sample,
            jnp.stack(pk).reshape(kv_shape), jnp.stack(pv).reshape(kv_shape),
            jnp.swapaxes(jnp.stack(ps), -1, -2),
            jnp.stack(sk).reshape(kv_shape), jnp.stack(sv).reshape(kv_shape), jnp.stack(ss))
```

```python
import functools
import math

import jax
import jax.numpy as jnp
from jax import lax
from jax.experimental import pallas as pl
from jax.experimental.pallas import tpu as pltpu

D_MODEL = 2048
BATCH = 4
SEQ = 2048
DEPTH = 4
DEC_BATCH = 32
DEC_SEQ = 8
N_META = 16
D_ATTN = 1024
D_REC = 1024
HEAD_DIM = 64
N_HEADS = 16
N_KV = 4
GROUP = 4
KV_DIM = 256
WINDOW = 128
N_BUCKETS = 32
MAX_DISTANCE = 128
REC_HEAD_DIM = 128
N_REC_HEADS = 8
EPS = 1e-6
D_IN = 2 * D_ATTN + 2 * KV_DIM + 4 * D_REC

OFF_QA = 0
OFF_KA = D_ATTN
OFF_VA = OFF_KA + KV_DIM
OFF_GA = OFF_VA + KV_DIM
OFF_QR = OFF_GA + D_ATTN
OFF_FR = OFF_QR + D_REC
OFF_IR = OFF_FR + D_REC
OFF_GR = OFF_IR + D_REC
ATT_COLS = OFF_QR

TILE = 256
QBLK = WINDOW
CHUNK = 64
HALF = CHUNK // 2
ROW_TILES = SEQ // TILE + 1
PAD = TILE - N_META
PADDED_SEQ = ROW_TILES * TILE
PROJ_COLS = 512
REC_GROUP = PROJ_COLS // REC_HEAD_DIM
NEG = -1e30
N_DEC = DEC_BATCH * DEC_SEQ
V7X_VMEM_BYTES = 64 * 1024 * 1024
VMEM_LIMIT = V7X_VMEM_BYTES - 2 * 1024 * 1024

F32 = jnp.float32
BF16 = jnp.bfloat16


def _sigmoid_pair(x):
    e = jnp.exp(-jnp.abs(x))
    inv = 1.0 / (1.0 + e)
    pos = x >= 0
    return jnp.where(pos, inv, e * inv), jnp.where(pos, e * inv, inv)


def _silu(x):
    s, _ = _sigmoid_pair(x)
    return x * s


def _cumsum_rows(x):
    n = x.shape[0]
    row = lax.broadcasted_iota(jnp.int32, x.shape, 0)
    k = 1
    while k < n:
        x = x + jnp.where(row >= k, pltpu.roll(x, k, axis=0), 0.0)
        k *= 2
    return x


def _rmsnorm_rows(x, w):
    ms = jnp.mean(x * x, axis=-1, keepdims=True)
    return (x * lax.rsqrt(ms + EPS)) * w


def _dot_nt(a, b):
    return lax.dot_general(a, b, (((1,), (1,)), ((), ())), preferred_element_type=F32)


def _dot_tn(a, b):
    return lax.dot_general(a, b, (((0,), (0,)), ((), ())), preferred_element_type=F32)


def _dot(a, b):
    return jnp.dot(a, b, preferred_element_type=F32)


def _gate_features(fr, lb, oml):
    sig, sneg = _sigmoid_pair(fr)
    logf = jnp.log(lb + oml * sig)
    return logf, oml * sneg


def _attention_unit(layer, qb, kh, za_ref, kv_ref, y_ref, bias_ref, sink_ref, c_tile):
    r0 = qb * QBLK
    kpos = c_tile + r0 - QBLK + lax.broadcasted_iota(jnp.int32, (1, 2 * QBLK), 1)
    kvalid = kpos >= 0
    kk = kv_ref[r0:r0 + 2 * QBLK, kh * HEAD_DIM:(kh + 1) * HEAD_DIM]
    q4 = jnp.concatenate(
        [za_ref[r0:r0 + QBLK, (kh * GROUP + g) * HEAD_DIM:(kh * GROUP + g + 1) * HEAD_DIM]
         for g in range(GROUP)], axis=0)
    s = _dot_nt((q4 * (HEAD_DIM ** -0.5)).astype(BF16), kk)
    yield
    s = s + bias_ref[kh * GROUP * QBLK:(kh + 1) * GROUP * QBLK, :]
    s = jnp.where(kvalid, s, NEG)
    es, dens = [], []
    for g in range(GROUP):
        sink = sink_ref[layer, kh * GROUP + g]
        sg = s[g * QBLK:(g + 1) * QBLK]
        m = jnp.maximum(jnp.max(sg, axis=-1, keepdims=True), sink)
        e = jnp.exp(sg - m)
        dens.append(jnp.sum(e, axis=-1, keepdims=True) + jnp.exp(sink - m))
        es.append(e.astype(BF16))
    vv = kv_ref[r0:r0 + 2 * QBLK, KV_DIM + kh * HEAD_DIM:KV_DIM + (kh + 1) * HEAD_DIM]
    o4 = _dot(jnp.concatenate(es, axis=0), vv)
    yield
    outs = []
    for g in range(GROUP):
        h = kh * GROUP + g
        og = o4[g * QBLK:(g + 1) * QBLK] / dens[g]
        ga = za_ref[r0:r0 + QBLK, OFF_GA + h * HEAD_DIM:OFF_GA + (h + 1) * HEAD_DIM]
        outs.append(og * _silu(ga))
    y_ref[r0:r0 + QBLK, kh * GROUP * HEAD_DIM:(kh + 1) * GROUP * HEAD_DIM] = (
        jnp.concatenate(outs, axis=-1).astype(BF16))


def _hgrn2_unit(ci, hh, zr_ref, st_ref, y_ref, lb_ref, oml_ref, recw_ref, c_tile):
    row = lax.broadcasted_iota(jnp.int32, (CHUNK, CHUNK), 0)
    col = lax.broadcasted_iota(jnp.int32, (CHUNK, CHUNK), 1)
    causal = col <= row
    r0 = ci * CHUNK
    rows = slice(r0, r0 + CHUNK)
    rpos = c_tile + r0 + lax.broadcasted_iota(jnp.int32, (CHUNK, 1), 0)
    rvalid = rpos >= 0
    cs = hh * REC_HEAD_DIM
    zc = (hh % REC_GROUP) * REC_HEAD_DIM
    lb = lb_ref[:, cs:cs + REC_HEAD_DIM]
    oml = oml_ref[:, cs:cs + REC_HEAD_DIM]
    logf, kr = _gate_features(zr_ref[rows, PROJ_COLS + zc:PROJ_COLS + zc + REC_HEAD_DIM], lb, oml)
    q = _silu(zr_ref[rows, zc:zc + REC_HEAD_DIM])
    v = jnp.where(rvalid, zr_ref[rows, 2 * PROJ_COLS + zc:2 * PROJ_COLS + zc + REC_HEAD_DIM], 0.0)
    vb = v.astype(BF16)
    a = _cumsum_rows(logf)
    a_mid = a[HALF - 1:HALF, :]
    a_end = a[CHUNK - 1:CHUNK, :]
    qh = (q * jnp.exp(a - a_mid)).astype(BF16)
    kh_ = (kr * jnp.exp(a_mid - a)).astype(BF16)
    scores = _dot_nt(qh, kh_)
    qt = (q * jnp.exp(a)).astype(BF16)
    kt = (kr * jnp.exp(a_end - a)).astype(BF16)
    st = st_ref[hh]
    o_state = _dot_nt(qt, st.astype(BF16))
    yield
    att = jnp.where(causal, scores, 0.0)
    o = _dot(att.astype(BF16), vb) + o_state
    st_ref[hh] = st * jnp.exp(a_end) + _dot_tn(vb, kt)
    yield
    on = o * lax.rsqrt(jnp.mean(o * o, axis=-1, keepdims=True) + EPS)
    gr = zr_ref[rows, 3 * PROJ_COLS + zc:3 * PROJ_COLS + zc + REC_HEAD_DIM]
    yr = (on * recw_ref[:, cs:cs + REC_HEAD_DIM]) * _silu(gr)
    y_ref[rows, D_ATTN + cs:D_ATTN + cs + REC_HEAD_DIM] = yr.astype(BF16)


def _emit_staggered(units, fillers):
    n = len(units)
    rounds = n + 2
    at_round = [[] for _ in range(rounds)]
    for k, f in enumerate(fillers):
        at_round[k * rounds // len(fillers)].append(f)
    for i in range(rounds):
        if i < n:
            next(units[i])
        for f in at_round[i]:
            f()
        if 0 <= i - 1 < n:
            next(units[i - 1])
        if 0 <= i - 2 < n:
            next(units[i - 2], None)


def _prompt_layer_kernel(layer, first, final, sink_ref, *refs):
    if first:
        meta_ref, xmain_ref = refs[:2]
        refs = refs[2:]
    else:
        hin_ref = refs[0]
        refs = refs[1:]
    (win_ref, wout_ref, normw_ref, lb_ref, oml_ref, recw_ref, bias_ref, finw_ref,
     hout_ref, pk_ref, pv_ref, ps_ref,
     xn_ref, za_ref, zr_ref, kv_ref, st_ref, y_ref) = refs
    j = pl.program_id(1)

    @pl.when(j == 0)
    def _():
        kv_ref[0:QBLK, :] = jnp.zeros((QBLK, 2 * KV_DIM), BF16)
        st_ref[...] = jnp.zeros(st_ref.shape, F32)

    def tile_input(cols=slice(None)):
        if first:
            meta = meta_ref[:, cols]
            meta_rows = jnp.concatenate([jnp.zeros((PAD, meta.shape[1]), F32), meta], axis=0)
            return jnp.where(j == 0, meta_rows, xmain_ref[0, :, cols])
        return hin_ref[0, :, cols]

    c_tile = j * TILE - PAD
    xn_ref[...] = _rmsnorm_rows(tile_input(), normw_ref[...]).astype(BF16)

    def project(dst_ref, dst_col, src_col):
        dst_ref[:, dst_col:dst_col + PROJ_COLS] = _dot(xn_ref[...], win_ref[:, src_col:src_col + PROJ_COLS])

    def project_rec(dst_ref, group):
        return [functools.partial(project, dst_ref, part * PROJ_COLS, off + group * PROJ_COLS)
                for part, off in enumerate((OFF_QR, OFF_FR, OFF_IR, OFF_GR))]

    def out_proj(k0, k1, c0, first_half):
        cols = slice(c0, c0 + PROJ_COLS)
        base = tile_input(cols) if first_half else hout_ref[0, :, cols]
        hout_ref[0, :, cols] = base + _dot(y_ref[:, k0:k1], wout_ref[k0:k1, cols])

    for c0 in range(0, ATT_COLS, PROJ_COLS):
        project(za_ref, c0, c0)
    kv_ref[QBLK:QBLK + TILE, :] = za_ref[:, OFF_KA:OFF_KA + 2 * KV_DIM].astype(BF16)
    pk_ref[0] = za_ref[TILE - WINDOW:TILE, OFF_KA:OFF_KA + KV_DIM]
    pv_ref[0] = za_ref[TILE - WINDOW:TILE, OFF_VA:OFF_VA + KV_DIM]

    att_units = [_attention_unit(layer, qb, kh, za_ref, kv_ref, y_ref, bias_ref, sink_ref, c_tile)
                 for qb in range(TILE // QBLK) for kh in range(N_KV)]
    _emit_staggered(att_units, project_rec(zr_ref, 0))
    kv_ref[0:QBLK, :] = kv_ref[TILE:TILE + QBLK, :]

    n_groups = N_REC_HEADS // REC_GROUP
    z_bufs = [zr_ref, za_ref]
    for grp in range(n_groups):
        zsrc = z_bufs[grp % 2]
        units = [_hgrn2_unit(ci, grp * REC_GROUP + hl, zsrc, st_ref, y_ref, lb_ref, oml_ref, recw_ref, c_tile)
                 for ci in range(TILE // CHUNK) for hl in range(REC_GROUP)]
        if grp + 1 < n_groups:
            fillers = project_rec(z_bufs[(grp + 1) % 2], grp + 1)
        else:
            fillers = [functools.partial(out_proj, 0, D_ATTN, c0, True) for c0 in range(0, D_MODEL, PROJ_COLS)]
        _emit_staggered(units, fillers)
    for c0 in range(0, D_MODEL, PROJ_COLS):
        out_proj(D_ATTN, D_MODEL, c0, False)
    ps_ref[0] = st_ref[...]

    if final:
        @pl.when(j > 0)
        def _():
            hout_ref[0] = _rmsnorm_rows(hout_ref[0], finw_ref[...])


def _layer_block(shape, layer):
    nd = len(shape)
    return pl.BlockSpec((None,) + tuple(shape), lambda b, j: (layer,) + (0,) * nd, pipeline_mode=pl.Buffered(1))


def _const_block(shape):
    nd = len(shape)
    return pl.BlockSpec(tuple(shape), lambda b, j: (0,) * nd, pipeline_mode=pl.Buffered(1))


def _prompt_layer(layer, h_in, w_in, w_out, norm_w, lb, oml, rec_w, sinks, bias, fin_w):
    first = layer == 0
    final = layer == DEPTH - 1
    padded = pl.BlockSpec((1, TILE, D_MODEL), lambda b, j: (b, j, 0))
    main = pl.BlockSpec((1, TILE, D_MODEL), lambda b, j: (b, jnp.maximum(j - 1, 0), 0))
    kv_spec = pl.BlockSpec((1, WINDOW, KV_DIM), lambda b, j: (b, 0, 0))
    st_spec = pl.BlockSpec((1, N_REC_HEADS, REC_HEAD_DIM, REC_HEAD_DIM), lambda b, j: (b, 0, 0, 0))
    if first:
        h_specs = [_const_block((N_META, D_MODEL)), main]
        h_args = list(h_in)
    else:
        h_specs = [padded]
        h_args = [h_in]
    if final:
        h_out_spec, h_out_shape = main, jax.ShapeDtypeStruct((BATCH, SEQ, D_MODEL), F32)
    else:
        h_out_spec, h_out_shape = padded, jax.ShapeDtypeStruct((BATCH, PADDED_SEQ, D_MODEL), F32)
    return pl.pallas_call(
        functools.partial(_prompt_layer_kernel, layer, first, final),
        grid=(BATCH, ROW_TILES),
        in_specs=[pl.BlockSpec(memory_space=pltpu.SMEM)] + h_specs + [
            _layer_block((D_MODEL, D_IN), layer),
            _layer_block((D_MODEL, D_MODEL), layer),
            _layer_block((1, D_MODEL), layer),
            _layer_block((1, D_REC), layer),
            _layer_block((1, D_REC), layer),
            _layer_block((1, D_REC), layer),
            _const_block((N_HEADS * QBLK, 2 * QBLK)),
            _const_block((1, D_MODEL)),
        ],
        out_specs=[h_out_spec, kv_spec, kv_spec, st_spec],
        out_shape=[
            h_out_shape,
            jax.ShapeDtypeStruct((BATCH, WINDOW, KV_DIM), F32),
            jax.ShapeDtypeStruct((BATCH, WINDOW, KV_DIM), F32),
            jax.ShapeDtypeStruct((BATCH, N_REC_HEADS, REC_HEAD_DIM, REC_HEAD_DIM), F32),
        ],
        scratch_shapes=[
            pltpu.VMEM((TILE, D_MODEL), BF16),
            pltpu.VMEM((TILE, ATT_COLS), F32),
            pltpu.VMEM((TILE, 4 * PROJ_COLS), F32),
            pltpu.VMEM((QBLK + TILE, 2 * KV_DIM), BF16),
            pltpu.VMEM((N_REC_HEADS, REC_HEAD_DIM, REC_HEAD_DIM), F32),
            pltpu.VMEM((TILE, D_MODEL), BF16),
        ],
        compiler_params=pltpu.CompilerParams(
            dimension_semantics=("arbitrary", "arbitrary"), vmem_limit_bytes=VMEM_LIMIT),
        name="prompt_layer",
    )(sinks, *h_args, w_in, w_out, norm_w, lb, oml, rec_w, bias, fin_w)


def _dec_proj_kernel(hs_ref, normw_ref, win_ref, z_ref):
    xn = _rmsnorm_rows(hs_ref[...], normw_ref[...]).astype(BF16)
    z_ref[...] = _dot(xn, win_ref[...])


def _dec_proj(layer, hs, norm_w, w_in):
    return pl.pallas_call(
        _dec_proj_kernel,
        grid=(D_IN // PROJ_COLS,),
        in_specs=[
            pl.BlockSpec((N_DEC, D_MODEL), lambda n: (0, 0)),
            pl.BlockSpec((None, 1, D_MODEL), lambda n: (layer, 0, 0)),
            pl.BlockSpec((None, D_MODEL, PROJ_COLS), lambda n: (layer, 0, n)),
        ],
        out_specs=pl.BlockSpec((N_DEC, PROJ_COLS), lambda n: (0, n)),
        out_shape=jax.ShapeDtypeStruct((N_DEC, D_IN), F32),
        compiler_params=pltpu.CompilerParams(dimension_semantics=("arbitrary",)),
        name="dec_proj",
    )(hs, norm_w, w_in)


def _dec_out_kernel(hs_ref, y_ref, wout_ref, o_ref):
    o_ref[...] = hs_ref[...] + _dot(y_ref[...], wout_ref[...])


def _dec_out(layer, hs, y, w_out):
    return pl.pallas_call(
        _dec_out_kernel,
        grid=(D_MODEL // PROJ_COLS,),
        in_specs=[
            pl.BlockSpec((N_DEC, PROJ_COLS), lambda n: (0, n)),
            pl.BlockSpec((N_DEC, D_MODEL), lambda n: (0, 0)),
            pl.BlockSpec((None, D_MODEL, PROJ_COLS), lambda n: (layer, 0, n)),
        ],
        out_specs=pl.BlockSpec((N_DEC, PROJ_COLS), lambda n: (0, n)),
        out_shape=jax.ShapeDtypeStruct((N_DEC, D_MODEL), F32),
        compiler_params=pltpu.CompilerParams(dimension_semantics=("arbitrary",)),
        name="dec_out",
    )(hs, y, w_out)


def _dec_mix_kernel(layer, sink_ref, z_ref, ck_ref, cv_ref, s0_ref, lb_ref, oml_ref, recw_ref, biasc_ref,
                    biasn_ref, y_ref, nk_ref, nv_ref, ns_ref):
    t = DEC_SEQ
    k_new = z_ref[:, OFF_KA:OFF_KA + KV_DIM]
    v_new = z_ref[:, OFF_VA:OFF_VA + KV_DIM]
    nk_ref[0, 0:WINDOW - t, :] = ck_ref[0, t:WINDOW, :]
    nk_ref[0, WINDOW - t:WINDOW, :] = k_new
    nv_ref[0, 0:WINDOW - t, :] = cv_ref[0, t:WINDOW, :]
    nv_ref[0, WINDOW - t:WINDOW, :] = v_new

    for kh in range(N_KV):
        hs_ = slice(kh * HEAD_DIM, (kh + 1) * HEAD_DIM)
        kc = ck_ref[0, :, hs_].astype(BF16)
        vc = cv_ref[0, :, hs_].astype(BF16)
        kn = k_new[:, hs_]
        vn = v_new[:, hs_]
        q4 = jnp.concatenate(
            [z_ref[:, (kh * GROUP + g) * HEAD_DIM:(kh * GROUP + g + 1) * HEAD_DIM] for g in range(GROUP)],
            axis=0) * (HEAD_DIM ** -0.5)
        rows = slice(kh * GROUP * t, (kh + 1) * GROUP * t)
        sc = _dot_nt(q4.astype(BF16), kc) + biasc_ref[rows, :]
        sn = _dot_nt(q4, kn) + biasn_ref[rows, :]
        outs = []
        for g in range(GROUP):
            h = kh * GROUP + g
            sink = sink_ref[layer, h]
            scg = sc[g * t:(g + 1) * t]
            sng = sn[g * t:(g + 1) * t]
            m = jnp.maximum(jnp.maximum(jnp.max(scg, axis=-1, keepdims=True),
                                        jnp.max(sng, axis=-1, keepdims=True)), sink)
            ec = jnp.exp(scg - m)
            en = jnp.exp(sng - m)
            den = (jnp.sum(ec, axis=-1, keepdims=True) + jnp.sum(en, axis=-1, keepdims=True)
                   + jnp.exp(sink - m))
            og = (_dot(ec, vc.astype(F32)) + _dot(en, vn)) / den
            ga = z_ref[:, OFF_GA + h * HEAD_DIM:OFF_GA + (h + 1) * HEAD_DIM]
            outs.append(og * _silu(ga))
        y_ref[:, kh * GROUP * HEAD_DIM:(kh + 1) * GROUP * HEAD_DIM] = (
            jnp.concatenate(outs, axis=-1).astype(BF16))

    row = lax.broadcasted_iota(jnp.int32, (t, t), 0)
    col = lax.broadcasted_iota(jnp.int32, (t, t), 1)
    causal = col <= row
    for hh in range(N_REC_HEADS):
        cs = hh * REC_HEAD_DIM
        lb = lb_ref[:, cs:cs + REC_HEAD_DIM]
        oml = oml_ref[:, cs:cs + REC_HEAD_DIM]
        logf, kr = _gate_features(z_ref[:, OFF_FR + cs:OFF_FR + cs + REC_HEAD_DIM], lb, oml)
        q = _silu(z_ref[:, OFF_QR + cs:OFF_QR + cs + REC_HEAD_DIM])
        v = z_ref[:, OFF_IR + cs:OFF_IR + cs + REC_HEAD_DIM]
        a = _cumsum_rows(logf)
        a_mid = a[t // 2 - 1:t // 2, :]
        a_end = a[t - 1:t, :]
        att = jnp.where(causal, _dot_nt(q * jnp.exp(a - a_mid), kr * jnp.exp(a_mid - a)), 0.0)
        qt = q * jnp.exp(a)
        kt = kr * jnp.exp(a_end - a)
        s0 = s0_ref[0, hh]
        o = _dot(att, v) + _dot(qt, s0)
        ktd = jnp.concatenate([kt, jnp.exp(a_end), jnp.zeros((t - 1, REC_HEAD_DIM), F32)], axis=0).T
        ns_ref[0, hh] = s0 * ktd[:, t:t + 1] + _dot(ktd[:, 0:t], v)
        on = o * lax.rsqrt(jnp.mean(o * o, axis=-1, keepdims=True) + EPS)
        gr = z_ref[:, OFF_GR + cs:OFF_GR + cs + REC_HEAD_DIM]
        yr = (on * recw_ref[:, cs:cs + REC_HEAD_DIM]) * _silu(gr)
        y_ref[:, D_ATTN + cs:D_ATTN + cs + REC_HEAD_DIM] = yr.astype(BF16)


def _dec_mix(layer, z, ck, cv, s0, lb, oml, rec_w, sinks, bias_c, bias_n):
    t = DEC_SEQ
    const2 = lambda s: (0, 0)
    per_layer = lambda s: (layer, 0, 0)
    kv_in = pl.BlockSpec((None, 1, WINDOW, KV_DIM), lambda s: (layer, s, 0, 0))
    st_in = pl.BlockSpec((None, 1, N_REC_HEADS, REC_HEAD_DIM, REC_HEAD_DIM), lambda s: (layer, s, 0, 0, 0))
    return pl.pallas_call(
        functools.partial(_dec_mix_kernel, layer),
        grid=(DEC_BATCH,),
        in_specs=[
            pl.BlockSpec(memory_space=pltpu.SMEM),
            pl.BlockSpec((t, D_IN), lambda s: (s, 0)),
            kv_in,
            kv_in,
            st_in,
            pl.BlockSpec((None, 1, D_REC), per_layer),
            pl.BlockSpec((None, 1, D_REC), per_layer),
            pl.BlockSpec((None, 1, D_REC), per_layer),
            pl.BlockSpec((N_HEADS * t, WINDOW), const2),
            pl.BlockSpec((N_HEADS * t, t), const2),
        ],
        out_specs=[
            pl.BlockSpec((t, D_MODEL), lambda s: (s, 0)),
            pl.BlockSpec((1, WINDOW, KV_DIM), lambda s: (s, 0, 0)),
            pl.BlockSpec((1, WINDOW, KV_DIM), lambda s: (s, 0, 0)),
            pl.BlockSpec((1, N_REC_HEADS, REC_HEAD_DIM, REC_HEAD_DIM), lambda s: (s, 0, 0, 0)),
        ],
        out_shape=[
            jax.ShapeDtypeStruct((N_DEC, D_MODEL), BF16),
            jax.ShapeDtypeStruct((DEC_BATCH, WINDOW, KV_DIM), F32),
            jax.ShapeDtypeStruct((DEC_BATCH, WINDOW, KV_DIM), F32),
            jax.ShapeDtypeStruct((DEC_BATCH, N_REC_HEADS, REC_HEAD_DIM, REC_HEAD_DIM), F32),
        ],
        compiler_params=pltpu.CompilerParams(dimension_semantics=("arbitrary",)),
        name="dec_mix",
    )(sinks, z, ck, cv, s0, lb, oml, rec_w, bias_c, bias_n)


def _final_norm_kernel(x_ref, w_ref, o_ref):
    o_ref[...] = _rmsnorm_rows(x_ref[...], w_ref[...])


def _final_norm_dec(hs, w):
    return pl.pallas_call(
        _final_norm_kernel,
        out_shape=jax.ShapeDtypeStruct((N_DEC, D_MODEL), F32),
        name="final_norm_dec",
    )(hs, w)


def _t5_bucket(dist):
    max_exact = N_BUCKETS // 2
    d = jnp.maximum(dist, 0)
    df = jnp.maximum(d, 1).astype(F32)
    large = max_exact + (jnp.log(df / max_exact) / math.log(MAX_DISTANCE / max_exact)
                         * (N_BUCKETS - max_exact)).astype(jnp.int32)
    large = jnp.minimum(large, N_BUCKETS - 1)
    return jnp.where(d < max_exact, d, large)


def _bias_tables(rel_bias_table):
    bias_d = rel_bias_table[_t5_bucket(jnp.arange(WINDOW))].T.astype(F32)
    period = 3 * WINDOW + 1
    u = jnp.concatenate([jnp.full((N_HEADS, 1), NEG, F32), bias_d[:, ::-1],
                         jnp.full((N_HEADS, period - 1 - WINDOW), NEG, F32)], axis=1)
    rows = jnp.tile(u, (1, WINDOW))[:, :WINDOW * (period - 1)].reshape(N_HEADS, WINDOW, period - 1)
    prompt = rows[:, :, :2 * WINDOW].reshape(N_HEADS * WINDOW, 2 * WINDOW)
    dec = rows[:, :DEC_SEQ, :WINDOW + DEC_SEQ].reshape(N_HEADS * DEC_SEQ, WINDOW + DEC_SEQ)
    return prompt, dec[:, :WINDOW], dec[:, WINDOW:]


def kernel(x_prompt, x_sample, cache_k, cache_v, state_h, meta_tokens, w_in, w_out, norm_w, final_norm_w,
           attn_sinks, rel_bias_table, hgrn_lb_logits, hgrn_norm_w):
    w_in_b = w_in.astype(BF16)
    w_out_b = w_out.astype(BF16)
    pl_ = jax.nn.softmax(hgrn_lb_logits.astype(F32), axis=0)
    lb = (jnp.cumsum(pl_, axis=0) - pl_[0:1]).reshape(DEPTH, 1, D_REC)
    oml = 1.0 - lb
    bias_p, bias_dc, bias_dn = _bias_tables(rel_bias_table)
    nw = norm_w.astype(F32).reshape(DEPTH, 1, D_MODEL)
    rw = hgrn_norm_w.astype(F32).reshape(DEPTH, 1, D_REC)
    fw = final_norm_w.astype(F32).reshape(1, D_MODEL)
    sinks = attn_sinks.astype(F32)

    hp = (meta_tokens.astype(F32), x_prompt)
    hs = x_sample.reshape(N_DEC, D_MODEL)
    ck = cache_k.reshape(DEPTH, DEC_BATCH, WINDOW, KV_DIM)
    cv = cache_v.reshape(DEPTH, DEC_BATCH, WINDOW, KV_DIM)

    pk, pv, ps, sk, sv, ss = [], [], [], [], [], []
    for l in range(DEPTH):
        hp, k_l, v_l, s_l = _prompt_layer(l, hp, w_in_b, w_out_b, nw, lb, oml, rw, sinks, bias_p, fw)
        pk.append(k_l)
        pv.append(v_l)
        ps.append(s_l)

        z = _dec_proj(l, hs, nw, w_in_b)
        y, nk, nv, ns = _dec_mix(l, z, ck, cv, state_h, lb, oml, rw, sinks, bias_dc, bias_dn)
        hs = _dec_out(l, hs, y, w_out_b)
        sk.append(nk)
        sv.append(nv)
        ss.append(ns)

    y_prompt = hp
    y_sample = _final_norm_dec(hs, fw).reshape(DEC_BATCH, DEC_SEQ, D_MODEL)
    kv_shape = (DEPTH, -1, WINDOW, N_KV, HEAD_DIM)
    return (y_prompt, y_sample,
            jnp.stack(pk).reshape(kv_shape), jnp.stack(pv).reshape(kv_shape),
            jnp.swapaxes(jnp.stack(ps), -1, -2),
            jnp.stack(sk).reshape(kv_shape), jnp.stack(sv).reshape(kv_shape), jnp.stack(ss))
```

```python
import functools
import math

import jax
import jax.numpy as jnp
from jax import lax
from jax.experimental import pallas as pl
from jax.experimental.pallas import tpu as pltpu

D_MODEL = 2048
BATCH = 4
SEQ = 2048
DEPTH = 4
DEC_BATCH = 32
DEC_SEQ = 8
N_META = 16
D_ATTN = 1024
D_REC = 1024
HEAD_DIM = 64
N_HEADS = 16
N_KV = 4
GROUP = 4
KV_DIM = 256
WINDOW = 128
N_BUCKETS = 32
MAX_DISTANCE = 128
REC_HEAD_DIM = 128
N_REC_HEADS = 8
EPS = 1e-6
D_IN = 2 * D_ATTN + 2 * KV_DIM + 4 * D_REC

OFF_QA = 0
OFF_KA = D_ATTN
OFF_VA = OFF_KA + KV_DIM
OFF_GA = OFF_VA + KV_DIM
OFF_QR = OFF_GA + D_ATTN
OFF_FR = OFF_QR + D_REC
OFF_IR = OFF_FR + D_REC
OFF_GR = OFF_IR + D_REC
ATT_COLS = OFF_QR

TILE = 256
QBLK = WINDOW
CHUNK = 64
HALF = CHUNK // 2
ROW_TILES = SEQ // TILE
PAD = TILE - N_META
PROJ_COLS = 512
REC_GROUP = PROJ_COLS // REC_HEAD_DIM
NEG = -1e30
N_DEC = DEC_BATCH * DEC_SEQ
V7X_VMEM_BYTES = 64 * 1024 * 1024
VMEM_LIMIT = V7X_VMEM_BYTES - 2 * 1024 * 1024

F32 = jnp.float32
BF16 = jnp.bfloat16


def _sigmoid_pair(x):
    e = jnp.exp(-jnp.abs(x))
    inv = 1.0 / (1.0 + e)
    pos = x >= 0
    return jnp.where(pos, inv, e * inv), jnp.where(pos, e * inv, inv)


def _silu(x):
    s, _ = _sigmoid_pair(x)
    return x * s


def _cumsum_rows(x):
    n = x.shape[0]
    row = lax.broadcasted_iota(jnp.int32, x.shape, 0)
    k = 1
    while k < n:
        x = x + jnp.where(row >= k, pltpu.roll(x, k, axis=0), 0.0)
        k *= 2
    return x


def _rmsnorm_rows(x, w):
    ms = jnp.mean(x * x, axis=-1, keepdims=True)
    return (x * lax.rsqrt(ms + EPS)) * w


def _dot_nt(a, b):
    return lax.dot_general(a, b, (((1,), (1,)), ((), ())), preferred_element_type=F32)


def _dot_tn(a, b):
    return lax.dot_general(a, b, (((0,), (0,)), ((), ())), preferred_element_type=F32)


def _dot(a, b):
    return jnp.dot(a, b, preferred_element_type=F32)


def _gate_features(fr, lb, oml):
    sig, sneg = _sigmoid_pair(fr)
    logf = jnp.log(lb + oml * sig)
    return logf, oml * sneg


def _attention_unit(layer, qb, kh, za_ref, kv_ref, y_ref, bias_ref, sink_ref, c_tile):
    r0 = qb * QBLK
    kpos = c_tile + r0 - QBLK + lax.broadcasted_iota(jnp.int32, (1, 2 * QBLK), 1)
    kvalid = kpos >= 0
    kk = kv_ref[r0:r0 + 2 * QBLK, kh * HEAD_DIM:(kh + 1) * HEAD_DIM]
    q4 = jnp.concatenate(
        [za_ref[r0:r0 + QBLK, (kh * GROUP + g) * HEAD_DIM:(kh * GROUP + g + 1) * HEAD_DIM]
         for g in range(GROUP)], axis=0)
    s = _dot_nt((q4 * (HEAD_DIM ** -0.5)).astype(BF16), kk)
    yield
    s = s + bias_ref[kh * GROUP * QBLK:(kh + 1) * GROUP * QBLK, :]
    s = jnp.where(kvalid, s, NEG)
    es, dens = [], []
    for g in range(GROUP):
        sink = sink_ref[layer, kh * GROUP + g]
        sg = s[g * QBLK:(g + 1) * QBLK]
        m = jnp.maximum(jnp.max(sg, axis=-1, keepdims=True), sink)
        e = jnp.exp(sg - m)
        dens.append(jnp.sum(e, axis=-1, keepdims=True) + jnp.exp(sink - m))
        es.append(e.astype(BF16))
    vv = kv_ref[r0:r0 + 2 * QBLK, KV_DIM + kh * HEAD_DIM:KV_DIM + (kh + 1) * HEAD_DIM]
    o4 = _dot(jnp.concatenate(es, axis=0), vv)
    yield
    outs = []
    for g in range(GROUP):
        h = kh * GROUP + g
        og = o4[g * QBLK:(g + 1) * QBLK] / dens[g]
        ga = za_ref[r0:r0 + QBLK, OFF_GA + h * HEAD_DIM:OFF_GA + (h + 1) * HEAD_DIM]
        outs.append(og * _silu(ga))
    y_ref[r0:r0 + QBLK, kh * GROUP * HEAD_DIM:(kh + 1) * GROUP * HEAD_DIM] = (
        jnp.concatenate(outs, axis=-1).astype(BF16))


def _hgrn2_unit(ci, hh, zr_ref, st_ref, y_ref, lb_ref, oml_ref, recw_ref, c_tile):
    row = lax.broadcasted_iota(jnp.int32, (CHUNK, CHUNK), 0)
    col = lax.broadcasted_iota(jnp.int32, (CHUNK, CHUNK), 1)
    causal = col <= row
    r0 = ci * CHUNK
    rows = slice(r0, r0 + CHUNK)
    rpos = c_tile + r0 + lax.broadcasted_iota(jnp.int32, (CHUNK, 1), 0)
    rvalid = rpos >= 0
    cs = hh * REC_HEAD_DIM
    zc = (hh % REC_GROUP) * REC_HEAD_DIM
    lb = lb_ref[:, cs:cs + REC_HEAD_DIM]
    oml = oml_ref[:, cs:cs + REC_HEAD_DIM]
    logf, kr = _gate_features(zr_ref[rows, PROJ_COLS + zc:PROJ_COLS + zc + REC_HEAD_DIM], lb, oml)
    q = _silu(zr_ref[rows, zc:zc + REC_HEAD_DIM])
    v = jnp.where(rvalid, zr_ref[rows, 2 * PROJ_COLS + zc:2 * PROJ_COLS + zc + REC_HEAD_DIM], 0.0)
    vb = v.astype(BF16)
    a = _cumsum_rows(logf)
    a_mid = a[HALF - 1:HALF, :]
    a_end = a[CHUNK - 1:CHUNK, :]
    qh = (q * jnp.exp(a - a_mid)).astype(BF16)
    kh_ = (kr * jnp.exp(a_mid - a)).astype(BF16)
    scores = _dot_nt(qh, kh_)
    qt = (q * jnp.exp(a)).astype(BF16)
    kt = (kr * jnp.exp(a_end - a)).astype(BF16)
    st = st_ref[hh]
    o_state = _dot_nt(qt, st.astype(BF16))
    yield
    att = jnp.where(causal, scores, 0.0)
    o = _dot(att.astype(BF16), vb) + o_state
    st_ref[hh] = st * jnp.exp(a_end) + _dot_tn(vb, kt)
    yield
    on = o * lax.rsqrt(jnp.mean(o * o, axis=-1, keepdims=True) + EPS)
    gr = zr_ref[rows, 3 * PROJ_COLS + zc:3 * PROJ_COLS + zc + REC_HEAD_DIM]
    yr = (on * recw_ref[:, cs:cs + REC_HEAD_DIM]) * _silu(gr)
    y_ref[rows, D_ATTN + cs:D_ATTN + cs + REC_HEAD_DIM] = yr.astype(BF16)


def _emit_staggered(units, fillers=(), lag=1):
    n = len(units)
    rounds = n + 2 * lag
    at_round = [[] for _ in range(rounds)]
    for k, f in enumerate(fillers):
        at_round[k * rounds // len(fillers)].append(f)
    for i in range(rounds):
        if i < n:
            next(units[i])
        for f in at_round[i]:
            f()
        if 0 <= i - lag < n:
            next(units[i - lag])
        if 0 <= i - 2 * lag < n:
            next(units[i - 2 * lag], None)


def _main_tile(s):
    t = jnp.maximum(s - 1, 0)
    return lax.div(t, ROW_TILES), lax.rem(t, ROW_TILES)


def _prompt_layer_kernel(layer, final, sink_ref, meta_ref, hin_ref, win_ref, wout_ref, normw_ref, lb_ref, oml_ref,
                         recw_ref, bias_ref, finw_ref, *refs):
    if final:
        hout_ref, pk_ref, pv_ref, ps_ref = refs[:4]
        refs = refs[4:]
    else:
        meta_out_ref, hout_ref, pk_ref, pv_ref, ps_ref = refs[:5]
        refs = refs[5:]
    xn_ref, za_ref, zr_ref, kv_ref, st_ref, y_ref, kv_meta_ref, st_meta_ref = refs
    s = pl.program_id(0)
    is_meta = s == 0
    _, j = _main_tile(s)

    @pl.when(is_meta)
    def _():
        kv_ref[0:QBLK, :] = jnp.zeros((QBLK, 2 * KV_DIM), BF16)
        st_ref[...] = jnp.zeros(st_ref.shape, F32)

    @pl.when(jnp.logical_and(jnp.logical_not(is_meta), j == 0))
    def _():
        kv_ref[0:QBLK, :] = kv_meta_ref[...]
        st_ref[...] = st_meta_ref[...]

    def tile_input(cols=slice(None)):
        meta = meta_ref[:, cols]
        meta_rows = jnp.concatenate([jnp.zeros((PAD, meta.shape[1]), F32), meta], axis=0)
        return jnp.where(is_meta, meta_rows, hin_ref[0, :, cols])

    c_tile = jnp.where(is_meta, -PAD, j * TILE + N_META)
    xn_ref[...] = _rmsnorm_rows(tile_input(), normw_ref[...]).astype(BF16)

    def project(dst_ref, dst_col, src_col):
        dst_ref[:, dst_col:dst_col + PROJ_COLS] = _dot(xn_ref[...], win_ref[:, src_col:src_col + PROJ_COLS])

    def project_rec(dst_ref, group):
        return [functools.partial(project, dst_ref, part * PROJ_COLS, off + group * PROJ_COLS)
                for part, off in enumerate((OFF_QR, OFF_FR, OFF_IR, OFF_GR))]

    def out_proj(k0, k1, c0, first_half):
        cols = slice(c0, c0 + PROJ_COLS)
        base = tile_input(cols) if first_half else hout_ref[0, :, cols]
        hout_ref[0, :, cols] = base + _dot(y_ref[:, k0:k1], wout_ref[k0:k1, cols])

    for c0 in range(0, ATT_COLS, PROJ_COLS):
        project(za_ref, c0, c0)
    kv_ref[QBLK:QBLK + TILE, :] = za_ref[:, OFF_KA:OFF_KA + 2 * KV_DIM].astype(BF16)
    pk_ref[0] = za_ref[TILE - WINDOW:TILE, OFF_KA:OFF_KA + KV_DIM]
    pv_ref[0] = za_ref[TILE - WINDOW:TILE, OFF_VA:OFF_VA + KV_DIM]

    att_units = [_attention_unit(layer, qb, kh, za_ref, kv_ref, y_ref, bias_ref, sink_ref, c_tile)
                 for qb in range(TILE // QBLK) for kh in range(N_KV)]
    _emit_staggered(att_units, project_rec(zr_ref, 0))
    kv_ref[0:QBLK, :] = kv_ref[TILE:TILE + QBLK, :]

    n_groups = N_REC_HEADS // REC_GROUP
    z_bufs = [zr_ref, za_ref]
    for grp in range(n_groups):
        zsrc = z_bufs[grp % 2]
        units = [_hgrn2_unit(ci, grp * REC_GROUP + hl, zsrc, st_ref, y_ref, lb_ref, oml_ref, recw_ref, c_tile)
                 for ci in range(TILE // CHUNK) for hl in range(REC_GROUP)]
        if grp + 1 < n_groups:
            fillers = project_rec(z_bufs[(grp + 1) % 2], grp + 1)
        else:
            fillers = [functools.partial(out_proj, 0, D_ATTN, c0, True) for c0 in range(0, D_MODEL, PROJ_COLS)]
        _emit_staggered(units, fillers)
    for c0 in range(0, D_MODEL, PROJ_COLS):
        out_proj(D_ATTN, D_MODEL, c0, False)
    ps_ref[0] = st_ref[...]

    @pl.when(is_meta)
    def _():
        kv_meta_ref[...] = kv_ref[0:QBLK, :]
        st_meta_ref[...] = st_ref[...]
        if not final:
            meta_out_ref[...] = hout_ref[0, PAD:TILE, :]

    if final:
        @pl.when(jnp.logical_not(is_meta))
        def _():
            hout_ref[0] = _rmsnorm_rows(hout_ref[0], finw_ref[...])


def _layer_block(shape, layer):
    nd = len(shape)
    return pl.BlockSpec((None,) + tuple(shape), lambda s: (layer,) + (0,) * nd, pipeline_mode=pl.Buffered(1))


def _const_block(shape):
    nd = len(shape)
    return pl.BlockSpec(tuple(shape), lambda s: (0,) * nd, pipeline_mode=pl.Buffered(1))


def _prompt_layer(layer, h_meta, h_main, w_in, w_out, norm_w, lb, oml, rec_w, sinks, bias, fin_w):
    final = layer == DEPTH - 1
    main = pl.BlockSpec((1, TILE, D_MODEL), lambda s: _main_tile(s) + (0,))
    meta = pl.BlockSpec((N_META, D_MODEL), lambda s: (0, 0))
    kv_spec = pl.BlockSpec((1, WINDOW, KV_DIM), lambda s: (_main_tile(s)[0], 0, 0))
    st_spec = pl.BlockSpec((1, N_REC_HEADS, REC_HEAD_DIM, REC_HEAD_DIM), lambda s: (_main_tile(s)[0], 0, 0, 0))
    main_shape = jax.ShapeDtypeStruct((BATCH, SEQ, D_MODEL), F32)
    meta_shape = jax.ShapeDtypeStruct((N_META, D_MODEL), F32)
    return pl.pallas_call(
        functools.partial(_prompt_layer_kernel, layer, final),
        grid=(1 + BATCH * ROW_TILES,),
        in_specs=[
            pl.BlockSpec(memory_space=pltpu.SMEM),
            _const_block((N_META, D_MODEL)),
            main,
            _layer_block((D_MODEL, D_IN), layer),
            _layer_block((D_MODEL, D_MODEL), layer),
            _layer_block((1, D_MODEL), layer),
            _layer_block((1, D_REC), layer),
            _layer_block((1, D_REC), layer),
            _layer_block((1, D_REC), layer),
            _const_block((N_HEADS * QBLK, 2 * QBLK)),
            _const_block((1, D_MODEL)),
        ],
        out_specs=([] if final else [meta]) + [main, kv_spec, kv_spec, st_spec],
        out_shape=([] if final else [meta_shape]) + [
            main_shape,
            jax.ShapeDtypeStruct((BATCH, WINDOW, KV_DIM), F32),
            jax.ShapeDtypeStruct((BATCH, WINDOW, KV_DIM), F32),
            jax.ShapeDtypeStruct((BATCH, N_REC_HEADS, REC_HEAD_DIM, REC_HEAD_DIM), F32),
        ],
        scratch_shapes=[
            pltpu.VMEM((TILE, D_MODEL), BF16),
            pltpu.VMEM((TILE, ATT_COLS), F32),
            pltpu.VMEM((TILE, 4 * PROJ_COLS), F32),
            pltpu.VMEM((QBLK + TILE, 2 * KV_DIM), BF16),
            pltpu.VMEM((N_REC_HEADS, REC_HEAD_DIM, REC_HEAD_DIM), F32),
            pltpu.VMEM((TILE, D_MODEL), BF16),
            pltpu.VMEM((QBLK, 2 * KV_DIM), BF16),
            pltpu.VMEM((N_REC_HEADS, REC_HEAD_DIM, REC_HEAD_DIM), F32),
        ],
        compiler_params=pltpu.CompilerParams(
            dimension_semantics=("arbitrary",), vmem_limit_bytes=VMEM_LIMIT),
        name="prompt_layer",
    )(sinks, h_meta, h_main, w_in, w_out, norm_w, lb, oml, rec_w, bias, fin_w)


def _dec_proj_kernel(hs_ref, normw_ref, win_ref, z_ref):
    xn = _rmsnorm_rows(hs_ref[...], normw_ref[...]).astype(BF16)
    z_ref[...] = _dot(xn, win_ref[...])


def _dec_proj(layer, hs, norm_w, w_in):
    return pl.pallas_call(
        _dec_proj_kernel,
        grid=(D_IN // PROJ_COLS,),
        in_specs=[
            pl.BlockSpec((N_DEC, D_MODEL), lambda n: (0, 0)),
            pl.BlockSpec((None, 1, D_MODEL), lambda n: (layer, 0, 0)),
            pl.BlockSpec((None, D_MODEL, PROJ_COLS), lambda n: (layer, 0, n)),
        ],
        out_specs=pl.BlockSpec((N_DEC, PROJ_COLS), lambda n: (0, n)),
        out_shape=jax.ShapeDtypeStruct((N_DEC, D_IN), F32),
        compiler_params=pltpu.CompilerParams(dimension_semantics=("arbitrary",)),
        name="dec_proj",
    )(hs, norm_w, w_in)


def _dec_out_kernel(hs_ref, y_ref, wout_ref, o_ref):
    o_ref[...] = hs_ref[...] + _dot(y_ref[...], wout_ref[...])


def _dec_out(layer, hs, y, w_out):
    return pl.pallas_call(
        _dec_out_kernel,
        grid=(D_MODEL // PROJ_COLS,),
        in_specs=[
            pl.BlockSpec((N_DEC, PROJ_COLS), lambda n: (0, n)),
            pl.BlockSpec((N_DEC, D_MODEL), lambda n: (0, 0)),
            pl.BlockSpec((None, D_MODEL, PROJ_COLS), lambda n: (layer, 0, n)),
        ],
        out_specs=pl.BlockSpec((N_DEC, PROJ_COLS), lambda n: (0, n)),
        out_shape=jax.ShapeDtypeStruct((N_DEC, D_MODEL), F32),
        compiler_params=pltpu.CompilerParams(dimension_semantics=("arbitrary",)),
        name="dec_out",
    )(hs, y, w_out)


DEC_SEQS_PER_STEP = 4
DEC_LAG = 8


def _dec_attention_unit(layer, q, kh, sink_ref, z_ref, ck_ref, cv_ref, biasc_ref, biasn_ref, y_ref):
    t = DEC_SEQ
    tok = slice(q * t, (q + 1) * t)
    hs_ = slice(kh * HEAD_DIM, (kh + 1) * HEAD_DIM)
    kc = ck_ref[q, :, hs_].astype(BF16)
    kn = z_ref[tok, OFF_KA + kh * HEAD_DIM:OFF_KA + (kh + 1) * HEAD_DIM]
    q4 = jnp.concatenate(
        [z_ref[tok, (kh * GROUP + g) * HEAD_DIM:(kh * GROUP + g + 1) * HEAD_DIM] for g in range(GROUP)],
        axis=0) * (HEAD_DIM ** -0.5)
    rows = slice(kh * GROUP * t, (kh + 1) * GROUP * t)
    sc = _dot_nt(q4.astype(BF16), kc)
    sn = _dot_nt(q4, kn)
    yield
    sc = sc + biasc_ref[rows, :]
    sn = sn + biasn_ref[rows, :]
    ecs, ens, dens = [], [], []
    for g in range(GROUP):
        sink = sink_ref[layer, kh * GROUP + g]
        scg = sc[g * t:(g + 1) * t]
        sng = sn[g * t:(g + 1) * t]
        m = jnp.maximum(jnp.maximum(jnp.max(scg, axis=-1, keepdims=True),
                                    jnp.max(sng, axis=-1, keepdims=True)), sink)
        ec = jnp.exp(scg - m)
        en = jnp.exp(sng - m)
        dens.append(jnp.sum(ec, axis=-1, keepdims=True) + jnp.sum(en, axis=-1, keepdims=True)
                    + jnp.exp(sink - m))
        ecs.append(ec)
        ens.append(en)
    vc = cv_ref[q, :, hs_]
    vn = z_ref[tok, OFF_VA + kh * HEAD_DIM:OFF_VA + (kh + 1) * HEAD_DIM]
    o4 = (_dot(jnp.concatenate(ecs, axis=0).astype(BF16), vc.astype(BF16))
          + _dot(jnp.concatenate(ens, axis=0), vn))
    yield
    outs = []
    for g in range(GROUP):
        h = kh * GROUP + g
        og = o4[g * t:(g + 1) * t] / dens[g]
        ga = z_ref[tok, OFF_GA + h * HEAD_DIM:OFF_GA + (h + 1) * HEAD_DIM]
        outs.append(og * _silu(ga))
    y_ref[tok, kh * GROUP * HEAD_DIM:(kh + 1) * GROUP * HEAD_DIM] = jnp.concatenate(outs, axis=-1).astype(BF16)


def _dec_hgrn2_unit(q, hh, z_ref, s0_ref, lb_ref, oml_ref, recw_ref, y_ref, ns_ref):
    t = DEC_SEQ
    tok = slice(q * t, (q + 1) * t)
    row = lax.broadcasted_iota(jnp.int32, (t, t), 0)
    col = lax.broadcasted_iota(jnp.int32, (t, t), 1)
    causal = col <= row
    cs = hh * REC_HEAD_DIM
    lb = lb_ref[:, cs:cs + REC_HEAD_DIM]
    oml = oml_ref[:, cs:cs + REC_HEAD_DIM]
    logf, kr = _gate_features(z_ref[tok, OFF_FR + cs:OFF_FR + cs + REC_HEAD_DIM], lb, oml)
    qv = _silu(z_ref[tok, OFF_QR + cs:OFF_QR + cs + REC_HEAD_DIM])
    v = z_ref[tok, OFF_IR + cs:OFF_IR + cs + REC_HEAD_DIM]
    a = _cumsum_rows(logf)
    a_mid = a[t // 2 - 1:t // 2, :]
    a_end = a[t - 1:t, :]
    scores = _dot_nt(qv * jnp.exp(a - a_mid), kr * jnp.exp(a_mid - a))
    qt = qv * jnp.exp(a)
    kt = kr * jnp.exp(a_end - a)
    s0 = s0_ref[q, hh]
    o_state = _dot(qt, s0)
    yield
    att = jnp.where(causal, scores, 0.0)
    o = _dot(att, v) + o_state
    ktd = jnp.concatenate([kt, jnp.exp(a_end), jnp.zeros((t - 1, REC_HEAD_DIM), F32)], axis=0).T
    ns_ref[q, hh] = s0 * ktd[:, t:t + 1] + _dot(ktd[:, 0:t], v)
    yield
    on = o * lax.rsqrt(jnp.mean(o * o, axis=-1, keepdims=True) + EPS)
    gr = z_ref[tok, OFF_GR + cs:OFF_GR + cs + REC_HEAD_DIM]
    yr = (on * recw_ref[:, cs:cs + REC_HEAD_DIM]) * _silu(gr)
    y_ref[tok, D_ATTN + cs:D_ATTN + cs + REC_HEAD_DIM] = yr.astype(BF16)


def _dec_mix_kernel(layer, sink_ref, z_ref, ck_ref, cv_ref, s0_ref, lb_ref, oml_ref, recw_ref, biasc_ref,
                    biasn_ref, y_ref, nk_ref, nv_ref, ns_ref):
    t = DEC_SEQ
    for q in range(DEC_SEQS_PER_STEP):
        tok = slice(q * t, (q + 1) * t)
        nk_ref[q, 0:WINDOW - t, :] = ck_ref[q, t:WINDOW, :]
        nk_ref[q, WINDOW - t:WINDOW, :] = z_ref[tok, OFF_KA:OFF_KA + KV_DIM]
        nv_ref[q, 0:WINDOW - t, :] = cv_ref[q, t:WINDOW, :]
        nv_ref[q, WINDOW - t:WINDOW, :] = z_ref[tok, OFF_VA:OFF_VA + KV_DIM]
    units = []
    for kh in range(N_KV):
        units += [_dec_attention_unit(layer, q, kh, sink_ref, z_ref, ck_ref, cv_ref, biasc_ref, biasn_ref, y_ref)
                  for q in range(DEC_SEQS_PER_STEP)]
    for hh in range(N_REC_HEADS):
        units += [_dec_hgrn2_unit(q, hh, z_ref, s0_ref, lb_ref, oml_ref, recw_ref, y_ref, ns_ref)
                  for q in range(DEC_SEQS_PER_STEP)]
    _emit_staggered(units, lag=DEC_LAG)


def _dec_mix(layer, z, ck, cv, s0, lb, oml, rec_w, sinks, bias_c, bias_n):
    t = DEC_SEQ
    nq = DEC_SEQS_PER_STEP
    const2 = lambda s: (0, 0)
    per_layer = lambda s: (layer, 0, 0)
    kv_in = pl.BlockSpec((None, nq, WINDOW, KV_DIM), lambda s: (layer, s, 0, 0))
    st_in = pl.BlockSpec((None, nq, N_REC_HEADS, REC_HEAD_DIM, REC_HEAD_DIM), lambda s: (layer, s, 0, 0, 0))
    return pl.pallas_call(
        functools.partial(_dec_mix_kernel, layer),
        grid=(DEC_BATCH // nq,),
        in_specs=[
            pl.BlockSpec(memory_space=pltpu.SMEM),
            pl.BlockSpec((nq * t, D_IN), lambda s: (s, 0)),
            kv_in,
            kv_in,
            st_in,
            pl.BlockSpec((None, 1, D_REC), per_layer),
            pl.BlockSpec((None, 1, D_REC), per_layer),
            pl.BlockSpec((None, 1, D_REC), per_layer),
            pl.BlockSpec((N_HEADS * t, WINDOW), const2),
            pl.BlockSpec((N_HEADS * t, t), const2),
        ],
        out_specs=[
            pl.BlockSpec((nq * t, D_MODEL), lambda s: (s, 0)),
            pl.BlockSpec((nq, WINDOW, KV_DIM), lambda s: (s, 0, 0)),
            pl.BlockSpec((nq, WINDOW, KV_DIM), lambda s: (s, 0, 0)),
            pl.BlockSpec((nq, N_REC_HEADS, REC_HEAD_DIM, REC_HEAD_DIM), lambda s: (s, 0, 0, 0)),
        ],
        out_shape=[
            jax.ShapeDtypeStruct((N_DEC, D_MODEL), BF16),
            jax.ShapeDtypeStruct((DEC_BATCH, WINDOW, KV_DIM), F32),
            jax.ShapeDtypeStruct((DEC_BATCH, WINDOW, KV_DIM), F32),
            jax.ShapeDtypeStruct((DEC_BATCH, N_REC_HEADS, REC_HEAD_DIM, REC_HEAD_DIM), F32),
        ],
        compiler_params=pltpu.CompilerParams(dimension_semantics=("arbitrary",)),
        name="dec_mix",
    )(sinks, z, ck, cv, s0, lb, oml, rec_w, bias_c, bias_n)


def _final_norm_kernel(x_ref, w_ref, o_ref):
    o_ref[...] = _rmsnorm_rows(x_ref[...], w_ref[...])


def _final_norm_dec(hs, w):
    return pl.pallas_call(
        _final_norm_kernel,
        out_shape=jax.ShapeDtypeStruct((N_DEC, D_MODEL), F32),
        name="final_norm_dec",
    )(hs, w)


def _t5_bucket(dist):
    max_exact = N_BUCKETS // 2
    d = jnp.maximum(dist, 0)
    df = jnp.maximum(d, 1).astype(F32)
    large = max_exact + (jnp.log(df / max_exact) / math.log(MAX_DISTANCE / max_exact)
                         * (N_BUCKETS - max_exact)).astype(jnp.int32)
    large = jnp.minimum(large, N_BUCKETS - 1)
    return jnp.where(d < max_exact, d, large)


def _bias_tables(rel_bias_table):
    bias_d = rel_bias_table[_t5_bucket(jnp.arange(WINDOW))].T.astype(F32)
    period = 3 * WINDOW + 1
    u = jnp.concatenate([jnp.full((N_HEADS, 1), NEG, F32), bias_d[:, ::-1],
                         jnp.full((N_HEADS, period - 1 - WINDOW), NEG, F32)], axis=1)
    rows = jnp.tile(u, (1, WINDOW))[:, :WINDOW * (period - 1)].reshape(N_HEADS, WINDOW, period - 1)
    prompt = rows[:, :, :2 * WINDOW].reshape(N_HEADS * WINDOW, 2 * WINDOW)
    dec = rows[:, :DEC_SEQ, :WINDOW + DEC_SEQ].reshape(N_HEADS * DEC_SEQ, WINDOW + DEC_SEQ)
    return prompt, dec[:, :WINDOW], dec[:, WINDOW:]


def kernel(x_prompt, x_sample, cache_k, cache_v, state_h, meta_tokens, w_in, w_out, norm_w, final_norm_w,
           attn_sinks, rel_bias_table, hgrn_lb_logits, hgrn_norm_w):
    w_in_b = w_in.astype(BF16)
    w_out_b = w_out.astype(BF16)
    pl_ = jax.nn.softmax(hgrn_lb_logits.astype(F32), axis=0)
    lb = (jnp.cumsum(pl_, axis=0) - pl_[0:1]).reshape(DEPTH, 1, D_REC)
    oml = 1.0 - lb
    bias_p, bias_dc, bias_dn = _bias_tables(rel_bias_table)
    nw = norm_w.astype(F32).reshape(DEPTH, 1, D_MODEL)
    rw = hgrn_norm_w.astype(F32).reshape(DEPTH, 1, D_REC)
    fw = final_norm_w.astype(F32).reshape(1, D_MODEL)
    sinks = attn_sinks.astype(F32)

    h_meta = meta_tokens.astype(F32)
    h_main = x_prompt
    hs = x_sample.reshape(N_DEC, D_MODEL)
    ck = cache_k.reshape(DEPTH, DEC_BATCH, WINDOW, KV_DIM)
    cv = cache_v.reshape(DEPTH, DEC_BATCH, WINDOW, KV_DIM)

    pk, pv, ps, sk, sv, ss = [], [], [], [], [], []
    for l in range(DEPTH):
        outs = _prompt_layer(l, h_meta, h_main, w_in_b, w_out_b, nw, lb, oml, rw, sinks, bias_p, fw)
        if l < DEPTH - 1:
            h_meta, h_main, k_l, v_l, s_l = outs
        else:
            h_main, k_l, v_l, s_l = outs
        pk.append(k_l)
        pv.append(v_l)
        ps.append(s_l)

        z = _dec_proj(l, hs, nw, w_in_b)
        y, nk, nv, ns = _dec_mix(l, z, ck, cv, state_h, lb, oml, rw, sinks, bias_dc, bias_dn)
        hs = _dec_out(l, hs, y, w_out_b)
        sk.append(nk)
        sv.append(nv)
        ss.append(ns)

    y_prompt = h_main
    y_sample = _final_norm_dec(hs, fw).reshape(DEC_BATCH, DEC_SEQ, D_MODEL)
    kv_shape = (DEPTH, -1, WINDOW, N_KV, HEAD_DIM)
    return (y_prompt, y_sample,
            jnp.stack(pk).reshape(kv_shape), jnp.stack(pv).reshape(kv_shape),
            jnp.swapaxes(jnp.stack(ps), -1, -2),
            jnp.stack(sk).reshape(kv_shape), jnp.stack(sv).reshape(kv_shape), jnp.stack(ss))
```

```python
import functools
import math

import jax
import jax.numpy as jnp
from jax import lax
from jax.experimental import pallas as pl
from jax.experimental.pallas import tpu as pltpu

D_MODEL = 2048
BATCH = 4
SEQ = 2048
DEPTH = 4
DEC_BATCH = 32
DEC_SEQ = 8
N_META = 16
D_ATTN = 1024
D_REC = 1024
HEAD_DIM = 64
N_HEADS = 16
N_KV = 4
GROUP = 4
KV_DIM = 256
WINDOW = 128
N_BUCKETS = 32
MAX_DISTANCE = 128
REC_HEAD_DIM = 128
N_REC_HEADS = 8
EPS = 1e-6
D_IN = 2 * D_ATTN + 2 * KV_DIM + 4 * D_REC

OFF_QA = 0
OFF_KA = D_ATTN
OFF_VA = OFF_KA + KV_DIM
OFF_GA = OFF_VA + KV_DIM
OFF_QR = OFF_GA + D_ATTN
OFF_FR = OFF_QR + D_REC
OFF_IR = OFF_FR + D_REC
OFF_GR = OFF_IR + D_REC
ATT_COLS = OFF_QR

TILE = 256
QBLK = WINDOW
CHUNK = 64
HALF = CHUNK // 2
ROW_TILES = SEQ // TILE
PAD = TILE - N_META
PROJ_COLS = 512
REC_GROUP = PROJ_COLS // REC_HEAD_DIM
NEG = -1e30
N_DEC = DEC_BATCH * DEC_SEQ
V7X_VMEM_BYTES = 64 * 1024 * 1024
VMEM_LIMIT = V7X_VMEM_BYTES - 2 * 1024 * 1024

F32 = jnp.float32
BF16 = jnp.bfloat16


def _sigmoid_pair(x):
    e = jnp.exp(-jnp.abs(x))
    inv = 1.0 / (1.0 + e)
    pos = x >= 0
    return jnp.where(pos, inv, e * inv), jnp.where(pos, e * inv, inv)


def _silu(x):
    return x * (0.5 * jnp.tanh(0.5 * x) + 0.5)


def _cumsum_rows(x):
    n = x.shape[0]
    row = lax.broadcasted_iota(jnp.int32, x.shape, 0)
    k = 1
    while k < n:
        x = x + jnp.where(row >= k, pltpu.roll(x, k, axis=0), 0.0)
        k *= 2
    return x


def _rmsnorm_rows(x, w):
    ms = jnp.mean(x * x, axis=-1, keepdims=True)
    return (x * lax.rsqrt(ms + EPS)) * w


def _dot_nt(a, b):
    return lax.dot_general(a, b, (((1,), (1,)), ((), ())), preferred_element_type=F32)


def _dot_tn(a, b):
    return lax.dot_general(a, b, (((0,), (0,)), ((), ())), preferred_element_type=F32)


def _dot(a, b):
    return jnp.dot(a, b, preferred_element_type=F32)


def _gate_features(fr, lb, oml):
    sig, sneg = _sigmoid_pair(fr)
    logf = jnp.log(lb + oml * sig)
    return logf, oml * sneg


def _attention_unit(layer, qb, kh, za_ref, kv_ref, y_ref, bias_ref, sink_ref, c_tile):
    r0 = qb * QBLK
    kpos = c_tile + r0 - QBLK + lax.broadcasted_iota(jnp.int32, (1, 2 * QBLK), 1)
    kvalid = kpos >= 0
    kk = kv_ref[r0:r0 + 2 * QBLK, kh * HEAD_DIM:(kh + 1) * HEAD_DIM]
    q4 = jnp.concatenate(
        [za_ref[r0:r0 + QBLK, (kh * GROUP + g) * HEAD_DIM:(kh * GROUP + g + 1) * HEAD_DIM]
         for g in range(GROUP)], axis=0)
    s = _dot_nt((q4 * (HEAD_DIM ** -0.5)).astype(BF16), kk)
    yield
    s = s + bias_ref[kh * GROUP * QBLK:(kh + 1) * GROUP * QBLK, :]
    s = jnp.where(kvalid, s, NEG)
    es, dens = [], []
    for g in range(GROUP):
        sink = sink_ref[layer, kh * GROUP + g]
        sg = s[g * QBLK:(g + 1) * QBLK]
        m = jnp.maximum(jnp.max(sg, axis=-1, keepdims=True), sink)
        e = jnp.exp(sg - m)
        dens.append(jnp.sum(e, axis=-1, keepdims=True) + jnp.exp(sink - m))
        es.append(e.astype(BF16))
    vv = kv_ref[r0:r0 + 2 * QBLK, KV_DIM + kh * HEAD_DIM:KV_DIM + (kh + 1) * HEAD_DIM]
    o4 = _dot(jnp.concatenate(es, axis=0), vv)
    yield
    outs = []
    for g in range(GROUP):
        h = kh * GROUP + g
        og = o4[g * QBLK:(g + 1) * QBLK] / dens[g]
        ga = za_ref[r0:r0 + QBLK, OFF_GA + h * HEAD_DIM:OFF_GA + (h + 1) * HEAD_DIM]
        outs.append(og * _silu(ga))
    y_ref[r0:r0 + QBLK, kh * GROUP * HEAD_DIM:(kh + 1) * GROUP * HEAD_DIM] = (
        jnp.concatenate(outs, axis=-1).astype(BF16))


def _hgrn2_unit(ci, hh, zr_ref, st_ref, y_ref, lb_ref, oml_ref, recw_ref, c_tile):
    row = lax.broadcasted_iota(jnp.int32, (CHUNK, CHUNK), 0)
    col = lax.broadcasted_iota(jnp.int32, (CHUNK, CHUNK), 1)
    causal = col <= row
    r0 = ci * CHUNK
    rows = slice(r0, r0 + CHUNK)
    rpos = c_tile + r0 + lax.broadcasted_iota(jnp.int32, (CHUNK, 1), 0)
    rvalid = rpos >= 0
    cs = hh * REC_HEAD_DIM
    zc = (hh % REC_GROUP) * REC_HEAD_DIM
    lb = lb_ref[:, cs:cs + REC_HEAD_DIM]
    oml = oml_ref[:, cs:cs + REC_HEAD_DIM]
    logf, kr = _gate_features(zr_ref[rows, PROJ_COLS + zc:PROJ_COLS + zc + REC_HEAD_DIM], lb, oml)
    q = _silu(zr_ref[rows, zc:zc + REC_HEAD_DIM])
    v = jnp.where(rvalid, zr_ref[rows, 2 * PROJ_COLS + zc:2 * PROJ_COLS + zc + REC_HEAD_DIM], 0.0)
    vb = v.astype(BF16)
    a = _cumsum_rows(logf)
    a_mid = a[HALF - 1:HALF, :]
    a_end = a[CHUNK - 1:CHUNK, :]
    qh = (q * jnp.exp(a - a_mid)).astype(BF16)
    kh_ = (kr * jnp.exp(a_mid - a)).astype(BF16)
    scores = _dot_nt(qh, kh_)
    qt = (q * jnp.exp(a)).astype(BF16)
    kt = (kr * jnp.exp(a_end - a)).astype(BF16)
    st = st_ref[hh]
    o_state = _dot_nt(qt, st.astype(BF16))
    yield
    att = jnp.where(causal, scores, 0.0)
    o = _dot(att.astype(BF16), vb) + o_state
    st_ref[hh] = st * jnp.exp(a_end) + _dot_tn(vb, kt)
    yield
    on = o * lax.rsqrt(jnp.mean(o * o, axis=-1, keepdims=True) + EPS)
    gr = zr_ref[rows, 3 * PROJ_COLS + zc:3 * PROJ_COLS + zc + REC_HEAD_DIM]
    yr = (on * recw_ref[:, cs:cs + REC_HEAD_DIM]) * _silu(gr)
    y_ref[rows, D_ATTN + cs:D_ATTN + cs + REC_HEAD_DIM] = yr.astype(BF16)


def _emit_staggered(units, fillers=(), lag=1):
    n = len(units)
    rounds = n + 2 * lag
    at_round = [[] for _ in range(rounds)]
    for k, f in enumerate(fillers):
        at_round[k * rounds // len(fillers)].append(f)
    for i in range(rounds):
        if i < n:
            next(units[i])
        for f in at_round[i]:
            f()
        if 0 <= i - lag < n:
            next(units[i - lag])
        if 0 <= i - 2 * lag < n:
            next(units[i - 2 * lag], None)


def _main_tile(s):
    t = jnp.maximum(s - 1, 0)
    return lax.div(t, ROW_TILES), lax.rem(t, ROW_TILES)


def _prompt_layer_kernel(layer, final, sink_ref, meta_ref, hin_ref, win_ref, wout_ref, normw_ref, lb_ref, oml_ref,
                         recw_ref, bias_ref, finw_ref, *refs):
    if final:
        hout_ref, pk_ref, pv_ref, ps_ref = refs[:4]
        refs = refs[4:]
    else:
        meta_out_ref, hout_ref, pk_ref, pv_ref, ps_ref = refs[:5]
        refs = refs[5:]
    xn_ref, za_ref, zr_ref, kv_ref, st_ref, y_ref, kv_meta_ref, st_meta_ref = refs
    s = pl.program_id(0)
    is_meta = s == 0
    _, j = _main_tile(s)

    @pl.when(is_meta)
    def _():
        kv_ref[0:QBLK, :] = jnp.zeros((QBLK, 2 * KV_DIM), BF16)
        st_ref[...] = jnp.zeros(st_ref.shape, F32)

    @pl.when(jnp.logical_and(jnp.logical_not(is_meta), j == 0))
    def _():
        kv_ref[0:QBLK, :] = kv_meta_ref[...]
        st_ref[...] = st_meta_ref[...]

    def tile_input(cols=slice(None)):
        meta = meta_ref[:, cols]
        meta_rows = jnp.concatenate([jnp.zeros((PAD, meta.shape[1]), F32), meta], axis=0)
        return jnp.where(is_meta, meta_rows, hin_ref[0, :, cols])

    c_tile = jnp.where(is_meta, -PAD, j * TILE + N_META)
    xn_ref[...] = _rmsnorm_rows(tile_input(), normw_ref[...]).astype(BF16)

    def project(dst_ref, dst_col, src_col):
        dst_ref[:, dst_col:dst_col + PROJ_COLS] = _dot(xn_ref[...], win_ref[:, src_col:src_col + PROJ_COLS])

    def project_rec(dst_ref, group):
        return [functools.partial(project, dst_ref, part * PROJ_COLS, off + group * PROJ_COLS)
                for part, off in enumerate((OFF_QR, OFF_FR, OFF_IR, OFF_GR))]

    def out_proj(k0, k1, c0, first_half):
        cols = slice(c0, c0 + PROJ_COLS)
        base = tile_input(cols) if first_half else hout_ref[0, :, cols]
        hout_ref[0, :, cols] = base + _dot(y_ref[:, k0:k1], wout_ref[k0:k1, cols])

    for c0 in range(0, ATT_COLS, PROJ_COLS):
        project(za_ref, c0, c0)
    kv_ref[QBLK:QBLK + TILE, :] = za_ref[:, OFF_KA:OFF_KA + 2 * KV_DIM].astype(BF16)
    pk_ref[0] = za_ref[TILE - WINDOW:TILE, OFF_KA:OFF_KA + KV_DIM]
    pv_ref[0] = za_ref[TILE - WINDOW:TILE, OFF_VA:OFF_VA + KV_DIM]

    att_units = [_attention_unit(layer, qb, kh, za_ref, kv_ref, y_ref, bias_ref, sink_ref, c_tile)
                 for qb in range(TILE // QBLK) for kh in range(N_KV)]
    _emit_staggered(att_units, project_rec(zr_ref, 0))
    kv_ref[0:QBLK, :] = kv_ref[TILE:TILE + QBLK, :]

    n_groups = N_REC_HEADS // REC_GROUP
    z_bufs = [zr_ref, za_ref]
    for grp in range(n_groups):
        zsrc = z_bufs[grp % 2]
        units = [_hgrn2_unit(ci, grp * REC_GROUP + hl, zsrc, st_ref, y_ref, lb_ref, oml_ref, recw_ref, c_tile)
                 for ci in range(TILE // CHUNK) for hl in range(REC_GROUP)]
        if grp + 1 < n_groups:
            fillers = project_rec(z_bufs[(grp + 1) % 2], grp + 1)
        else:
            fillers = [functools.partial(out_proj, 0, D_ATTN, c0, True) for c0 in range(0, D_MODEL, PROJ_COLS)]
        _emit_staggered(units, fillers)
    for c0 in range(0, D_MODEL, PROJ_COLS):
        out_proj(D_ATTN, D_MODEL, c0, False)
    ps_ref[0] = st_ref[...]

    @pl.when(is_meta)
    def _():
        kv_meta_ref[...] = kv_ref[0:QBLK, :]
        st_meta_ref[...] = st_ref[...]
        if not final:
            meta_out_ref[...] = hout_ref[0, PAD:TILE, :]

    if final:
        @pl.when(jnp.logical_not(is_meta))
        def _():
            hout_ref[0] = _rmsnorm_rows(hout_ref[0], finw_ref[...])


def _layer_block(shape, layer):
    nd = len(shape)
    return pl.BlockSpec((None,) + tuple(shape), lambda s: (layer,) + (0,) * nd, pipeline_mode=pl.Buffered(1))


def _const_block(shape):
    nd = len(shape)
    return pl.BlockSpec(tuple(shape), lambda s: (0,) * nd, pipeline_mode=pl.Buffered(1))


def _prompt_layer(layer, h_meta, h_main, w_in, w_out, norm_w, lb, oml, rec_w, sinks, bias, fin_w):
    final = layer == DEPTH - 1
    main = pl.BlockSpec((1, TILE, D_MODEL), lambda s: _main_tile(s) + (0,))
    meta = pl.BlockSpec((N_META, D_MODEL), lambda s: (0, 0))
    kv_spec = pl.BlockSpec((1, WINDOW, KV_DIM), lambda s: (_main_tile(s)[0], 0, 0))
    st_spec = pl.BlockSpec((1, N_REC_HEADS, REC_HEAD_DIM, REC_HEAD_DIM), lambda s: (_main_tile(s)[0], 0, 0, 0))
    main_shape = jax.ShapeDtypeStruct((BATCH, SEQ, D_MODEL), F32)
    meta_shape = jax.ShapeDtypeStruct((N_META, D_MODEL), F32)
    return pl.pallas_call(
        functools.partial(_prompt_layer_kernel, layer, final),
        grid=(1 + BATCH * ROW_TILES,),
        in_specs=[
            pl.BlockSpec(memory_space=pltpu.SMEM),
            _const_block((N_META, D_MODEL)),
            main,
            _const_block((D_MODEL, D_IN)),
            _const_block((D_MODEL, D_MODEL)),
            _layer_block((1, D_MODEL), layer),
            _layer_block((1, D_REC), layer),
            _layer_block((1, D_REC), layer),
            _layer_block((1, D_REC), layer),
            _const_block((N_HEADS * QBLK, 2 * QBLK)),
            _const_block((1, D_MODEL)),
        ],
        out_specs=([] if final else [meta]) + [main, kv_spec, kv_spec, st_spec],
        out_shape=([] if final else [meta_shape]) + [
            main_shape,
            jax.ShapeDtypeStruct((BATCH, WINDOW, KV_DIM), F32),
            jax.ShapeDtypeStruct((BATCH, WINDOW, KV_DIM), F32),
            jax.ShapeDtypeStruct((BATCH, N_REC_HEADS, REC_HEAD_DIM, REC_HEAD_DIM), F32),
        ],
        scratch_shapes=[
            pltpu.VMEM((TILE, D_MODEL), BF16),
            pltpu.VMEM((TILE, ATT_COLS), F32),
            pltpu.VMEM((TILE, 4 * PROJ_COLS), F32),
            pltpu.VMEM((QBLK + TILE, 2 * KV_DIM), BF16),
            pltpu.VMEM((N_REC_HEADS, REC_HEAD_DIM, REC_HEAD_DIM), F32),
            pltpu.VMEM((TILE, D_MODEL), BF16),
            pltpu.VMEM((QBLK, 2 * KV_DIM), BF16),
            pltpu.VMEM((N_REC_HEADS, REC_HEAD_DIM, REC_HEAD_DIM), F32),
        ],
        compiler_params=pltpu.CompilerParams(
            dimension_semantics=("arbitrary",), vmem_limit_bytes=VMEM_LIMIT),
        name="prompt_layer",
    )(sinks, h_meta, h_main, w_in, w_out, norm_w, lb, oml, rec_w, bias, fin_w)


def _dec_proj_kernel(hs_ref, normw_ref, win_ref, z_ref, wb_ref):
    xn = _rmsnorm_rows(hs_ref[...], normw_ref[...]).astype(BF16)
    wb = win_ref[...].astype(BF16)
    wb_ref[...] = wb
    z_ref[...] = _dot(xn, wb)


def _dec_proj(layer, hs, norm_w, w_in):
    return pl.pallas_call(
        _dec_proj_kernel,
        grid=(D_IN // PROJ_COLS,),
        in_specs=[
            pl.BlockSpec((N_DEC, D_MODEL), lambda n: (0, 0)),
            pl.BlockSpec((None, 1, D_MODEL), lambda n: (layer, 0, 0)),
            pl.BlockSpec((None, D_MODEL, PROJ_COLS), lambda n: (layer, 0, n)),
        ],
        out_specs=[pl.BlockSpec((N_DEC, PROJ_COLS), lambda n: (0, n)),
                   pl.BlockSpec((D_MODEL, PROJ_COLS), lambda n: (0, n))],
        out_shape=[jax.ShapeDtypeStruct((N_DEC, D_IN), F32),
                   jax.ShapeDtypeStruct((D_MODEL, D_IN), BF16)],
        compiler_params=pltpu.CompilerParams(dimension_semantics=("arbitrary",)),
        name="dec_proj",
    )(hs, norm_w, w_in)


def _dec_out_kernel(hs_ref, y_ref, wout_ref, o_ref, wb_ref):
    wb = wout_ref[...].astype(BF16)
    wb_ref[...] = wb
    o_ref[...] = hs_ref[...] + _dot(y_ref[...], wb)


def _dec_out(layer, hs, y, w_out):
    return pl.pallas_call(
        _dec_out_kernel,
        grid=(D_MODEL // PROJ_COLS,),
        in_specs=[
            pl.BlockSpec((N_DEC, PROJ_COLS), lambda n: (0, n)),
            pl.BlockSpec((N_DEC, D_MODEL), lambda n: (0, 0)),
            pl.BlockSpec((None, D_MODEL, PROJ_COLS), lambda n: (layer, 0, n)),
        ],
        out_specs=[pl.BlockSpec((N_DEC, PROJ_COLS), lambda n: (0, n)),
                   pl.BlockSpec((D_MODEL, PROJ_COLS), lambda n: (0, n))],
        out_shape=[jax.ShapeDtypeStruct((N_DEC, D_MODEL), F32),
                   jax.ShapeDtypeStruct((D_MODEL, D_MODEL), BF16)],
        compiler_params=pltpu.CompilerParams(dimension_semantics=("arbitrary",)),
        name="dec_out",
    )(hs, y, w_out)


DEC_SEQS_PER_STEP = 4
DEC_LAG = 8


def _dec_attention_unit(layer, q, kh, sink_ref, z_ref, ck_ref, cv_ref, biasc_ref, biasn_ref, y_ref):
    t = DEC_SEQ
    tok = slice(q * t, (q + 1) * t)
    hs_ = slice(kh * HEAD_DIM, (kh + 1) * HEAD_DIM)
    kc = ck_ref[q, :, hs_].astype(BF16)
    kn = z_ref[tok, OFF_KA + kh * HEAD_DIM:OFF_KA + (kh + 1) * HEAD_DIM]
    q4 = jnp.concatenate(
        [z_ref[tok, (kh * GROUP + g) * HEAD_DIM:(kh * GROUP + g + 1) * HEAD_DIM] for g in range(GROUP)],
        axis=0) * (HEAD_DIM ** -0.5)
    rows = slice(kh * GROUP * t, (kh + 1) * GROUP * t)
    sc = _dot_nt(q4.astype(BF16), kc)
    sn = _dot_nt(q4, kn)
    yield
    sc = sc + biasc_ref[rows, :]
    sn = sn + biasn_ref[rows, :]
    ecs, ens, dens = [], [], []
    for g in range(GROUP):
        sink = sink_ref[layer, kh * GROUP + g]
        scg = sc[g * t:(g + 1) * t]
        sng = sn[g * t:(g + 1) * t]
        m = jnp.maximum(jnp.maximum(jnp.max(scg, axis=-1, keepdims=True),
                                    jnp.max(sng, axis=-1, keepdims=True)), sink)
        ec = jnp.exp(scg - m)
        en = jnp.exp(sng - m)
        dens.append(jnp.sum(ec, axis=-1, keepdims=True) + jnp.sum(en, axis=-1, keepdims=True)
                    + jnp.exp(sink - m))
        ecs.append(ec)
        ens.append(en)
    vc = cv_ref[q, :, hs_]
    vn = z_ref[tok, OFF_VA + kh * HEAD_DIM:OFF_VA + (kh + 1) * HEAD_DIM]
    o4 = (_dot(jnp.concatenate(ecs, axis=0).astype(BF16), vc.astype(BF16))
          + _dot(jnp.concatenate(ens, axis=0), vn))
    yield
    outs = []
    for g in range(GROUP):
        h = kh * GROUP + g
        og = o4[g * t:(g + 1) * t] / dens[g]
        ga = z_ref[tok, OFF_GA + h * HEAD_DIM:OFF_GA + (h + 1) * HEAD_DIM]
        outs.append(og * _silu(ga))
    y_ref[tok, kh * GROUP * HEAD_DIM:(kh + 1) * GROUP * HEAD_DIM] = jnp.concatenate(outs, axis=-1).astype(BF16)


def _dec_hgrn2_unit(q, hh, z_ref, s0_ref, lb_ref, oml_ref, recw_ref, y_ref, ns_ref):
    t = DEC_SEQ
    tok = slice(q * t, (q + 1) * t)
    row = lax.broadcasted_iota(jnp.int32, (t, t), 0)
    col = lax.broadcasted_iota(jnp.int32, (t, t), 1)
    causal = col <= row
    cs = hh * REC_HEAD_DIM
    lb = lb_ref[:, cs:cs + REC_HEAD_DIM]
    oml = oml_ref[:, cs:cs + REC_HEAD_DIM]
    logf, kr = _gate_features(z_ref[tok, OFF_FR + cs:OFF_FR + cs + REC_HEAD_DIM], lb, oml)
    qv = _silu(z_ref[tok, OFF_QR + cs:OFF_QR + cs + REC_HEAD_DIM])
    v = z_ref[tok, OFF_IR + cs:OFF_IR + cs + REC_HEAD_DIM]
    a = _cumsum_rows(logf)
    a_mid = a[t // 2 - 1:t // 2, :]
    a_end = a[t - 1:t, :]
    scores = _dot_nt(qv * jnp.exp(a - a_mid), kr * jnp.exp(a_mid - a))
    qt = qv * jnp.exp(a)
    kt = kr * jnp.exp(a_end - a)
    s0 = s0_ref[q, hh]
    o_state = _dot(qt, s0)
    yield
    att = jnp.where(causal, scores, 0.0)
    o = _dot(att, v) + o_state
    ktd = jnp.concatenate([kt, jnp.exp(a_end), jnp.zeros((t - 1, REC_HEAD_DIM), F32)], axis=0).T
    ns_ref[q, hh] = s0 * ktd[:, t:t + 1] + _dot(ktd[:, 0:t], v)
    yield
    on = o * lax.rsqrt(jnp.mean(o * o, axis=-1, keepdims=True) + EPS)
    gr = z_ref[tok, OFF_GR + cs:OFF_GR + cs + REC_HEAD_DIM]
    yr = (on * recw_ref[:, cs:cs + REC_HEAD_DIM]) * _silu(gr)
    y_ref[tok, D_ATTN + cs:D_ATTN + cs + REC_HEAD_DIM] = yr.astype(BF16)


def _dec_mix_kernel(layer, sink_ref, z_ref, ck_ref, cv_ref, s0_ref, lb_ref, oml_ref, recw_ref, biasc_ref,
                    biasn_ref, y_ref, nk_ref, nv_ref, ns_ref):
    t = DEC_SEQ
    for q in range(DEC_SEQS_PER_STEP):
        tok = slice(q * t, (q + 1) * t)
        nk_ref[q, 0:WINDOW - t, :] = ck_ref[q, t:WINDOW, :]
        nk_ref[q, WINDOW - t:WINDOW, :] = z_ref[tok, OFF_KA:OFF_KA + KV_DIM]
        nv_ref[q, 0:WINDOW - t, :] = cv_ref[q, t:WINDOW, :]
        nv_ref[q, WINDOW - t:WINDOW, :] = z_ref[tok, OFF_VA:OFF_VA + KV_DIM]
    units = []
    for kh in range(N_KV):
        units += [_dec_attention_unit(layer, q, kh, sink_ref, z_ref, ck_ref, cv_ref, biasc_ref, biasn_ref, y_ref)
                  for q in range(DEC_SEQS_PER_STEP)]
    for hh in range(N_REC_HEADS):
        units += [_dec_hgrn2_unit(q, hh, z_ref, s0_ref, lb_ref, oml_ref, recw_ref, y_ref, ns_ref)
                  for q in range(DEC_SEQS_PER_STEP)]
    _emit_staggered(units, lag=DEC_LAG)


def _dec_mix(layer, z, ck, cv, s0, lb, oml, rec_w, sinks, bias_c, bias_n):
    t = DEC_SEQ
    nq = DEC_SEQS_PER_STEP
    const2 = lambda s: (0, 0)
    per_layer = lambda s: (layer, 0, 0)
    kv_in = pl.BlockSpec((None, nq, WINDOW, KV_DIM), lambda s: (layer, s, 0, 0))
    st_in = pl.BlockSpec((None, nq, N_REC_HEADS, REC_HEAD_DIM, REC_HEAD_DIM), lambda s: (layer, s, 0, 0, 0))
    return pl.pallas_call(
        functools.partial(_dec_mix_kernel, layer),
        grid=(DEC_BATCH // nq,),
        in_specs=[
            pl.BlockSpec(memory_space=pltpu.SMEM),
            pl.BlockSpec((nq * t, D_IN), lambda s: (s, 0)),
            kv_in,
            kv_in,
            st_in,
            pl.BlockSpec((None, 1, D_REC), per_layer),
            pl.BlockSpec((None, 1, D_REC), per_layer),
            pl.BlockSpec((None, 1, D_REC), per_layer),
            pl.BlockSpec((N_HEADS * t, WINDOW), const2),
            pl.BlockSpec((N_HEADS * t, t), const2),
        ],
        out_specs=[
            pl.BlockSpec((nq * t, D_MODEL), lambda s: (s, 0)),
            pl.BlockSpec((nq, WINDOW, KV_DIM), lambda s: (s, 0, 0)),
            pl.BlockSpec((nq, WINDOW, KV_DIM), lambda s: (s, 0, 0)),
            pl.BlockSpec((nq, N_REC_HEADS, REC_HEAD_DIM, REC_HEAD_DIM), lambda s: (s, 0, 0, 0)),
        ],
        out_shape=[
            jax.ShapeDtypeStruct((N_DEC, D_MODEL), BF16),
            jax.ShapeDtypeStruct((DEC_BATCH, WINDOW, KV_DIM), F32),
            jax.ShapeDtypeStruct((DEC_BATCH, WINDOW, KV_DIM), F32),
            jax.ShapeDtypeStruct((DEC_BATCH, N_REC_HEADS, REC_HEAD_DIM, REC_HEAD_DIM), F32),
        ],
        compiler_params=pltpu.CompilerParams(dimension_semantics=("arbitrary",)),
        name="dec_mix",
    )(sinks, z, ck, cv, s0, lb, oml, rec_w, bias_c, bias_n)


def _final_norm_kernel(x_ref, w_ref, o_ref):
    o_ref[...] = _rmsnorm_rows(x_ref[...], w_ref[...])


def _final_norm_dec(hs, w):
    return pl.pallas_call(
        _final_norm_kernel,
        out_shape=jax.ShapeDtypeStruct((N_DEC, D_MODEL), F32),
        name="final_norm_dec",
    )(hs, w)


def _t5_bucket(dist):
    max_exact = N_BUCKETS // 2
    d = jnp.maximum(dist, 0)
    df = jnp.maximum(d, 1).astype(F32)
    large = max_exact + (jnp.log(df / max_exact) / math.log(MAX_DISTANCE / max_exact)
                         * (N_BUCKETS - max_exact)).astype(jnp.int32)
    large = jnp.minimum(large, N_BUCKETS - 1)
    return jnp.where(d < max_exact, d, large)


def _bias_tables(rel_bias_table):
    bias_d = rel_bias_table[_t5_bucket(jnp.arange(WINDOW))].T.astype(F32)
    period = 3 * WINDOW + 1
    u = jnp.concatenate([jnp.full((N_HEADS, 1), NEG, F32), bias_d[:, ::-1],
                         jnp.full((N_HEADS, period - 1 - WINDOW), NEG, F32)], axis=1)
    rows = jnp.tile(u, (1, WINDOW))[:, :WINDOW * (period - 1)].reshape(N_HEADS, WINDOW, period - 1)
    prompt = rows[:, :, :2 * WINDOW].reshape(N_HEADS * WINDOW, 2 * WINDOW)
    dec = rows[:, :DEC_SEQ, :WINDOW + DEC_SEQ].reshape(N_HEADS * DEC_SEQ, WINDOW + DEC_SEQ)
    return prompt, dec[:, :WINDOW], dec[:, WINDOW:]


def kernel(x_prompt, x_sample, cache_k, cache_v, state_h, meta_tokens, w_in, w_out, norm_w, final_norm_w,
           attn_sinks, rel_bias_table, hgrn_lb_logits, hgrn_norm_w):
    pl_ = jax.nn.softmax(hgrn_lb_logits.astype(F32), axis=0)
    lb = (jnp.cumsum(pl_, axis=0) - pl_[0:1]).reshape(DEPTH, 1, D_REC)
    oml = 1.0 - lb
    bias_p, bias_dc, bias_dn = _bias_tables(rel_bias_table)
    nw = norm_w.astype(F32).reshape(DEPTH, 1, D_MODEL)
    rw = hgrn_norm_w.astype(F32).reshape(DEPTH, 1, D_REC)
    fw = final_norm_w.astype(F32).reshape(1, D_MODEL)
    sinks = attn_sinks.astype(F32)

    h_meta = meta_tokens.astype(F32)
    h_main = x_prompt
    hs = x_sample.reshape(N_DEC, D_MODEL)
    ck = cache_k.reshape(DEPTH, DEC_BATCH, WINDOW, KV_DIM)
    cv = cache_v.reshape(DEPTH, DEC_BATCH, WINDOW, KV_DIM)

    pk, pv, ps, sk, sv, ss = [], [], [], [], [], []
    for l in range(DEPTH):
        z, w_in_b = _dec_proj(l, hs, nw, w_in)
        y, nk, nv, ns = _dec_mix(l, z, ck, cv, state_h, lb, oml, rw, sinks, bias_dc, bias_dn)
        hs, w_out_b = _dec_out(l, hs, y, w_out)
        sk.append(nk)
        sv.append(nv)
        ss.append(ns)

        outs = _prompt_layer(l, h_meta, h_main, w_in_b, w_out_b, nw, lb, oml, rw, sinks, bias_p, fw)
        if l < DEPTH - 1:
            h_meta, h_main, k_l, v_l, s_l = outs
        else:
            h_main, k_l, v_l, s_l = outs
        pk.append(k_l)
        pv.append(v_l)
        ps.append(s_l)

    y_prompt = h_main
    y_sample = _final_norm_dec(hs, fw).reshape(DEC_BATCH, DEC_SEQ, D_MODEL)
    kv_shape = (DEPTH, -1, WINDOW, N_KV, HEAD_DIM)
    return (y_prompt, y_sample,
            jnp.stack(pk).reshape(kv_shape), jnp.stack(pv).reshape(kv_shape),
            jnp.swapaxes(jnp.stack(ps), -1, -2),
            jnp.stack(sk).reshape(kv_shape), jnp.stack(sv).reshape(kv_shape), jnp.stack(ss))
```

```python
import functools
import math

import jax
import jax.numpy as jnp
from jax import lax
from jax.experimental import pallas as pl
from jax.experimental.pallas import tpu as pltpu

D_MODEL = 2048
BATCH = 4
SEQ = 2048
DEPTH = 4
DEC_BATCH = 32
DEC_SEQ = 8
N_META = 16
D_ATTN = 1024
D_REC = 1024
HEAD_DIM = 64
N_HEADS = 16
N_KV = 4
GROUP = 4
KV_DIM = 256
WINDOW = 128
N_BUCKETS = 32
MAX_DISTANCE = 128
REC_HEAD_DIM = 128
N_REC_HEADS = 8
EPS = 1e-6
D_IN = 2 * D_ATTN + 2 * KV_DIM + 4 * D_REC

OFF_QA = 0
OFF_KA = D_ATTN
OFF_VA = OFF_KA + KV_DIM
OFF_GA = OFF_VA + KV_DIM
OFF_QR = OFF_GA + D_ATTN
OFF_FR = OFF_QR + D_REC
OFF_IR = OFF_FR + D_REC
OFF_GR = OFF_IR + D_REC
ATT_COLS = OFF_QR

TILE = 256
QBLK = WINDOW
CHUNK = 64
HALF = CHUNK // 2
ROW_TILES = SEQ // TILE
PAD = TILE - N_META
PROJ_COLS = 512
REC_GROUP = PROJ_COLS // REC_HEAD_DIM
NEG = -1e30
MAX_FACTOR_EXPONENT = 80.0
N_DEC = DEC_BATCH * DEC_SEQ
V7X_VMEM_BYTES = 64 * 1024 * 1024
VMEM_LIMIT = V7X_VMEM_BYTES - 1024 * 1024

F32 = jnp.float32
BF16 = jnp.bfloat16


def _sigmoid_pair(x):
    e = jnp.exp(-jnp.abs(x))
    inv = 1.0 / (1.0 + e)
    pos = x >= 0
    return jnp.where(pos, inv, e * inv), jnp.where(pos, e * inv, inv)


def _silu(x):
    return x * (0.5 * jnp.tanh(0.5 * x) + 0.5)


def _cumsum_rows(x):
    n = x.shape[0]
    row = lax.broadcasted_iota(jnp.int32, x.shape, 0)
    k = 1
    while k < n:
        x = x + jnp.where(row >= k, pltpu.roll(x, k, axis=0), 0.0)
        k *= 2
    return x


def _rmsnorm_rows(x, w):
    ms = jnp.mean(x * x, axis=-1, keepdims=True)
    return (x * lax.rsqrt(ms + EPS)) * w


def _dot_nt(a, b):
    return lax.dot_general(a, b, (((1,), (1,)), ((), ())), preferred_element_type=F32)


def _dot_tn(a, b):
    return lax.dot_general(a, b, (((0,), (0,)), ((), ())), preferred_element_type=F32)


def _dot(a, b):
    return jnp.dot(a, b, preferred_element_type=F32)


def _gate_features(fr, lb, oml):
    sig, sneg = _sigmoid_pair(fr)
    logf = jnp.log(lb + oml * sig)
    return logf, oml * sneg


def _attention_unit(layer, qb, kh, za_ref, kv_ref, y_ref, bias_ref, sink_ref, c_tile):
    r0 = qb * QBLK
    kpos = c_tile + r0 - QBLK + lax.broadcasted_iota(jnp.int32, (1, 2 * QBLK), 1)
    kvalid = kpos >= 0
    kk = kv_ref[r0:r0 + 2 * QBLK, kh * HEAD_DIM:(kh + 1) * HEAD_DIM]
    q4 = jnp.concatenate(
        [za_ref[r0:r0 + QBLK, (kh * GROUP + g) * HEAD_DIM:(kh * GROUP + g + 1) * HEAD_DIM]
         for g in range(GROUP)], axis=0)
    s = _dot_nt((q4 * (HEAD_DIM ** -0.5)).astype(BF16), kk)
    yield
    s = s + bias_ref[kh * GROUP * QBLK:(kh + 1) * GROUP * QBLK, :].astype(F32)
    s = jnp.where(kvalid, s, NEG)
    es, dens = [], []
    for g in range(GROUP):
        sink = sink_ref[layer, kh * GROUP + g]
        sg = s[g * QBLK:(g + 1) * QBLK]
        m = jnp.maximum(jnp.max(sg, axis=-1, keepdims=True), sink)
        e = jnp.exp(sg - m)
        dens.append(jnp.sum(e, axis=-1, keepdims=True) + jnp.exp(sink - m))
        es.append(e.astype(BF16))
    vv = kv_ref[r0:r0 + 2 * QBLK, KV_DIM + kh * HEAD_DIM:KV_DIM + (kh + 1) * HEAD_DIM]
    o4 = _dot(jnp.concatenate(es, axis=0), vv)
    yield
    outs = []
    for g in range(GROUP):
        h = kh * GROUP + g
        og = o4[g * QBLK:(g + 1) * QBLK] / dens[g]
        ga = za_ref[r0:r0 + QBLK, OFF_GA + h * HEAD_DIM:OFF_GA + (h + 1) * HEAD_DIM]
        outs.append(og * _silu(ga))
    y_ref[r0:r0 + QBLK, kh * GROUP * HEAD_DIM:(kh + 1) * GROUP * HEAD_DIM] = (
        jnp.concatenate(outs, axis=-1).astype(BF16))


def _decay_scores_direct(q, kr, a):
    n = q.shape[0]
    row = lax.broadcasted_iota(jnp.int32, (n, n), 0)
    col = lax.broadcasted_iota(jnp.int32, (n, n), 1)

    def diagonal(d, acc):
        ks = pltpu.roll(kr, d, axis=0)
        a_s = pltpu.roll(a, d, axis=0)
        p = q * ks * jnp.exp(jnp.minimum(a - a_s, 0.0))
        return jnp.where(col == row - d, jnp.sum(p, axis=-1, keepdims=True), acc)

    acc = jnp.zeros((n, n), F32)
    if n <= DEC_SEQ:
        for d in range(n):
            acc = diagonal(d, acc)
        return acc
    return lax.fori_loop(0, n, diagonal, acc)


def _hgrn2_unit(ci, hh, zr_ref, st_ref, y_ref, lb_ref, oml_ref, recw_ref, c_tile, spans=None, heads_in_buf=None):
    heads_in_buf = REC_GROUP if heads_in_buf is None else heads_in_buf
    part = heads_in_buf * REC_HEAD_DIM
    row = lax.broadcasted_iota(jnp.int32, (CHUNK, CHUNK), 0)
    col = lax.broadcasted_iota(jnp.int32, (CHUNK, CHUNK), 1)
    causal = col <= row
    r0 = ci * CHUNK if isinstance(ci, int) else pl.multiple_of(ci * CHUNK, CHUNK)
    rows = pl.ds(r0, CHUNK)
    rpos = c_tile + r0 + lax.broadcasted_iota(jnp.int32, (CHUNK, 1), 0)
    rvalid = rpos >= 0
    cs = hh * REC_HEAD_DIM
    zc = (hh % heads_in_buf) * REC_HEAD_DIM
    lb = lb_ref[:, cs:cs + REC_HEAD_DIM]
    oml = oml_ref[:, cs:cs + REC_HEAD_DIM]
    logf, kr = _gate_features(zr_ref[rows, part + zc:part + zc + REC_HEAD_DIM], lb, oml)
    q = _silu(zr_ref[rows, zc:zc + REC_HEAD_DIM])
    v = jnp.where(rvalid, zr_ref[rows, 2 * part + zc:2 * part + zc + REC_HEAD_DIM], 0.0)
    vb = v.astype(BF16)
    a = _cumsum_rows(logf)
    a_mid = a[HALF - 1:HALF, :]
    a_end = a[CHUNK - 1:CHUNK, :]
    if spans is None:
        scores = _decay_scores_direct(q, kr, a)
    else:
        qh = (q * jnp.exp(a - a_mid)).astype(BF16)
        kh_ = (kr * jnp.exp(a_mid - a)).astype(BF16)
        scores = _dot_nt(qh, kh_)
        spans.append(jnp.maximum(a[0:1, :] - a_mid, a_mid - a_end))
    qt = (q * jnp.exp(a)).astype(BF16)
    kt = (kr * jnp.exp(a_end - a)).astype(BF16)
    st = st_ref[hh]
    o_state = _dot_nt(qt, st.astype(BF16))
    yield
    att = jnp.where(causal, scores, 0.0)
    o = _dot(att.astype(BF16), vb) + o_state
    st_ref[hh] = st * jnp.exp(a_end) + _dot_tn(vb, kt)
    yield
    on = o * lax.rsqrt(jnp.mean(o * o, axis=-1, keepdims=True) + EPS)
    gr = zr_ref[rows, 3 * part + zc:3 * part + zc + REC_HEAD_DIM]
    yr = (on * recw_ref[:, cs:cs + REC_HEAD_DIM]) * _silu(gr)
    y_ref[rows, D_ATTN + cs:D_ATTN + cs + REC_HEAD_DIM] = yr.astype(BF16)


def _emit_staggered(units, fillers=(), lag=1):
    n = len(units)
    rounds = n + 2 * lag
    at_round = [[] for _ in range(rounds)]
    for k, f in enumerate(fillers):
        at_round[k * rounds // len(fillers)].append(f)
    for i in range(rounds):
        if i < n:
            next(units[i])
        for f in at_round[i]:
            f()
        if 0 <= i - lag < n:
            next(units[i - lag])
        if 0 <= i - 2 * lag < n:
            next(units[i - 2 * lag], None)


def _main_tile(s):
    t = jnp.maximum(s - 1, 0)
    return lax.div(t, ROW_TILES), lax.rem(t, ROW_TILES)


def _prompt_layer_kernel(layer, final, sink_ref, meta_ref, hin_ref, win_ref, wout_ref, normw_ref, lb_ref, oml_ref,
                         recw_ref, bias_ref, finw_ref, *refs):
    if final:
        hout_ref, pk_ref, pv_ref, ps_ref = refs[:4]
        refs = refs[4:]
    else:
        meta_out_ref, hout_ref, pk_ref, pv_ref, ps_ref = refs[:5]
        refs = refs[5:]
    xn_ref, za_ref, zr_ref, kv_ref, st_ref, y_ref, kv_meta_ref, st_meta_ref, st_start_ref, zhead_ref = refs
    s = pl.program_id(0)
    is_meta = s == 0
    _, j = _main_tile(s)

    @pl.when(is_meta)
    def _():
        kv_ref[0:QBLK, :] = jnp.zeros((QBLK, 2 * KV_DIM), BF16)
        st_ref[...] = jnp.zeros(st_ref.shape, F32)

    @pl.when(jnp.logical_and(jnp.logical_not(is_meta), j == 0))
    def _():
        kv_ref[0:QBLK, :] = kv_meta_ref[...]
        st_ref[...] = st_meta_ref[...]

    st_start_ref[...] = st_ref[...]

    def tile_input(cols=slice(None)):
        meta = meta_ref[:, cols]
        meta_rows = jnp.concatenate([jnp.zeros((PAD, meta.shape[1]), F32), meta], axis=0)
        return jnp.where(is_meta, meta_rows, hin_ref[0, :, cols])

    c_tile = jnp.where(is_meta, -PAD, j * TILE + N_META)
    xn_ref[...] = _rmsnorm_rows(tile_input(), normw_ref[...]).astype(BF16)

    def project(dst_ref, dst_col, src_col):
        dst_ref[:, dst_col:dst_col + PROJ_COLS] = _dot(xn_ref[...], win_ref[:, src_col:src_col + PROJ_COLS])

    def project_rec(dst_ref, group):
        return [functools.partial(project, dst_ref, part * PROJ_COLS, off + group * PROJ_COLS)
                for part, off in enumerate((OFF_QR, OFF_FR, OFF_IR, OFF_GR))]

    def out_proj(k0, k1, c0, first_half):
        cols = slice(c0, c0 + PROJ_COLS)
        base = tile_input(cols) if first_half else hout_ref[0, :, cols]
        hout_ref[0, :, cols] = base + _dot(y_ref[:, k0:k1], wout_ref[k0:k1, cols])

    for c0 in range(0, ATT_COLS, PROJ_COLS):
        project(za_ref, c0, c0)
    kv_ref[QBLK:QBLK + TILE, :] = za_ref[:, OFF_KA:OFF_KA + 2 * KV_DIM].astype(BF16)
    pk_ref[0] = za_ref[TILE - WINDOW:TILE, OFF_KA:OFF_KA + KV_DIM]
    pv_ref[0] = za_ref[TILE - WINDOW:TILE, OFF_VA:OFF_VA + KV_DIM]

    att_units = [_attention_unit(layer, qb, kh, za_ref, kv_ref, y_ref, bias_ref, sink_ref, c_tile)
                 for qb in range(TILE // QBLK) for kh in range(N_KV)]
    _emit_staggered(att_units, project_rec(zr_ref, 0))
    kv_ref[0:QBLK, :] = kv_ref[TILE:TILE + QBLK, :]

    n_groups = N_REC_HEADS // REC_GROUP
    z_bufs = [zr_ref, za_ref]
    spans = []

    def rec_units(grp, state_ref, spans_):
        return [_hgrn2_unit(ci, grp * REC_GROUP + hl, z_bufs[grp % 2], state_ref, y_ref, lb_ref, oml_ref,
                            recw_ref, c_tile, spans_)
                for ci in range(TILE // CHUNK) for hl in range(REC_GROUP)]

    for grp in range(n_groups):
        if grp + 1 < n_groups:
            fillers = project_rec(z_bufs[(grp + 1) % 2], grp + 1)
        else:
            fillers = [functools.partial(out_proj, 0, D_ATTN, c0, True) for c0 in range(0, D_MODEL, PROJ_COLS)]
        _emit_staggered(rec_units(grp, st_ref, spans), fillers)
    for c0 in range(0, D_MODEL, PROJ_COLS):
        out_proj(D_ATTN, D_MODEL, c0, False)

    widest = jnp.max(jnp.concatenate(spans, axis=0))

    @pl.when(widest > MAX_FACTOR_EXPONENT)
    def _():
        def replay_chunk(hh, ci, carry):
            for _ in _hgrn2_unit(ci, hh, zhead_ref, st_start_ref, y_ref, lb_ref, oml_ref, recw_ref, c_tile,
                                 spans=None, heads_in_buf=1):
                pass
            return carry

        for hh in range(N_REC_HEADS):
            xn = _rmsnorm_rows(tile_input(), normw_ref[...]).astype(BF16)
            for part, off in enumerate((OFF_QR, OFF_FR, OFF_IR, OFF_GR)):
                src = off + hh * REC_HEAD_DIM
                zhead_ref[:, part * REC_HEAD_DIM:(part + 1) * REC_HEAD_DIM] = _dot(
                    xn, win_ref[:, src:src + REC_HEAD_DIM])
            lax.fori_loop(0, TILE // CHUNK, functools.partial(replay_chunk, hh), 0)
        for c0 in range(0, D_MODEL, PROJ_COLS):
            cols = slice(c0, c0 + PROJ_COLS)
            hout_ref[0, :, cols] = tile_input(cols) + _dot(y_ref[...], wout_ref[:, cols])

    ps_ref[0] = st_ref[...]

    @pl.when(is_meta)
    def _():
        kv_meta_ref[...] = kv_ref[0:QBLK, :]
        st_meta_ref[...] = st_ref[...]
        if not final:
            meta_out_ref[...] = hout_ref[0, PAD:TILE, :]

    if final:
        @pl.when(jnp.logical_not(is_meta))
        def _():
            hout_ref[0] = _rmsnorm_rows(hout_ref[0], finw_ref[...])


def _layer_block(shape, layer):
    nd = len(shape)
    return pl.BlockSpec((None,) + tuple(shape), lambda s: (layer,) + (0,) * nd, pipeline_mode=pl.Buffered(1))


def _const_block(shape):
    nd = len(shape)
    return pl.BlockSpec(tuple(shape), lambda s: (0,) * nd, pipeline_mode=pl.Buffered(1))


def _prompt_layer(layer, h_meta, h_main, w_in, w_out, norm_w, lb, oml, rec_w, sinks, bias, fin_w):
    final = layer == DEPTH - 1
    main = pl.BlockSpec((1, TILE, D_MODEL), lambda s: _main_tile(s) + (0,))
    meta = pl.BlockSpec((N_META, D_MODEL), lambda s: (0, 0))
    kv_spec = pl.BlockSpec((1, WINDOW, KV_DIM), lambda s: (_main_tile(s)[0], 0, 0))
    st_spec = pl.BlockSpec((1, N_REC_HEADS, REC_HEAD_DIM, REC_HEAD_DIM), lambda s: (_main_tile(s)[0], 0, 0, 0))
    main_shape = jax.ShapeDtypeStruct((BATCH, SEQ, D_MODEL), F32)
    meta_shape = jax.ShapeDtypeStruct((N_META, D_MODEL), F32)
    return pl.pallas_call(
        functools.partial(_prompt_layer_kernel, layer, final),
        grid=(1 + BATCH * ROW_TILES,),
        in_specs=[
            pl.BlockSpec(memory_space=pltpu.SMEM),
            _const_block((N_META, D_MODEL)),
            main,
            _const_block((D_MODEL, D_IN)),
            _const_block((D_MODEL, D_MODEL)),
            _layer_block((1, D_MODEL), layer),
            _layer_block((1, D_REC), layer),
            _layer_block((1, D_REC), layer),
            _layer_block((1, D_REC), layer),
            _const_block((N_HEADS * QBLK, 2 * QBLK)),
            _const_block((1, D_MODEL)),
        ],
        out_specs=([] if final else [meta]) + [main, kv_spec, kv_spec, st_spec],
        out_shape=([] if final else [meta_shape]) + [
            main_shape,
            jax.ShapeDtypeStruct((BATCH, WINDOW, KV_DIM), F32),
            jax.ShapeDtypeStruct((BATCH, WINDOW, KV_DIM), F32),
            jax.ShapeDtypeStruct((BATCH, N_REC_HEADS, REC_HEAD_DIM, REC_HEAD_DIM), F32),
        ],
        scratch_shapes=[
            pltpu.VMEM((TILE, D_MODEL), BF16),
            pltpu.VMEM((TILE, ATT_COLS), F32),
            pltpu.VMEM((TILE, 4 * PROJ_COLS), F32),
            pltpu.VMEM((QBLK + TILE, 2 * KV_DIM), BF16),
            pltpu.VMEM((N_REC_HEADS, REC_HEAD_DIM, REC_HEAD_DIM), F32),
            pltpu.VMEM((TILE, D_MODEL), BF16),
            pltpu.VMEM((QBLK, 2 * KV_DIM), BF16),
            pltpu.VMEM((N_REC_HEADS, REC_HEAD_DIM, REC_HEAD_DIM), F32),
            pltpu.VMEM((N_REC_HEADS, REC_HEAD_DIM, REC_HEAD_DIM), F32),
            pltpu.VMEM((TILE, 4 * REC_HEAD_DIM), F32),
        ],
        compiler_params=pltpu.CompilerParams(
            dimension_semantics=("arbitrary",), vmem_limit_bytes=VMEM_LIMIT),
        name="prompt_layer",
    )(sinks, h_meta, h_main, w_in, w_out, norm_w, lb, oml, rec_w, bias, fin_w)


def _dec_proj_kernel(hs_ref, normw_ref, win_ref, z_ref, wb_ref):
    xn = _rmsnorm_rows(hs_ref[...], normw_ref[...]).astype(BF16)
    wb = win_ref[...].astype(BF16)
    wb_ref[...] = wb
    z_ref[...] = _dot(xn, wb)


def _dec_proj(layer, hs, norm_w, w_in):
    return pl.pallas_call(
        _dec_proj_kernel,
        grid=(D_IN // PROJ_COLS,),
        in_specs=[
            pl.BlockSpec((N_DEC, D_MODEL), lambda n: (0, 0)),
            pl.BlockSpec((None, 1, D_MODEL), lambda n: (layer, 0, 0)),
            pl.BlockSpec((None, D_MODEL, PROJ_COLS), lambda n: (layer, 0, n)),
        ],
        out_specs=[pl.BlockSpec((N_DEC, PROJ_COLS), lambda n: (0, n)),
                   pl.BlockSpec((D_MODEL, PROJ_COLS), lambda n: (0, n))],
        out_shape=[jax.ShapeDtypeStruct((N_DEC, D_IN), F32),
                   jax.ShapeDtypeStruct((D_MODEL, D_IN), BF16)],
        compiler_params=pltpu.CompilerParams(dimension_semantics=("arbitrary",)),
        name="dec_proj",
    )(hs, norm_w, w_in)


def _dec_out_kernel(hs_ref, y_ref, wout_ref, o_ref, wb_ref):
    wb = wout_ref[...].astype(BF16)
    wb_ref[...] = wb
    o_ref[...] = hs_ref[...] + _dot(y_ref[...], wb)


def _dec_out(layer, hs, y, w_out):
    return pl.pallas_call(
        _dec_out_kernel,
        grid=(D_MODEL // PROJ_COLS,),
        in_specs=[
            pl.BlockSpec((N_DEC, PROJ_COLS), lambda n: (0, n)),
            pl.BlockSpec((N_DEC, D_MODEL), lambda n: (0, 0)),
            pl.BlockSpec((None, D_MODEL, PROJ_COLS), lambda n: (layer, 0, n)),
        ],
        out_specs=[pl.BlockSpec((N_DEC, PROJ_COLS), lambda n: (0, n)),
                   pl.BlockSpec((D_MODEL, PROJ_COLS), lambda n: (0, n))],
        out_shape=[jax.ShapeDtypeStruct((N_DEC, D_MODEL), F32),
                   jax.ShapeDtypeStruct((D_MODEL, D_MODEL), BF16)],
        compiler_params=pltpu.CompilerParams(dimension_semantics=("arbitrary",)),
        name="dec_out",
    )(hs, y, w_out)


DEC_SEQS_PER_STEP = 4
DEC_LAG = 8


def _dec_attention_unit(layer, q, kh, sink_ref, z_ref, ck_ref, cv_ref, biasc_ref, biasn_ref, y_ref):
    t = DEC_SEQ
    tok = slice(q * t, (q + 1) * t)
    hs_ = slice(kh * HEAD_DIM, (kh + 1) * HEAD_DIM)
    kc = ck_ref[q, :, hs_].astype(BF16)
    kn = z_ref[tok, OFF_KA + kh * HEAD_DIM:OFF_KA + (kh + 1) * HEAD_DIM]
    q4 = jnp.concatenate(
        [z_ref[tok, (kh * GROUP + g) * HEAD_DIM:(kh * GROUP + g + 1) * HEAD_DIM] for g in range(GROUP)],
        axis=0) * (HEAD_DIM ** -0.5)
    rows = slice(kh * GROUP * t, (kh + 1) * GROUP * t)
    sc = _dot_nt(q4.astype(BF16), kc)
    sn = _dot_nt(q4, kn)
    yield
    sc = sc + biasc_ref[rows, :]
    sn = sn + biasn_ref[rows, :]
    ecs, ens, dens = [], [], []
    for g in range(GROUP):
        sink = sink_ref[layer, kh * GROUP + g]
        scg = sc[g * t:(g + 1) * t]
        sng = sn[g * t:(g + 1) * t]
        m = jnp.maximum(jnp.maximum(jnp.max(scg, axis=-1, keepdims=True),
                                    jnp.max(sng, axis=-1, keepdims=True)), sink)
        ec = jnp.exp(scg - m)
        en = jnp.exp(sng - m)
        dens.append(jnp.sum(ec, axis=-1, keepdims=True) + jnp.sum(en, axis=-1, keepdims=True)
                    + jnp.exp(sink - m))
        ecs.append(ec)
        ens.append(en)
    vc = cv_ref[q, :, hs_]
    vn = z_ref[tok, OFF_VA + kh * HEAD_DIM:OFF_VA + (kh + 1) * HEAD_DIM]
    o4 = (_dot(jnp.concatenate(ecs, axis=0).astype(BF16), vc.astype(BF16))
          + _dot(jnp.concatenate(ens, axis=0), vn))
    yield
    outs = []
    for g in range(GROUP):
        h = kh * GROUP + g
        og = o4[g * t:(g + 1) * t] / dens[g]
        ga = z_ref[tok, OFF_GA + h * HEAD_DIM:OFF_GA + (h + 1) * HEAD_DIM]
        outs.append(og * _silu(ga))
    y_ref[tok, kh * GROUP * HEAD_DIM:(kh + 1) * GROUP * HEAD_DIM] = jnp.concatenate(outs, axis=-1).astype(BF16)


def _dec_hgrn2_unit(q, hh, z_ref, s0_ref, lb_ref, oml_ref, recw_ref, y_ref, ns_ref):
    t = DEC_SEQ
    tok = slice(q * t, (q + 1) * t)
    row = lax.broadcasted_iota(jnp.int32, (t, t), 0)
    col = lax.broadcasted_iota(jnp.int32, (t, t), 1)
    causal = col <= row
    cs = hh * REC_HEAD_DIM
    lb = lb_ref[:, cs:cs + REC_HEAD_DIM]
    oml = oml_ref[:, cs:cs + REC_HEAD_DIM]
    logf, kr = _gate_features(z_ref[tok, OFF_FR + cs:OFF_FR + cs + REC_HEAD_DIM], lb, oml)
    qv = _silu(z_ref[tok, OFF_QR + cs:OFF_QR + cs + REC_HEAD_DIM])
    v = z_ref[tok, OFF_IR + cs:OFF_IR + cs + REC_HEAD_DIM]
    a = _cumsum_rows(logf)
    a_end = a[t - 1:t, :]
    scores = _decay_scores_direct(qv, kr, a)
    qt = qv * jnp.exp(a)
    kt = kr * jnp.exp(a_end - a)
    s0 = s0_ref[q, hh]
    o_state = _dot(qt, s0)
    yield
    att = jnp.where(causal, scores, 0.0)
    o = _dot(att, v) + o_state
    ktd = jnp.concatenate([kt, jnp.exp(a_end), jnp.zeros((t - 1, REC_HEAD_DIM), F32)], axis=0).T
    ns_ref[q, hh] = s0 * ktd[:, t:t + 1] + _dot(ktd[:, 0:t], v)
    yield
    on = o * lax.rsqrt(jnp.mean(o * o, axis=-1, keepdims=True) + EPS)
    gr = z_ref[tok, OFF_GR + cs:OFF_GR + cs + REC_HEAD_DIM]
    yr = (on * recw_ref[:, cs:cs + REC_HEAD_DIM]) * _silu(gr)
    y_ref[tok, D_ATTN + cs:D_ATTN + cs + REC_HEAD_DIM] = yr.astype(BF16)


def _dec_mix_kernel(layer, sink_ref, z_ref, ck_ref, cv_ref, s0_ref, lb_ref, oml_ref, recw_ref, biasc_ref,
                    biasn_ref, y_ref, nk_ref, nv_ref, ns_ref):
    t = DEC_SEQ
    for q in range(DEC_SEQS_PER_STEP):
        tok = slice(q * t, (q + 1) * t)
        nk_ref[q, 0:WINDOW - t, :] = ck_ref[q, t:WINDOW, :]
        nk_ref[q, WINDOW - t:WINDOW, :] = z_ref[tok, OFF_KA:OFF_KA + KV_DIM]
        nv_ref[q, 0:WINDOW - t, :] = cv_ref[q, t:WINDOW, :]
        nv_ref[q, WINDOW - t:WINDOW, :] = z_ref[tok, OFF_VA:OFF_VA + KV_DIM]
    units = []
    for kh in range(N_KV):
        units += [_dec_attention_unit(layer, q, kh, sink_ref, z_ref, ck_ref, cv_ref, biasc_ref, biasn_ref, y_ref)
                  for q in range(DEC_SEQS_PER_STEP)]
    for hh in range(N_REC_HEADS):
        units += [_dec_hgrn2_unit(q, hh, z_ref, s0_ref, lb_ref, oml_ref, recw_ref, y_ref, ns_ref)
                  for q in range(DEC_SEQS_PER_STEP)]
    _emit_staggered(units, lag=DEC_LAG)


def _dec_mix(layer, z, ck, cv, s0, lb, oml, rec_w, sinks, bias_c, bias_n):
    t = DEC_SEQ
    nq = DEC_SEQS_PER_STEP
    const2 = lambda s: (0, 0)
    per_layer = lambda s: (layer, 0, 0)
    kv_in = pl.BlockSpec((None, nq, WINDOW, KV_DIM), lambda s: (layer, s, 0, 0))
    st_in = pl.BlockSpec((None, nq, N_REC_HEADS, REC_HEAD_DIM, REC_HEAD_DIM), lambda s: (layer, s, 0, 0, 0))
    return pl.pallas_call(
        functools.partial(_dec_mix_kernel, layer),
        grid=(DEC_BATCH // nq,),
        in_specs=[
            pl.BlockSpec(memory_space=pltpu.SMEM),
            pl.BlockSpec((nq * t, D_IN), lambda s: (s, 0)),
            kv_in,
            kv_in,
            st_in,
            pl.BlockSpec((None, 1, D_REC), per_layer),
            pl.BlockSpec((None, 1, D_REC), per_layer),
            pl.BlockSpec((None, 1, D_REC), per_layer),
            pl.BlockSpec((N_HEADS * t, WINDOW), const2),
            pl.BlockSpec((N_HEADS * t, t), const2),
        ],
        out_specs=[
            pl.BlockSpec((nq * t, D_MODEL), lambda s: (s, 0)),
            pl.BlockSpec((nq, WINDOW, KV_DIM), lambda s: (s, 0, 0)),
            pl.BlockSpec((nq, WINDOW, KV_DIM), lambda s: (s, 0, 0)),
            pl.BlockSpec((nq, N_REC_HEADS, REC_HEAD_DIM, REC_HEAD_DIM), lambda s: (s, 0, 0, 0)),
        ],
        out_shape=[
            jax.ShapeDtypeStruct((N_DEC, D_MODEL), BF16),
            jax.ShapeDtypeStruct((DEC_BATCH, WINDOW, KV_DIM), F32),
            jax.ShapeDtypeStruct((DEC_BATCH, WINDOW, KV_DIM), F32),
            jax.ShapeDtypeStruct((DEC_BATCH, N_REC_HEADS, REC_HEAD_DIM, REC_HEAD_DIM), F32),
        ],
        compiler_params=pltpu.CompilerParams(dimension_semantics=("arbitrary",)),
        name="dec_mix",
    )(sinks, z, ck, cv, s0, lb, oml, rec_w, bias_c, bias_n)


def _final_norm_kernel(x_ref, w_ref, o_ref):
    o_ref[...] = _rmsnorm_rows(x_ref[...], w_ref[...])


def _final_norm_dec(hs, w):
    return pl.pallas_call(
        _final_norm_kernel,
        out_shape=jax.ShapeDtypeStruct((N_DEC, D_MODEL), F32),
        name="final_norm_dec",
    )(hs, w)


def _t5_bucket(dist):
    max_exact = N_BUCKETS // 2
    d = jnp.maximum(dist, 0)
    df = jnp.maximum(d, 1).astype(F32)
    large = max_exact + (jnp.log(df / max_exact) / math.log(MAX_DISTANCE / max_exact)
                         * (N_BUCKETS - max_exact)).astype(jnp.int32)
    large = jnp.minimum(large, N_BUCKETS - 1)
    return jnp.where(d < max_exact, d, large)


def _bias_tables(rel_bias_table):
    bias_d = rel_bias_table[_t5_bucket(jnp.arange(WINDOW))].T.astype(F32)
    period = 3 * WINDOW + 1
    u = jnp.concatenate([jnp.full((N_HEADS, 1), NEG, F32), bias_d[:, ::-1],
                         jnp.full((N_HEADS, period - 1 - WINDOW), NEG, F32)], axis=1)
    rows = jnp.tile(u, (1, WINDOW))[:, :WINDOW * (period - 1)].reshape(N_HEADS, WINDOW, period - 1)
    prompt = rows[:, :, :2 * WINDOW].reshape(N_HEADS * WINDOW, 2 * WINDOW).astype(BF16)
    dec = rows[:, :DEC_SEQ, :WINDOW + DEC_SEQ].reshape(N_HEADS * DEC_SEQ, WINDOW + DEC_SEQ)
    return prompt, dec[:, :WINDOW], dec[:, WINDOW:]


def kernel(x_prompt, x_sample, cache_k, cache_v, state_h, meta_tokens, w_in, w_out, norm_w, final_norm_w,
           attn_sinks, rel_bias_table, hgrn_lb_logits, hgrn_norm_w):
    pl_ = jax.nn.softmax(hgrn_lb_logits.astype(F32), axis=0)
    lb = (jnp.cumsum(pl_, axis=0) - pl_[0:1]).reshape(DEPTH, 1, D_REC)
    oml = 1.0 - lb
    bias_p, bias_dc, bias_dn = _bias_tables(rel_bias_table)
    nw = norm_w.astype(F32).reshape(DEPTH, 1, D_MODEL)
    rw = hgrn_norm_w.astype(F32).reshape(DEPTH, 1, D_REC)
    fw = final_norm_w.astype(F32).reshape(1, D_MODEL)
    sinks = attn_sinks.astype(F32)

    h_meta = meta_tokens.astype(F32)
    h_main = x_prompt
    hs = x_sample.reshape(N_DEC, D_MODEL)
    ck = cache_k.reshape(DEPTH, DEC_BATCH, WINDOW, KV_DIM)
    cv = cache_v.reshape(DEPTH, DEC_BATCH, WINDOW, KV_DIM)

    pk, pv, ps, sk, sv, ss = [], [], [], [], [], []
    for l in range(DEPTH):
        z, w_in_b = _dec_proj(l, hs, nw, w_in)
        y, nk, nv, ns = _dec_mix(l, z, ck, cv, state_h, lb, oml, rw, sinks, bias_dc, bias_dn)
        hs, w_out_b = _dec_out(l, hs, y, w_out)
        sk.append(nk)
        sv.append(nv)
        ss.append(ns)

        outs = _prompt_layer(l, h_meta, h_main, w_in_b, w_out_b, nw, lb, oml, rw, sinks, bias_p, fw)
        if l < DEPTH - 1:
            h_meta, h_main, k_l, v_l, s_l = outs
        else:
            h_main, k_l, v_l, s_l = outs
        pk.append(k_l)
        pv.append(v_l)
        ps.append(s_l)

    y_prompt = h_main
    y_sample = _final_norm_dec(hs, fw).reshape(DEC_BATCH, DEC_SEQ, D_MODEL)
    kv_shape = (DEPTH, -1, WINDOW, N_KV, HEAD_DIM)
    return (y_prompt, y_sample,
            jnp.stack(pk).reshape(kv_shape), jnp.stack(pv).reshape(kv_shape),
            jnp.swapaxes(jnp.stack(ps), -1, -2),
            jnp.stack(sk).reshape(kv_shape), jnp.stack(sv).reshape(kv_shape), jnp.stack(ss))
```

```python
import functools
import math

import jax
import jax.numpy as jnp
from jax import lax
from jax.experimental import pallas as pl
from jax.experimental.pallas import tpu as pltpu

D_MODEL = 2048
BATCH = 4
SEQ = 2048
DEPTH = 4
DEC_BATCH = 32
DEC_SEQ = 8
N_META = 16
D_ATTN = 1024
D_REC = 1024
HEAD_DIM = 64
N_HEADS = 16
N_KV = 4
GROUP = 4
KV_DIM = 256
WINDOW = 128
N_BUCKETS = 32
MAX_DISTANCE = 128
REC_HEAD_DIM = 128
N_REC_HEADS = 8
EPS = 1e-6
D_IN = 2 * D_ATTN + 2 * KV_DIM + 4 * D_REC

OFF_QA = 0
OFF_KA = D_ATTN
OFF_VA = OFF_KA + KV_DIM
OFF_GA = OFF_VA + KV_DIM
OFF_QR = OFF_GA + D_ATTN
OFF_FR = OFF_QR + D_REC
OFF_IR = OFF_FR + D_REC
OFF_GR = OFF_IR + D_REC
ATT_COLS = OFF_QR

TILE = 256
QBLK = WINDOW
CHUNK = 64
HALF = CHUNK // 2
ROW_TILES = SEQ // TILE
PAD = TILE - N_META
PROJ_COLS = 512
REC_GROUP = PROJ_COLS // REC_HEAD_DIM
NEG = -1e30
MAX_FACTOR_EXPONENT = 80.0
N_DEC = DEC_BATCH * DEC_SEQ
V7X_VMEM_BYTES = 64 * 1024 * 1024
VMEM_LIMIT = V7X_VMEM_BYTES - 512 * 1024

F32 = jnp.float32
BF16 = jnp.bfloat16


def _sigmoid_pair(x):
    e = jnp.exp(-jnp.abs(x))
    inv = 1.0 / (1.0 + e)
    pos = x >= 0
    return jnp.where(pos, inv, e * inv), jnp.where(pos, e * inv, inv)


def _silu(x):
    return x * (0.5 * jnp.tanh(0.5 * x) + 0.5)


def _cumsum_rows(x):
    n = x.shape[0]
    row = lax.broadcasted_iota(jnp.int32, x.shape, 0)
    k = 1
    while k < n:
        x = x + jnp.where(row >= k, pltpu.roll(x, k, axis=0), 0.0)
        k *= 2
    return x


def _rmsnorm_rows(x, w):
    ms = jnp.mean(x * x, axis=-1, keepdims=True)
    return (x * lax.rsqrt(ms + EPS)) * w


def _dot_nt(a, b):
    return lax.dot_general(a, b, (((1,), (1,)), ((), ())), preferred_element_type=F32)


def _dot_tn(a, b):
    return lax.dot_general(a, b, (((0,), (0,)), ((), ())), preferred_element_type=F32)


def _dot(a, b):
    return jnp.dot(a, b, preferred_element_type=F32)


def _gate_features(fr, lb, oml):
    sig, sneg = _sigmoid_pair(fr)
    logf = jnp.log(lb + oml * sig)
    return logf, oml * sneg


def _attention_unit(layer, qb, kh, za_ref, kv_ref, y_ref, bias_ref, sink_ref, c_tile):
    r0 = qb * QBLK
    kpos = c_tile + r0 - QBLK + lax.broadcasted_iota(jnp.int32, (1, 2 * QBLK), 1)
    kvalid = kpos >= 0
    kk = kv_ref[r0:r0 + 2 * QBLK, kh * HEAD_DIM:(kh + 1) * HEAD_DIM]
    q4 = jnp.concatenate(
        [za_ref[r0:r0 + QBLK, (kh * GROUP + g) * HEAD_DIM:(kh * GROUP + g + 1) * HEAD_DIM]
         for g in range(GROUP)], axis=0)
    s = _dot_nt((q4 * (HEAD_DIM ** -0.5)).astype(BF16), kk)
    yield
    s = s + bias_ref[kh * GROUP * QBLK:(kh + 1) * GROUP * QBLK, :]
    s = jnp.where(kvalid, s, NEG)
    es, dens = [], []
    for g in range(GROUP):
        sink = sink_ref[layer, kh * GROUP + g]
        sg = s[g * QBLK:(g + 1) * QBLK]
        m = jnp.maximum(jnp.max(sg, axis=-1, keepdims=True), sink)
        e = jnp.exp(sg - m)
        dens.append(jnp.sum(e, axis=-1, keepdims=True) + jnp.exp(sink - m))
        es.append(e.astype(BF16))
    vv = kv_ref[r0:r0 + 2 * QBLK, KV_DIM + kh * HEAD_DIM:KV_DIM + (kh + 1) * HEAD_DIM]
    o4 = _dot(jnp.concatenate(es, axis=0), vv)
    yield
    outs = []
    for g in range(GROUP):
        h = kh * GROUP + g
        og = o4[g * QBLK:(g + 1) * QBLK] / dens[g]
        ga = za_ref[r0:r0 + QBLK, OFF_GA + h * HEAD_DIM:OFF_GA + (h + 1) * HEAD_DIM]
        outs.append(og * _silu(ga))
    y_ref[r0:r0 + QBLK, kh * GROUP * HEAD_DIM:(kh + 1) * GROUP * HEAD_DIM] = (
        jnp.concatenate(outs, axis=-1).astype(BF16))


def _decay_scores_direct(q, kr, a):
    n = q.shape[0]
    row = lax.broadcasted_iota(jnp.int32, (n, n), 0)
    col = lax.broadcasted_iota(jnp.int32, (n, n), 1)

    def diagonal(d, acc):
        ks = pltpu.roll(kr, d, axis=0)
        a_s = pltpu.roll(a, d, axis=0)
        p = q * ks * jnp.exp(jnp.minimum(a - a_s, 0.0))
        return jnp.where(col == row - d, jnp.sum(p, axis=-1, keepdims=True), acc)

    acc = jnp.zeros((n, n), F32)
    if n <= DEC_SEQ:
        for d in range(n):
            acc = diagonal(d, acc)
        return acc
    return lax.fori_loop(0, n, diagonal, acc)


def _hgrn2_unit(ci, hh, zr_ref, st_ref, y_ref, lb_ref, oml_ref, recw_ref, c_tile, spans=None, heads_in_buf=None):
    heads_in_buf = REC_GROUP if heads_in_buf is None else heads_in_buf
    part = heads_in_buf * REC_HEAD_DIM
    row = lax.broadcasted_iota(jnp.int32, (CHUNK, CHUNK), 0)
    col = lax.broadcasted_iota(jnp.int32, (CHUNK, CHUNK), 1)
    causal = col <= row
    r0 = ci * CHUNK if isinstance(ci, int) else pl.multiple_of(ci * CHUNK, CHUNK)
    rows = pl.ds(r0, CHUNK)
    rpos = c_tile + r0 + lax.broadcasted_iota(jnp.int32, (CHUNK, 1), 0)
    rvalid = rpos >= 0
    cs = hh * REC_HEAD_DIM
    zc = (hh % heads_in_buf) * REC_HEAD_DIM
    lb = lb_ref[:, cs:cs + REC_HEAD_DIM]
    oml = oml_ref[:, cs:cs + REC_HEAD_DIM]
    logf, kr = _gate_features(zr_ref[rows, part + zc:part + zc + REC_HEAD_DIM], lb, oml)
    q = _silu(zr_ref[rows, zc:zc + REC_HEAD_DIM])
    v = jnp.where(rvalid, zr_ref[rows, 2 * part + zc:2 * part + zc + REC_HEAD_DIM], 0.0)
    vb = v.astype(BF16)
    a = _cumsum_rows(logf)
    a_mid = a[HALF - 1:HALF, :]
    a_end = a[CHUNK - 1:CHUNK, :]
    if spans is None:
        scores = _decay_scores_direct(q, kr, a)
    else:
        qh = (q * jnp.exp(a - a_mid)).astype(BF16)
        kh_ = (kr * jnp.exp(a_mid - a)).astype(BF16)
        scores = _dot_nt(qh, kh_)
        spans.append(jnp.maximum(a[0:1, :] - a_mid, a_mid - a_end))
    qt = (q * jnp.exp(a)).astype(BF16)
    kt = (kr * jnp.exp(a_end - a)).astype(BF16)
    st = st_ref[hh]
    o_state = _dot_nt(qt, st.astype(BF16))
    yield
    att = jnp.where(causal, scores, 0.0)
    o = _dot(att.astype(BF16), vb) + o_state
    st_ref[hh] = st * jnp.exp(a_end) + _dot_tn(vb, kt)
    yield
    on = o * lax.rsqrt(jnp.mean(o * o, axis=-1, keepdims=True) + EPS)
    gr = zr_ref[rows, 3 * part + zc:3 * part + zc + REC_HEAD_DIM]
    yr = (on * recw_ref[:, cs:cs + REC_HEAD_DIM]) * _silu(gr)
    y_ref[rows, D_ATTN + cs:D_ATTN + cs + REC_HEAD_DIM] = yr.astype(BF16)


def _emit_staggered(units, fillers=(), lag=1):
    n = len(units)
    rounds = n + 2 * lag
    at_round = [[] for _ in range(rounds)]
    for k, f in enumerate(fillers):
        at_round[k * rounds // len(fillers)].append(f)
    for i in range(rounds):
        if i < n:
            next(units[i])
        for f in at_round[i]:
            f()
        if 0 <= i - lag < n:
            next(units[i - lag])
        if 0 <= i - 2 * lag < n:
            next(units[i - 2 * lag], None)


def _main_tile(s):
    t = jnp.maximum(s - 1, 0)
    return lax.div(t, ROW_TILES), lax.rem(t, ROW_TILES)


def _prompt_layer_kernel(layer, final, sink_ref, meta_ref, hin_ref, win_ref, wout_ref, normw_ref, lb_ref, oml_ref,
                         recw_ref, bias_ref, finw_ref, *refs):
    if final:
        hout_ref, pk_ref, pv_ref, ps_ref = refs[:4]
        refs = refs[4:]
    else:
        meta_out_ref, hout_ref, pk_ref, pv_ref, ps_ref = refs[:5]
        refs = refs[5:]
    xn_ref, za_ref, zr_ref, kv_ref, st_ref, y_ref, kv_meta_ref, st_meta_ref, st_start_ref, zhead_ref = refs
    s = pl.program_id(0)
    is_meta = s == 0
    _, j = _main_tile(s)

    @pl.when(is_meta)
    def _():
        kv_ref[0:QBLK, :] = jnp.zeros((QBLK, 2 * KV_DIM), BF16)
        st_ref[...] = jnp.zeros(st_ref.shape, F32)

    @pl.when(jnp.logical_and(jnp.logical_not(is_meta), j == 0))
    def _():
        kv_ref[0:QBLK, :] = kv_meta_ref[...]
        st_ref[...] = st_meta_ref[...]

    st_start_ref[...] = st_ref[...]

    def tile_input(cols=slice(None)):
        meta = meta_ref[:, cols]
        meta_rows = jnp.concatenate([jnp.zeros((PAD, meta.shape[1]), F32), meta], axis=0)
        return jnp.where(is_meta, meta_rows, hin_ref[0, :, cols])

    c_tile = jnp.where(is_meta, -PAD, j * TILE + N_META)
    xn_ref[...] = _rmsnorm_rows(tile_input(), normw_ref[...]).astype(BF16)

    def project(dst_ref, dst_col, src_col):
        dst_ref[:, dst_col:dst_col + PROJ_COLS] = _dot(xn_ref[...], win_ref[:, src_col:src_col + PROJ_COLS])

    def project_rec(dst_ref, group):
        return [functools.partial(project, dst_ref, part * PROJ_COLS, off + group * PROJ_COLS)
                for part, off in enumerate((OFF_QR, OFF_FR, OFF_IR, OFF_GR))]

    def out_proj(k0, k1, c0, first_half):
        cols = slice(c0, c0 + PROJ_COLS)
        base = tile_input(cols) if first_half else hout_ref[0, :, cols]
        hout_ref[0, :, cols] = base + _dot(y_ref[:, k0:k1], wout_ref[k0:k1, cols])

    for c0 in range(0, ATT_COLS, PROJ_COLS):
        project(za_ref, c0, c0)
    kv_ref[QBLK:QBLK + TILE, :] = za_ref[:, OFF_KA:OFF_KA + 2 * KV_DIM].astype(BF16)
    pk_ref[0] = za_ref[TILE - WINDOW:TILE, OFF_KA:OFF_KA + KV_DIM]
    pv_ref[0] = za_ref[TILE - WINDOW:TILE, OFF_VA:OFF_VA + KV_DIM]

    att_units = [_attention_unit(layer, qb, kh, za_ref, kv_ref, y_ref, bias_ref, sink_ref, c_tile)
                 for qb in range(TILE // QBLK) for kh in range(N_KV)]
    _emit_staggered(att_units, project_rec(zr_ref, 0))
    kv_ref[0:QBLK, :] = kv_ref[TILE:TILE + QBLK, :]

    n_groups = N_REC_HEADS // REC_GROUP
    z_bufs = [zr_ref, za_ref]
    spans = []

    def rec_units(grp, state_ref, spans_):
        return [_hgrn2_unit(ci, grp * REC_GROUP + hl, z_bufs[grp % 2], state_ref, y_ref, lb_ref, oml_ref,
                            recw_ref, c_tile, spans_)
                for ci in range(TILE // CHUNK) for hl in range(REC_GROUP)]

    for grp in range(n_groups):
        if grp + 1 < n_groups:
            fillers = project_rec(z_bufs[(grp + 1) % 2], grp + 1)
        else:
            fillers = [functools.partial(out_proj, 0, D_ATTN, c0, True) for c0 in range(0, D_MODEL, PROJ_COLS)]
        _emit_staggered(rec_units(grp, st_ref, spans), fillers)
    for c0 in range(0, D_MODEL, PROJ_COLS):
        out_proj(D_ATTN, D_MODEL, c0, False)

    widest = jnp.max(jnp.concatenate(spans, axis=0))

    @pl.when(widest > MAX_FACTOR_EXPONENT)
    def _():
        def replay_chunk(hh, ci, carry):
            for _ in _hgrn2_unit(ci, hh, zhead_ref, st_start_ref, y_ref, lb_ref, oml_ref, recw_ref, c_tile,
                                 spans=None, heads_in_buf=1):
                pass
            return carry

        for hh in range(N_REC_HEADS):
            for part, off in enumerate((OFF_QR, OFF_FR, OFF_IR, OFF_GR)):
                src = off + hh * REC_HEAD_DIM
                zhead_ref[:, part * REC_HEAD_DIM:(part + 1) * REC_HEAD_DIM] = _dot(
                    xn_ref[...], win_ref[:, src:src + REC_HEAD_DIM])
            lax.fori_loop(0, TILE // CHUNK, functools.partial(replay_chunk, hh), 0)
        for c0 in range(0, D_MODEL, PROJ_COLS):
            cols = slice(c0, c0 + PROJ_COLS)
            hout_ref[0, :, cols] = tile_input(cols) + _dot(y_ref[...], wout_ref[:, cols])

    ps_ref[0] = st_ref[...]

    @pl.when(is_meta)
    def _():
        kv_meta_ref[...] = kv_ref[0:QBLK, :]
        st_meta_ref[...] = st_ref[...]
        if not final:
            meta_out_ref[...] = hout_ref[0, PAD:TILE, :]

    if final:
        @pl.when(jnp.logical_not(is_meta))
        def _():
            hout_ref[0] = _rmsnorm_rows(hout_ref[0], finw_ref[...])


def _layer_block(shape, layer):
    nd = len(shape)
    return pl.BlockSpec((None,) + tuple(shape), lambda s: (layer,) + (0,) * nd, pipeline_mode=pl.Buffered(1))


def _const_block(shape):
    nd = len(shape)
    return pl.BlockSpec(tuple(shape), lambda s: (0,) * nd, pipeline_mode=pl.Buffered(1))


def _prompt_layer(layer, h_meta, h_main, w_in, w_out, norm_w, lb, oml, rec_w, sinks, bias, fin_w):
    final = layer == DEPTH - 1
    main = pl.BlockSpec((1, TILE, D_MODEL), lambda s: _main_tile(s) + (0,))
    meta = pl.BlockSpec((N_META, D_MODEL), lambda s: (0, 0))
    kv_spec = pl.BlockSpec((1, WINDOW, KV_DIM), lambda s: (_main_tile(s)[0], 0, 0))
    st_spec = pl.BlockSpec((1, N_REC_HEADS, REC_HEAD_DIM, REC_HEAD_DIM), lambda s: (_main_tile(s)[0], 0, 0, 0))
    main_shape = jax.ShapeDtypeStruct((BATCH, SEQ, D_MODEL), F32)
    meta_shape = jax.ShapeDtypeStruct((N_META, D_MODEL), F32)
    return pl.pallas_call(
        functools.partial(_prompt_layer_kernel, layer, final),
        grid=(1 + BATCH * ROW_TILES,),
        in_specs=[
            pl.BlockSpec(memory_space=pltpu.SMEM),
            _const_block((N_META, D_MODEL)),
            main,
            _const_block((D_MODEL, D_IN)),
            _const_block((D_MODEL, D_MODEL)),
            _layer_block((1, D_MODEL), layer),
            _layer_block((1, D_REC), layer),
            _layer_block((1, D_REC), layer),
            _layer_block((1, D_REC), layer),
            _const_block((N_HEADS * QBLK, 2 * QBLK)),
            _const_block((1, D_MODEL)),
        ],
        out_specs=([] if final else [meta]) + [main, kv_spec, kv_spec, st_spec],
        out_shape=([] if final else [meta_shape]) + [
            main_shape,
            jax.ShapeDtypeStruct((BATCH, WINDOW, KV_DIM), F32),
            jax.ShapeDtypeStruct((BATCH, WINDOW, KV_DIM), F32),
            jax.ShapeDtypeStruct((BATCH, N_REC_HEADS, REC_HEAD_DIM, REC_HEAD_DIM), F32),
        ],
        scratch_shapes=[
            pltpu.VMEM((TILE, D_MODEL), BF16),
            pltpu.VMEM((TILE, ATT_COLS), F32),
            pltpu.VMEM((TILE, 4 * PROJ_COLS), F32),
            pltpu.VMEM((QBLK + TILE, 2 * KV_DIM), BF16),
            pltpu.VMEM((N_REC_HEADS, REC_HEAD_DIM, REC_HEAD_DIM), F32),
            pltpu.VMEM((TILE, D_MODEL), BF16),
            pltpu.VMEM((QBLK, 2 * KV_DIM), BF16),
            pltpu.VMEM((N_REC_HEADS, REC_HEAD_DIM, REC_HEAD_DIM), F32),
            pltpu.VMEM((N_REC_HEADS, REC_HEAD_DIM, REC_HEAD_DIM), F32),
            pltpu.VMEM((TILE, 4 * REC_HEAD_DIM), F32),
        ],
        compiler_params=pltpu.CompilerParams(
            dimension_semantics=("arbitrary",), vmem_limit_bytes=VMEM_LIMIT),
        name="prompt_layer",
    )(sinks, h_meta, h_main, w_in, w_out, norm_w, lb, oml, rec_w, bias, fin_w)


def _dec_proj_kernel(hs_ref, normw_ref, win_ref, z_ref, wb_ref):
    xn = _rmsnorm_rows(hs_ref[...], normw_ref[...]).astype(BF16)
    wb = win_ref[...].astype(BF16)
    wb_ref[...] = wb
    z_ref[...] = _dot(xn, wb)


def _dec_proj(layer, hs, norm_w, w_in):
    return pl.pallas_call(
        _dec_proj_kernel,
        grid=(D_IN // PROJ_COLS,),
        in_specs=[
            pl.BlockSpec((N_DEC, D_MODEL), lambda n: (0, 0)),
            pl.BlockSpec((None, 1, D_MODEL), lambda n: (layer, 0, 0)),
            pl.BlockSpec((None, D_MODEL, PROJ_COLS), lambda n: (layer, 0, n)),
        ],
        out_specs=[pl.BlockSpec((N_DEC, PROJ_COLS), lambda n: (0, n)),
                   pl.BlockSpec((D_MODEL, PROJ_COLS), lambda n: (0, n))],
        out_shape=[jax.ShapeDtypeStruct((N_DEC, D_IN), F32),
                   jax.ShapeDtypeStruct((D_MODEL, D_IN), BF16)],
        compiler_params=pltpu.CompilerParams(dimension_semantics=("arbitrary",)),
        name="dec_proj",
    )(hs, norm_w, w_in)


def _dec_out_kernel(hs_ref, y_ref, wout_ref, o_ref, wb_ref):
    wb = wout_ref[...].astype(BF16)
    wb_ref[...] = wb
    o_ref[...] = hs_ref[...] + _dot(y_ref[...], wb)


def _dec_out(layer, hs, y, w_out):
    return pl.pallas_call(
        _dec_out_kernel,
        grid=(D_MODEL // PROJ_COLS,),
        in_specs=[
            pl.BlockSpec((N_DEC, PROJ_COLS), lambda n: (0, n)),
            pl.BlockSpec((N_DEC, D_MODEL), lambda n: (0, 0)),
            pl.BlockSpec((None, D_MODEL, PROJ_COLS), lambda n: (layer, 0, n)),
        ],
        out_specs=[pl.BlockSpec((N_DEC, PROJ_COLS), lambda n: (0, n)),
                   pl.BlockSpec((D_MODEL, PROJ_COLS), lambda n: (0, n))],
        out_shape=[jax.ShapeDtypeStruct((N_DEC, D_MODEL), F32),
                   jax.ShapeDtypeStruct((D_MODEL, D_MODEL), BF16)],
        compiler_params=pltpu.CompilerParams(dimension_semantics=("arbitrary",)),
        name="dec_out",
    )(hs, y, w_out)


DEC_SEQS_PER_STEP = 4
DEC_LAG = 8


def _dec_attention_unit(layer, q, kh, sink_ref, z_ref, ck_ref, cv_ref, biasc_ref, biasn_ref, y_ref):
    t = DEC_SEQ
    tok = slice(q * t, (q + 1) * t)
    hs_ = slice(kh * HEAD_DIM, (kh + 1) * HEAD_DIM)
    kc = ck_ref[q, :, hs_].astype(BF16)
    kn = z_ref[tok, OFF_KA + kh * HEAD_DIM:OFF_KA + (kh + 1) * HEAD_DIM]
    q4 = jnp.concatenate(
        [z_ref[tok, (kh * GROUP + g) * HEAD_DIM:(kh * GROUP + g + 1) * HEAD_DIM] for g in range(GROUP)],
        axis=0) * (HEAD_DIM ** -0.5)
    rows = slice(kh * GROUP * t, (kh + 1) * GROUP * t)
    sc = _dot_nt(q4.astype(BF16), kc)
    sn = _dot_nt(q4, kn)
    yield
    sc = sc + biasc_ref[rows, :]
    sn = sn + biasn_ref[rows, :]
    ecs, ens, dens = [], [], []
    for g in range(GROUP):
        sink = sink_ref[layer, kh * GROUP + g]
        scg = sc[g * t:(g + 1) * t]
        sng = sn[g * t:(g + 1) * t]
        m = jnp.maximum(jnp.maximum(jnp.max(scg, axis=-1, keepdims=True),
                                    jnp.max(sng, axis=-1, keepdims=True)), sink)
        ec = jnp.exp(scg - m)
        en = jnp.exp(sng - m)
        dens.append(jnp.sum(ec, axis=-1, keepdims=True) + jnp.sum(en, axis=-1, keepdims=True)
                    + jnp.exp(sink - m))
        ecs.append(ec)
        ens.append(en)
    vc = cv_ref[q, :, hs_]
    vn = z_ref[tok, OFF_VA + kh * HEAD_DIM:OFF_VA + (kh + 1) * HEAD_DIM]
    o4 = (_dot(jnp.concatenate(ecs, axis=0).astype(BF16), vc.astype(BF16))
          + _dot(jnp.concatenate(ens, axis=0), vn))
    yield
    outs = []
    for g in range(GROUP):
        h = kh * GROUP + g
        og = o4[g * t:(g + 1) * t] / dens[g]
        ga = z_ref[tok, OFF_GA + h * HEAD_DIM:OFF_GA + (h + 1) * HEAD_DIM]
        outs.append(og * _silu(ga))
    y_ref[tok, kh * GROUP * HEAD_DIM:(kh + 1) * GROUP * HEAD_DIM] = jnp.concatenate(outs, axis=-1).astype(BF16)


def _dec_hgrn2_unit(q, hh, z_ref, s0_ref, lb_ref, oml_ref, recw_ref, y_ref, ns_ref):
    t = DEC_SEQ
    tok = slice(q * t, (q + 1) * t)
    row = lax.broadcasted_iota(jnp.int32, (t, t), 0)
    col = lax.broadcasted_iota(jnp.int32, (t, t), 1)
    causal = col <= row
    cs = hh * REC_HEAD_DIM
    lb = lb_ref[:, cs:cs + REC_HEAD_DIM]
    oml = oml_ref[:, cs:cs + REC_HEAD_DIM]
    logf, kr = _gate_features(z_ref[tok, OFF_FR + cs:OFF_FR + cs + REC_HEAD_DIM], lb, oml)
    qv = _silu(z_ref[tok, OFF_QR + cs:OFF_QR + cs + REC_HEAD_DIM])
    v = z_ref[tok, OFF_IR + cs:OFF_IR + cs + REC_HEAD_DIM]
    a = _cumsum_rows(logf)
    a_end = a[t - 1:t, :]
    scores = _decay_scores_direct(qv, kr, a)
    qt = qv * jnp.exp(a)
    kt = kr * jnp.exp(a_end - a)
    s0 = s0_ref[q, hh]
    o_state = _dot(qt, s0)
    yield
    att = jnp.where(causal, scores, 0.0)
    o = _dot(att, v) + o_state
    ktd = jnp.concatenate([kt, jnp.exp(a_end), jnp.zeros((t - 1, REC_HEAD_DIM), F32)], axis=0).T
    ns_ref[q, hh] = s0 * ktd[:, t:t + 1] + _dot(ktd[:, 0:t], v)
    yield
    on = o * lax.rsqrt(jnp.mean(o * o, axis=-1, keepdims=True) + EPS)
    gr = z_ref[tok, OFF_GR + cs:OFF_GR + cs + REC_HEAD_DIM]
    yr = (on * recw_ref[:, cs:cs + REC_HEAD_DIM]) * _silu(gr)
    y_ref[tok, D_ATTN + cs:D_ATTN + cs + REC_HEAD_DIM] = yr.astype(BF16)


def _dec_mix_kernel(layer, sink_ref, z_ref, ck_ref, cv_ref, s0_ref, lb_ref, oml_ref, recw_ref, biasc_ref,
                    biasn_ref, y_ref, nk_ref, nv_ref, ns_ref):
    t = DEC_SEQ
    for q in range(DEC_SEQS_PER_STEP):
        tok = slice(q * t, (q + 1) * t)
        nk_ref[q, 0:WINDOW - t, :] = ck_ref[q, t:WINDOW, :]
        nk_ref[q, WINDOW - t:WINDOW, :] = z_ref[tok, OFF_KA:OFF_KA + KV_DIM]
        nv_ref[q, 0:WINDOW - t, :] = cv_ref[q, t:WINDOW, :]
        nv_ref[q, WINDOW - t:WINDOW, :] = z_ref[tok, OFF_VA:OFF_VA + KV_DIM]
    units = []
    for kh in range(N_KV):
        units += [_dec_attention_unit(layer, q, kh, sink_ref, z_ref, ck_ref, cv_ref, biasc_ref, biasn_ref, y_ref)
                  for q in range(DEC_SEQS_PER_STEP)]
    for hh in range(N_REC_HEADS):
        units += [_dec_hgrn2_unit(q, hh, z_ref, s0_ref, lb_ref, oml_ref, recw_ref, y_ref, ns_ref)
                  for q in range(DEC_SEQS_PER_STEP)]
    _emit_staggered(units, lag=DEC_LAG)


def _dec_mix(layer, z, ck, cv, s0, lb, oml, rec_w, sinks, bias_c, bias_n):
    t = DEC_SEQ
    nq = DEC_SEQS_PER_STEP
    const2 = lambda s: (0, 0)
    per_layer = lambda s: (layer, 0, 0)
    kv_in = pl.BlockSpec((None, nq, WINDOW, KV_DIM), lambda s: (layer, s, 0, 0))
    st_in = pl.BlockSpec((None, nq, N_REC_HEADS, REC_HEAD_DIM, REC_HEAD_DIM), lambda s: (layer, s, 0, 0, 0))
    return pl.pallas_call(
        functools.partial(_dec_mix_kernel, layer),
        grid=(DEC_BATCH // nq,),
        in_specs=[
            pl.BlockSpec(memory_space=pltpu.SMEM),
            pl.BlockSpec((nq * t, D_IN), lambda s: (s, 0)),
            kv_in,
            kv_in,
            st_in,
            pl.BlockSpec((None, 1, D_REC), per_layer),
            pl.BlockSpec((None, 1, D_REC), per_layer),
            pl.BlockSpec((None, 1, D_REC), per_layer),
            pl.BlockSpec((N_HEADS * t, WINDOW), const2),
            pl.BlockSpec((N_HEADS * t, t), const2),
        ],
        out_specs=[
            pl.BlockSpec((nq * t, D_MODEL), lambda s: (s, 0)),
            pl.BlockSpec((nq, WINDOW, KV_DIM), lambda s: (s, 0, 0)),
            pl.BlockSpec((nq, WINDOW, KV_DIM), lambda s: (s, 0, 0)),
            pl.BlockSpec((nq, N_REC_HEADS, REC_HEAD_DIM, REC_HEAD_DIM), lambda s: (s, 0, 0, 0)),
        ],
        out_shape=[
            jax.ShapeDtypeStruct((N_DEC, D_MODEL), BF16),
            jax.ShapeDtypeStruct((DEC_BATCH, WINDOW, KV_DIM), F32),
            jax.ShapeDtypeStruct((DEC_BATCH, WINDOW, KV_DIM), F32),
            jax.ShapeDtypeStruct((DEC_BATCH, N_REC_HEADS, REC_HEAD_DIM, REC_HEAD_DIM), F32),
        ],
        compiler_params=pltpu.CompilerParams(dimension_semantics=("arbitrary",)),
        name="dec_mix",
    )(sinks, z, ck, cv, s0, lb, oml, rec_w, bias_c, bias_n)


def _final_norm_kernel(x_ref, w_ref, o_ref):
    o_ref[...] = _rmsnorm_rows(x_ref[...], w_ref[...])


def _final_norm_dec(hs, w):
    return pl.pallas_call(
        _final_norm_kernel,
        out_shape=jax.ShapeDtypeStruct((N_DEC, D_MODEL), F32),
        name="final_norm_dec",
    )(hs, w)


def _t5_bucket(dist):
    max_exact = N_BUCKETS // 2
    d = jnp.maximum(dist, 0)
    df = jnp.maximum(d, 1).astype(F32)
    large = max_exact + (jnp.log(df / max_exact) / math.log(MAX_DISTANCE / max_exact)
                         * (N_BUCKETS - max_exact)).astype(jnp.int32)
    large = jnp.minimum(large, N_BUCKETS - 1)
    return jnp.where(d < max_exact, d, large)


def _bias_tables(rel_bias_table):
    bias_d = rel_bias_table[_t5_bucket(jnp.arange(WINDOW))].T.astype(F32)
    period = 3 * WINDOW + 1
    u = jnp.concatenate([jnp.full((N_HEADS, 1), NEG, F32), bias_d[:, ::-1],
                         jnp.full((N_HEADS, period - 1 - WINDOW), NEG, F32)], axis=1)
    rows = jnp.tile(u, (1, WINDOW))[:, :WINDOW * (period - 1)].reshape(N_HEADS, WINDOW, period - 1)
    prompt = rows[:, :, :2 * WINDOW].reshape(N_HEADS * WINDOW, 2 * WINDOW)
    dec = rows[:, :DEC_SEQ, :WINDOW + DEC_SEQ].reshape(N_HEADS * DEC_SEQ, WINDOW + DEC_SEQ)
    return prompt, dec[:, :WINDOW], dec[:, WINDOW:]


def kernel(x_prompt, x_sample, cache_k, cache_v, state_h, meta_tokens, w_in, w_out, norm_w, final_norm_w,
           attn_sinks, rel_bias_table, hgrn_lb_logits, hgrn_norm_w):
    pl_ = jax.nn.softmax(hgrn_lb_logits.astype(F32), axis=0)
    lb = (jnp.cumsum(pl_, axis=0) - pl_[0:1]).reshape(DEPTH, 1, D_REC)
    oml = 1.0 - lb
    bias_p, bias_dc, bias_dn = _bias_tables(rel_bias_table)
    nw = norm_w.astype(F32).reshape(DEPTH, 1, D_MODEL)
    rw = hgrn_norm_w.astype(F32).reshape(DEPTH, 1, D_REC)
    fw = final_norm_w.astype(F32).reshape(1, D_MODEL)
    sinks = attn_sinks.astype(F32)

    h_meta = meta_tokens.astype(F32)
    h_main = x_prompt
    hs = x_sample.reshape(N_DEC, D_MODEL)
    ck = cache_k.reshape(DEPTH, DEC_BATCH, WINDOW, KV_DIM)
    cv = cache_v.reshape(DEPTH, DEC_BATCH, WINDOW, KV_DIM)

    pk, pv, ps, sk, sv, ss = [], [], [], [], [], []
    for l in range(DEPTH):
        z, w_in_b = _dec_proj(l, hs, nw, w_in)
        y, nk, nv, ns = _dec_mix(l, z, ck, cv, state_h, lb, oml, rw, sinks, bias_dc, bias_dn)
        hs, w_out_b = _dec_out(l, hs, y, w_out)
        sk.append(nk)
        sv.append(nv)
        ss.append(ns)

        outs = _prompt_layer(l, h_meta, h_main, w_in_b, w_out_b, nw, lb, oml, rw, sinks, bias_p, fw)
        if l < DEPTH - 1:
            h_meta, h_main, k_l, v_l, s_l = outs
        else:
            h_main, k_l, v_l, s_l = outs
        pk.append(k_l)
        pv.append(v_l)
        ps.append(s_l)

    y_prompt = h_main
    y_sample = _final_norm_dec(hs, fw).reshape(DEC_BATCH, DEC_SEQ, D_MODEL)
    kv_shape = (DEPTH, -1, WINDOW, N_KV, HEAD_DIM)
    return (y_prompt, y_sample,
            jnp.stack(pk).reshape(kv_shape), jnp.stack(pv).reshape(kv_shape),
            jnp.swapaxes(jnp.stack(ps), -1, -2),
            jnp.stack(sk).reshape(kv_shape), jnp.stack(sv).reshape(kv_shape), jnp.stack(ss))
```

```python
import functools
import math

import jax
import jax.numpy as jnp
from jax import lax
from jax.experimental import pallas as pl
from jax.experimental.pallas import tpu as pltpu

D_MODEL = 2048
BATCH = 4
SEQ = 2048
DEPTH = 4
DEC_BATCH = 32
DEC_SEQ = 8
N_META = 16
D_ATTN = 1024
D_REC = 1024
HEAD_DIM = 64
N_HEADS = 16
N_KV = 4
GROUP = 4
KV_DIM = 256
WINDOW = 128
N_BUCKETS = 32
MAX_DISTANCE = 128
REC_HEAD_DIM = 128
N_REC_HEADS = 8
EPS = 1e-6
D_IN = 2 * D_ATTN + 2 * KV_DIM + 4 * D_REC

OFF_QA = 0
OFF_KA = D_ATTN
OFF_VA = OFF_KA + KV_DIM
OFF_GA = OFF_VA + KV_DIM
OFF_QR = OFF_GA + D_ATTN
OFF_FR = OFF_QR + D_REC
OFF_IR = OFF_FR + D_REC
OFF_GR = OFF_IR + D_REC
ATT_COLS = OFF_QR

TILE = 256
QBLK = WINDOW
CHUNK = 64
HALF = CHUNK // 2
ROW_TILES = SEQ // TILE
PAD = TILE - N_META
PROJ_COLS = 512
REC_GROUP = PROJ_COLS // REC_HEAD_DIM
NEG = -1e30
MAX_FACTOR_EXPONENT = 80.0
N_DEC = DEC_BATCH * DEC_SEQ
V7X_VMEM_BYTES = 64 * 1024 * 1024
VMEM_LIMIT = V7X_VMEM_BYTES - 1024 * 1024

F32 = jnp.float32
BF16 = jnp.bfloat16


def _sigmoid_pair(x):
    e = jnp.exp(-jnp.abs(x))
    inv = 1.0 / (1.0 + e)
    pos = x >= 0
    return jnp.where(pos, inv, e * inv), jnp.where(pos, e * inv, inv)


def _silu(x):
    return x * (0.5 * jnp.tanh(0.5 * x) + 0.5)


def _cumsum_rows(x):
    n = x.shape[0]
    row = lax.broadcasted_iota(jnp.int32, x.shape, 0)
    k = 1
    while k < n:
        x = x + jnp.where(row >= k, pltpu.roll(x, k, axis=0), 0.0)
        k *= 2
    return x


def _rmsnorm_rows(x, w):
    ms = jnp.mean(x * x, axis=-1, keepdims=True)
    return (x * lax.rsqrt(ms + EPS)) * w


def _dot_nt(a, b):
    return lax.dot_general(a, b, (((1,), (1,)), ((), ())), preferred_element_type=F32)


def _dot_tn(a, b):
    return lax.dot_general(a, b, (((0,), (0,)), ((), ())), preferred_element_type=F32)


def _dot(a, b):
    return jnp.dot(a, b, preferred_element_type=F32)


def _gate_features(fr, lb, oml):
    sig, sneg = _sigmoid_pair(fr)
    logf = jnp.log(lb + oml * sig)
    return logf, oml * sneg


def _attention_unit(layer, qb, kh, za_ref, kv_ref, y_ref, bias_ref, sink_ref, c_tile):
    r0 = qb * QBLK
    kpos = c_tile + r0 - QBLK + lax.broadcasted_iota(jnp.int32, (1, 2 * QBLK), 1)
    kvalid = kpos >= 0
    kk = kv_ref[r0:r0 + 2 * QBLK, kh * HEAD_DIM:(kh + 1) * HEAD_DIM]
    q4 = jnp.concatenate(
        [za_ref[r0:r0 + QBLK, (kh * GROUP + g) * HEAD_DIM:(kh * GROUP + g + 1) * HEAD_DIM]
         for g in range(GROUP)], axis=0)
    s = _dot_nt((q4 * (HEAD_DIM ** -0.5)).astype(BF16), kk)
    yield
    s = s + bias_ref[kh * GROUP * QBLK:(kh + 1) * GROUP * QBLK, :].astype(F32)
    s = jnp.where(kvalid, s, NEG)
    es, dens = [], []
    for g in range(GROUP):
        sink = sink_ref[layer, kh * GROUP + g]
        sg = s[g * QBLK:(g + 1) * QBLK]
        m = jnp.maximum(jnp.max(sg, axis=-1, keepdims=True), sink)
        e = jnp.exp(sg - m)
        dens.append(jnp.sum(e, axis=-1, keepdims=True) + jnp.exp(sink - m))
        es.append(e.astype(BF16))
    vv = kv_ref[r0:r0 + 2 * QBLK, KV_DIM + kh * HEAD_DIM:KV_DIM + (kh + 1) * HEAD_DIM]
    o4 = _dot(jnp.concatenate(es, axis=0), vv)
    yield
    outs = []
    for g in range(GROUP):
        h = kh * GROUP + g
        og = o4[g * QBLK:(g + 1) * QBLK] / dens[g]
        ga = za_ref[r0:r0 + QBLK, OFF_GA + h * HEAD_DIM:OFF_GA + (h + 1) * HEAD_DIM]
        outs.append(og * _silu(ga))
    y_ref[r0:r0 + QBLK, kh * GROUP * HEAD_DIM:(kh + 1) * GROUP * HEAD_DIM] = (
        jnp.concatenate(outs, axis=-1).astype(BF16))


def _decay_scores_direct(q, kr, a):
    n = q.shape[0]
    row = lax.broadcasted_iota(jnp.int32, (n, n), 0)
    col = lax.broadcasted_iota(jnp.int32, (n, n), 1)

    def diagonal(d, acc):
        ks = pltpu.roll(kr, d, axis=0)
        a_s = pltpu.roll(a, d, axis=0)
        p = q * ks * jnp.exp(jnp.minimum(a - a_s, 0.0))
        return jnp.where(col == row - d, jnp.sum(p, axis=-1, keepdims=True), acc)

    acc = jnp.zeros((n, n), F32)
    if n <= DEC_SEQ:
        for d in range(n):
            acc = diagonal(d, acc)
        return acc
    return lax.fori_loop(0, n, diagonal, acc)


def _hgrn2_unit(ci, hh, zr_ref, st_ref, y_ref, lb_ref, oml_ref, recw_ref, c_tile, spans=None, heads_in_buf=None):
    heads_in_buf = REC_GROUP if heads_in_buf is None else heads_in_buf
    part = heads_in_buf * REC_HEAD_DIM
    row = lax.broadcasted_iota(jnp.int32, (CHUNK, CHUNK), 0)
    col = lax.broadcasted_iota(jnp.int32, (CHUNK, CHUNK), 1)
    causal = col <= row
    r0 = ci * CHUNK if isinstance(ci, int) else pl.multiple_of(ci * CHUNK, CHUNK)
    rows = pl.ds(r0, CHUNK)
    rpos = c_tile + r0 + lax.broadcasted_iota(jnp.int32, (CHUNK, 1), 0)
    rvalid = rpos >= 0
    cs = hh * REC_HEAD_DIM
    zc = (hh % heads_in_buf) * REC_HEAD_DIM
    lb = lb_ref[:, cs:cs + REC_HEAD_DIM]
    oml = oml_ref[:, cs:cs + REC_HEAD_DIM]
    logf, kr = _gate_features(zr_ref[rows, part + zc:part + zc + REC_HEAD_DIM], lb, oml)
    q = _silu(zr_ref[rows, zc:zc + REC_HEAD_DIM])
    v = jnp.where(rvalid, zr_ref[rows, 2 * part + zc:2 * part + zc + REC_HEAD_DIM], 0.0)
    vb = v.astype(BF16)
    a = _cumsum_rows(logf)
    a_mid = a[HALF - 1:HALF, :]
    a_end = a[CHUNK - 1:CHUNK, :]
    if spans is None:
        scores = _decay_scores_direct(q, kr, a)
    else:
        qh = (q * jnp.exp(a - a_mid)).astype(BF16)
        kh_ = (kr * jnp.exp(a_mid - a)).astype(BF16)
        scores = _dot_nt(qh, kh_)
        spans.append(jnp.maximum(a[0:1, :] - a_mid, a_mid - a_end))
    qt = (q * jnp.exp(a)).astype(BF16)
    kt = (kr * jnp.exp(a_end - a)).astype(BF16)
    st = st_ref[hh]
    o_state = _dot_nt(qt, st.astype(BF16))
    yield
    att = jnp.where(causal, scores, 0.0)
    o = _dot(att.astype(BF16), vb) + o_state
    st_ref[hh] = st * jnp.exp(a_end) + _dot_tn(vb, kt)
    yield
    on = o * lax.rsqrt(jnp.mean(o * o, axis=-1, keepdims=True) + EPS)
    gr = zr_ref[rows, 3 * part + zc:3 * part + zc + REC_HEAD_DIM]
    yr = (on * recw_ref[:, cs:cs + REC_HEAD_DIM]) * _silu(gr)
    y_ref[rows, D_ATTN + cs:D_ATTN + cs + REC_HEAD_DIM] = yr.astype(BF16)


def _emit_staggered(units, fillers=(), lag=1):
    n = len(units)
    rounds = n + 2 * lag
    at_round = [[] for _ in range(rounds)]
    for k, f in enumerate(fillers):
        at_round[k * rounds // len(fillers)].append(f)
    for i in range(rounds):
        if i < n:
            next(units[i])
        for f in at_round[i]:
            f()
        if 0 <= i - lag < n:
            next(units[i - lag])
        if 0 <= i - 2 * lag < n:
            next(units[i - 2 * lag], None)


def _main_tile(s):
    t = jnp.maximum(s - 1, 0)
    return lax.div(t, ROW_TILES), lax.rem(t, ROW_TILES)


def _prompt_layer_kernel(layer, final, sink_ref, meta_ref, hin_ref, win_ref, wout_ref, normw_ref, lb_ref, oml_ref,
                         recw_ref, bias_ref, finw_ref, *refs):
    if final:
        hout_ref, pk_ref, pv_ref, ps_ref = refs[:4]
        refs = refs[4:]
    else:
        meta_out_ref, hout_ref, pk_ref, pv_ref, ps_ref = refs[:5]
        refs = refs[5:]
    xn_ref, za_ref, zr_ref, kv_ref, st_ref, y_ref, kv_meta_ref, st_meta_ref, st_start_ref, zhead_ref = refs
    s = pl.program_id(0)
    is_meta = s == 0
    _, j = _main_tile(s)

    @pl.when(is_meta)
    def _():
        kv_ref[0:QBLK, :] = jnp.zeros((QBLK, 2 * KV_DIM), BF16)
        st_ref[...] = jnp.zeros(st_ref.shape, F32)

    @pl.when(jnp.logical_and(jnp.logical_not(is_meta), j == 0))
    def _():
        kv_ref[0:QBLK, :] = kv_meta_ref[...]
        st_ref[...] = st_meta_ref[...]

    st_start_ref[...] = st_ref[...]

    def tile_input(cols=slice(None)):
        meta = meta_ref[:, cols]
        meta_rows = jnp.concatenate([jnp.zeros((PAD, meta.shape[1]), F32), meta], axis=0)
        return jnp.where(is_meta, meta_rows, hin_ref[0, :, cols])

    c_tile = jnp.where(is_meta, -PAD, j * TILE + N_META)
    xn_ref[...] = _rmsnorm_rows(tile_input(), normw_ref[...]).astype(BF16)

    def project(dst_ref, dst_col, src_col):
        dst_ref[:, dst_col:dst_col + PROJ_COLS] = _dot(xn_ref[...], win_ref[:, src_col:src_col + PROJ_COLS])

    def project_rec(dst_ref, group):
        return [functools.partial(project, dst_ref, part * PROJ_COLS, off + group * PROJ_COLS)
                for part, off in enumerate((OFF_QR, OFF_FR, OFF_IR, OFF_GR))]

    def out_proj(k0, k1, c0, first_half):
        cols = slice(c0, c0 + PROJ_COLS)
        base = tile_input(cols) if first_half else hout_ref[0, :, cols]
        hout_ref[0, :, cols] = base + _dot(y_ref[:, k0:k1], wout_ref[k0:k1, cols])

    for c0 in range(0, ATT_COLS, PROJ_COLS):
        project(za_ref, c0, c0)
    kv_ref[QBLK:QBLK + TILE, :] = za_ref[:, OFF_KA:OFF_KA + 2 * KV_DIM].astype(BF16)
    pk_ref[0] = za_ref[TILE - WINDOW:TILE, OFF_KA:OFF_KA + KV_DIM]
    pv_ref[0] = za_ref[TILE - WINDOW:TILE, OFF_VA:OFF_VA + KV_DIM]

    att_units = [_attention_unit(layer, qb, kh, za_ref, kv_ref, y_ref, bias_ref, sink_ref, c_tile)
                 for qb in range(TILE // QBLK) for kh in range(N_KV)]
    _emit_staggered(att_units, project_rec(zr_ref, 0))
    kv_ref[0:QBLK, :] = kv_ref[TILE:TILE + QBLK, :]

    n_groups = N_REC_HEADS // REC_GROUP
    z_bufs = [zr_ref, za_ref]
    spans = []

    def rec_units(grp, state_ref, spans_):
        return [_hgrn2_unit(ci, grp * REC_GROUP + hl, z_bufs[grp % 2], state_ref, y_ref, lb_ref, oml_ref,
                            recw_ref, c_tile, spans_)
                for ci in range(TILE // CHUNK) for hl in range(REC_GROUP)]

    for grp in range(n_groups):
        if grp + 1 < n_groups:
            fillers = project_rec(z_bufs[(grp + 1) % 2], grp + 1)
        else:
            fillers = [functools.partial(out_proj, 0, D_ATTN, c0, True) for c0 in range(0, D_MODEL, PROJ_COLS)]
        _emit_staggered(rec_units(grp, st_ref, spans), fillers)
    for c0 in range(0, D_MODEL, PROJ_COLS):
        out_proj(D_ATTN, D_MODEL, c0, False)

    widest = jnp.max(jnp.concatenate(spans, axis=0))

    @pl.when(widest > MAX_FACTOR_EXPONENT)
    def _():
        def replay_chunk(hh, ci, carry):
            for _ in _hgrn2_unit(ci, hh, zhead_ref, st_start_ref, y_ref, lb_ref, oml_ref, recw_ref, c_tile,
                                 spans=None, heads_in_buf=1):
                pass
            return carry

        for hh in range(N_REC_HEADS):
            for part, off in enumerate((OFF_QR, OFF_FR, OFF_IR, OFF_GR)):
                src = off + hh * REC_HEAD_DIM
                zhead_ref[:, part * REC_HEAD_DIM:(part + 1) * REC_HEAD_DIM] = _dot(
                    xn_ref[...], win_ref[:, src:src + REC_HEAD_DIM])
            lax.fori_loop(0, TILE // CHUNK, functools.partial(replay_chunk, hh), 0)
        for c0 in range(0, D_MODEL, PROJ_COLS):
            cols = slice(c0, c0 + PROJ_COLS)
            hout_ref[0, :, cols] = tile_input(cols) + _dot(y_ref[...], wout_ref[:, cols])

    ps_ref[0] = st_ref[...]

    @pl.when(is_meta)
    def _():
        kv_meta_ref[...] = kv_ref[0:QBLK, :]
        st_meta_ref[...] = st_ref[...]
        if not final:
            meta_out_ref[...] = hout_ref[0, PAD:TILE, :]

    if final:
        @pl.when(jnp.logical_not(is_meta))
        def _():
            hout_ref[0] = _rmsnorm_rows(hout_ref[0], finw_ref[...])


def _layer_block(shape, layer):
    nd = len(shape)
    return pl.BlockSpec((None,) + tuple(shape), lambda s: (layer,) + (0,) * nd, pipeline_mode=pl.Buffered(1))


def _const_block(shape):
    nd = len(shape)
    return pl.BlockSpec(tuple(shape), lambda s: (0,) * nd, pipeline_mode=pl.Buffered(1))


def _prompt_layer(layer, h_meta, h_main, w_in, w_out, norm_w, lb, oml, rec_w, sinks, bias, fin_w):
    final = layer == DEPTH - 1
    main = pl.BlockSpec((1, TILE, D_MODEL), lambda s: _main_tile(s) + (0,))
    meta = pl.BlockSpec((N_META, D_MODEL), lambda s: (0, 0))
    kv_spec = pl.BlockSpec((1, WINDOW, KV_DIM), lambda s: (_main_tile(s)[0], 0, 0))
    st_spec = pl.BlockSpec((1, N_REC_HEADS, REC_HEAD_DIM, REC_HEAD_DIM), lambda s: (_main_tile(s)[0], 0, 0, 0))
    main_shape = jax.ShapeDtypeStruct((BATCH, SEQ, D_MODEL), F32)
    meta_shape = jax.ShapeDtypeStruct((N_META, D_MODEL), F32)
    return pl.pallas_call(
        functools.partial(_prompt_layer_kernel, layer, final),
        grid=(1 + BATCH * ROW_TILES,),
        in_specs=[
            pl.BlockSpec(memory_space=pltpu.SMEM),
            _const_block((N_META, D_MODEL)),
            main,
            _const_block((D_MODEL, D_IN)),
            _const_block((D_MODEL, D_MODEL)),
            _layer_block((1, D_MODEL), layer),
            _layer_block((1, D_REC), layer),
            _layer_block((1, D_REC), layer),
            _layer_block((1, D_REC), layer),
            _const_block((N_HEADS * QBLK, 2 * QBLK)),
            _const_block((1, D_MODEL)),
        ],
        out_specs=([] if final else [meta]) + [main, kv_spec, kv_spec, st_spec],
        out_shape=([] if final else [meta_shape]) + [
            main_shape,
            jax.ShapeDtypeStruct((BATCH, WINDOW, KV_DIM), F32),
            jax.ShapeDtypeStruct((BATCH, WINDOW, KV_DIM), F32),
            jax.ShapeDtypeStruct((BATCH, N_REC_HEADS, REC_HEAD_DIM, REC_HEAD_DIM), F32),
        ],
        scratch_shapes=[
            pltpu.VMEM((TILE, D_MODEL), BF16),
            pltpu.VMEM((TILE, ATT_COLS), F32),
            pltpu.VMEM((TILE, 4 * PROJ_COLS), F32),
            pltpu.VMEM((QBLK + TILE, 2 * KV_DIM), BF16),
            pltpu.VMEM((N_REC_HEADS, REC_HEAD_DIM, REC_HEAD_DIM), F32),
            pltpu.VMEM((TILE, D_MODEL), BF16),
            pltpu.VMEM((QBLK, 2 * KV_DIM), BF16),
            pltpu.VMEM((N_REC_HEADS, REC_HEAD_DIM, REC_HEAD_DIM), F32),
            pltpu.VMEM((N_REC_HEADS, REC_HEAD_DIM, REC_HEAD_DIM), F32),
            pltpu.VMEM((TILE, 4 * REC_HEAD_DIM), F32),
        ],
        compiler_params=pltpu.CompilerParams(
            dimension_semantics=("arbitrary",), vmem_limit_bytes=VMEM_LIMIT),
        name="prompt_layer",
    )(sinks, h_meta, h_main, w_in, w_out, norm_w, lb, oml, rec_w, bias, fin_w)


DEC_PROJ_COLS = D_IN // 4
DEC_OUT_COLS = D_MODEL // 2
DEC_VMEM_LIMIT = 56 * 1024 * 1024


def _dec_proj_kernel(hs_ref, normw_ref, win_ref, z_ref, wb_ref):
    xn = _rmsnorm_rows(hs_ref[...], normw_ref[...]).astype(BF16)
    for c0 in range(0, DEC_PROJ_COLS, REC_HEAD_DIM):
        cols = slice(c0, c0 + REC_HEAD_DIM)
        wb = win_ref[:, cols].astype(BF16)
        wb_ref[:, cols] = wb
        z_ref[:, cols] = _dot(xn, wb)


def _dec_proj(layer, hs, norm_w, w_in):
    return pl.pallas_call(
        _dec_proj_kernel,
        grid=(D_IN // DEC_PROJ_COLS,),
        in_specs=[
            pl.BlockSpec((N_DEC, D_MODEL), lambda n: (0, 0)),
            pl.BlockSpec((None, 1, D_MODEL), lambda n: (layer, 0, 0)),
            pl.BlockSpec((None, D_MODEL, DEC_PROJ_COLS), lambda n: (layer, 0, n)),
        ],
        out_specs=[pl.BlockSpec((N_DEC, DEC_PROJ_COLS), lambda n: (0, n)),
                   pl.BlockSpec((D_MODEL, DEC_PROJ_COLS), lambda n: (0, n))],
        out_shape=[jax.ShapeDtypeStruct((N_DEC, D_IN), F32),
                   jax.ShapeDtypeStruct((D_MODEL, D_IN), BF16)],
        compiler_params=pltpu.CompilerParams(dimension_semantics=("arbitrary",), vmem_limit_bytes=DEC_VMEM_LIMIT),
        name="dec_proj",
    )(hs, norm_w, w_in)


def _dec_out_kernel(hs_ref, y_ref, wout_ref, o_ref, wb_ref):
    for c0 in range(0, DEC_OUT_COLS, 2 * REC_HEAD_DIM):
        cols = slice(c0, c0 + 2 * REC_HEAD_DIM)
        wb = wout_ref[:, cols].astype(BF16)
        wb_ref[:, cols] = wb
        o_ref[:, cols] = hs_ref[:, cols] + _dot(y_ref[...], wb)


def _dec_out(layer, hs, y, w_out):
    return pl.pallas_call(
        _dec_out_kernel,
        grid=(D_MODEL // DEC_OUT_COLS,),
        in_specs=[
            pl.BlockSpec((N_DEC, DEC_OUT_COLS), lambda n: (0, n)),
            pl.BlockSpec((N_DEC, D_MODEL), lambda n: (0, 0)),
            pl.BlockSpec((None, D_MODEL, DEC_OUT_COLS), lambda n: (layer, 0, n)),
        ],
        out_specs=[pl.BlockSpec((N_DEC, DEC_OUT_COLS), lambda n: (0, n)),
                   pl.BlockSpec((D_MODEL, DEC_OUT_COLS), lambda n: (0, n))],
        out_shape=[jax.ShapeDtypeStruct((N_DEC, D_MODEL), F32),
                   jax.ShapeDtypeStruct((D_MODEL, D_MODEL), BF16)],
        compiler_params=pltpu.CompilerParams(dimension_semantics=("arbitrary",), vmem_limit_bytes=DEC_VMEM_LIMIT),
        name="dec_out",
    )(hs, y, w_out)


DEC_SEQS_PER_STEP = 4
DEC_LAG = 8


def _dec_attention_unit(layer, q, kh, sink_ref, z_ref, ck_ref, cv_ref, biasc_ref, biasn_ref, y_ref):
    t = DEC_SEQ
    tok = slice(q * t, (q + 1) * t)
    hs_ = slice(kh * HEAD_DIM, (kh + 1) * HEAD_DIM)
    kc = ck_ref[q, :, hs_].astype(BF16)
    kn = z_ref[tok, OFF_KA + kh * HEAD_DIM:OFF_KA + (kh + 1) * HEAD_DIM]
    q4 = jnp.concatenate(
        [z_ref[tok, (kh * GROUP + g) * HEAD_DIM:(kh * GROUP + g + 1) * HEAD_DIM] for g in range(GROUP)],
        axis=0) * (HEAD_DIM ** -0.5)
    rows = slice(kh * GROUP * t, (kh + 1) * GROUP * t)
    sc = _dot_nt(q4.astype(BF16), kc)
    sn = _dot_nt(q4, kn)
    yield
    sc = sc + biasc_ref[rows, :]
    sn = sn + biasn_ref[rows, :]
    ecs, ens, dens = [], [], []
    for g in range(GROUP):
        sink = sink_ref[layer, kh * GROUP + g]
        scg = sc[g * t:(g + 1) * t]
        sng = sn[g * t:(g + 1) * t]
        m = jnp.maximum(jnp.maximum(jnp.max(scg, axis=-1, keepdims=True),
                                    jnp.max(sng, axis=-1, keepdims=True)), sink)
        ec = jnp.exp(scg - m)
        en = jnp.exp(sng - m)
        dens.append(jnp.sum(ec, axis=-1, keepdims=True) + jnp.sum(en, axis=-1, keepdims=True)
                    + jnp.exp(sink - m))
        ecs.append(ec)
        ens.append(en)
    vc = cv_ref[q, :, hs_]
    vn = z_ref[tok, OFF_VA + kh * HEAD_DIM:OFF_VA + (kh + 1) * HEAD_DIM]
    o4 = (_dot(jnp.concatenate(ecs, axis=0).astype(BF16), vc.astype(BF16))
          + _dot(jnp.concatenate(ens, axis=0), vn))
    yield
    outs = []
    for g in range(GROUP):
        h = kh * GROUP + g
        og = o4[g * t:(g + 1) * t] / dens[g]
        ga = z_ref[tok, OFF_GA + h * HEAD_DIM:OFF_GA + (h + 1) * HEAD_DIM]
        outs.append(og * _silu(ga))
    y_ref[tok, kh * GROUP * HEAD_DIM:(kh + 1) * GROUP * HEAD_DIM] = jnp.concatenate(outs, axis=-1).astype(BF16)


def _dec_hgrn2_unit(q, hh, z_ref, s0_ref, lb_ref, oml_ref, recw_ref, y_ref, ns_ref):
    t = DEC_SEQ
    tok = slice(q * t, (q + 1) * t)
    row = lax.broadcasted_iota(jnp.int32, (t, t), 0)
    col = lax.broadcasted_iota(jnp.int32, (t, t), 1)
    causal = col <= row
    cs = hh * REC_HEAD_DIM
    lb = lb_ref[:, cs:cs + REC_HEAD_DIM]
    oml = oml_ref[:, cs:cs + REC_HEAD_DIM]
    logf, kr = _gate_features(z_ref[tok, OFF_FR + cs:OFF_FR + cs + REC_HEAD_DIM], lb, oml)
    qv = _silu(z_ref[tok, OFF_QR + cs:OFF_QR + cs + REC_HEAD_DIM])
    v = z_ref[tok, OFF_IR + cs:OFF_IR + cs + REC_HEAD_DIM]
    a = _cumsum_rows(logf)
    a_end = a[t - 1:t, :]
    scores = _decay_scores_direct(qv, kr, a)
    qt = qv * jnp.exp(a)
    kt = kr * jnp.exp(a_end - a)
    s0 = s0_ref[q, hh]
    o_state = _dot(qt, s0)
    yield
    att = jnp.where(causal, scores, 0.0)
    o = _dot(att, v) + o_state
    ktd = jnp.concatenate([kt, jnp.exp(a_end), jnp.zeros((t - 1, REC_HEAD_DIM), F32)], axis=0).T
    ns_ref[q, hh] = s0 * ktd[:, t:t + 1] + _dot(ktd[:, 0:t], v)
    yield
    on = o * lax.rsqrt(jnp.mean(o * o, axis=-1, keepdims=True) + EPS)
    gr = z_ref[tok, OFF_GR + cs:OFF_GR + cs + REC_HEAD_DIM]
    yr = (on * recw_ref[:, cs:cs + REC_HEAD_DIM]) * _silu(gr)
    y_ref[tok, D_ATTN + cs:D_ATTN + cs + REC_HEAD_DIM] = yr.astype(BF16)


def _dec_mix_kernel(layer, sink_ref, z_ref, ck_ref, cv_ref, s0_ref, lb_ref, oml_ref, recw_ref, biasc_ref,
                    biasn_ref, y_ref, nk_ref, nv_ref, ns_ref):
    t = DEC_SEQ
    for q in range(DEC_SEQS_PER_STEP):
        tok = slice(q * t, (q + 1) * t)
        nk_ref[q, 0:WINDOW - t, :] = ck_ref[q, t:WINDOW, :]
        nk_ref[q, WINDOW - t:WINDOW, :] = z_ref[tok, OFF_KA:OFF_KA + KV_DIM]
        nv_ref[q, 0:WINDOW - t, :] = cv_ref[q, t:WINDOW, :]
        nv_ref[q, WINDOW - t:WINDOW, :] = z_ref[tok, OFF_VA:OFF_VA + KV_DIM]
    units = []
    for kh in range(N_KV):
        units += [_dec_attention_unit(layer, q, kh, sink_ref, z_ref, ck_ref, cv_ref, biasc_ref, biasn_ref, y_ref)
                  for q in range(DEC_SEQS_PER_STEP)]
    for hh in range(N_REC_HEADS):
        units += [_dec_hgrn2_unit(q, hh, z_ref, s0_ref, lb_ref, oml_ref, recw_ref, y_ref, ns_ref)
                  for q in range(DEC_SEQS_PER_STEP)]
    _emit_staggered(units, lag=DEC_LAG)


def _dec_mix(layer, z, ck, cv, s0, lb, oml, rec_w, sinks, bias_c, bias_n):
    t = DEC_SEQ
    nq = DEC_SEQS_PER_STEP
    const2 = lambda s: (0, 0)
    per_layer = lambda s: (layer, 0, 0)
    kv_in = pl.BlockSpec((None, nq, WINDOW, KV_DIM), lambda s: (layer, s, 0, 0))
    st_in = pl.BlockSpec((None, nq, N_REC_HEADS, REC_HEAD_DIM, REC_HEAD_DIM), lambda s: (layer, s, 0, 0, 0))
    return pl.pallas_call(
        functools.partial(_dec_mix_kernel, layer),
        grid=(DEC_BATCH // nq,),
        in_specs=[
            pl.BlockSpec(memory_space=pltpu.SMEM),
            pl.BlockSpec((nq * t, D_IN), lambda s: (s, 0)),
            kv_in,
            kv_in,
            st_in,
            pl.BlockSpec((None, 1, D_REC), per_layer),
            pl.BlockSpec((None, 1, D_REC), per_layer),
            pl.BlockSpec((None, 1, D_REC), per_layer),
            pl.BlockSpec((N_HEADS * t, WINDOW), const2),
            pl.BlockSpec((N_HEADS * t, t), const2),
        ],
        out_specs=[
            pl.BlockSpec((nq * t, D_MODEL), lambda s: (s, 0)),
            pl.BlockSpec((nq, WINDOW, KV_DIM), lambda s: (s, 0, 0)),
            pl.BlockSpec((nq, WINDOW, KV_DIM), lambda s: (s, 0, 0)),
            pl.BlockSpec((nq, N_REC_HEADS, REC_HEAD_DIM, REC_HEAD_DIM), lambda s: (s, 0, 0, 0)),
        ],
        out_shape=[
            jax.ShapeDtypeStruct((N_DEC, D_MODEL), BF16),
            jax.ShapeDtypeStruct((DEC_BATCH, WINDOW, KV_DIM), F32),
            jax.ShapeDtypeStruct((DEC_BATCH, WINDOW, KV_DIM), F32),
            jax.ShapeDtypeStruct((DEC_BATCH, N_REC_HEADS, REC_HEAD_DIM, REC_HEAD_DIM), F32),
        ],
        compiler_params=pltpu.CompilerParams(dimension_semantics=("arbitrary",)),
        name="dec_mix",
    )(sinks, z, ck, cv, s0, lb, oml, rec_w, bias_c, bias_n)


def _final_norm_kernel(x_ref, w_ref, o_ref):
    o_ref[...] = _rmsnorm_rows(x_ref[...], w_ref[...])


def _final_norm_dec(hs, w):
    return pl.pallas_call(
        _final_norm_kernel,
        out_shape=jax.ShapeDtypeStruct((N_DEC, D_MODEL), F32),
        name="final_norm_dec",
    )(hs, w)


def _t5_bucket(dist):
    max_exact = N_BUCKETS // 2
    d = jnp.maximum(dist, 0)
    df = jnp.maximum(d, 1).astype(F32)
    large = max_exact + (jnp.log(df / max_exact) / math.log(MAX_DISTANCE / max_exact)
                         * (N_BUCKETS - max_exact)).astype(jnp.int32)
    large = jnp.minimum(large, N_BUCKETS - 1)
    return jnp.where(d < max_exact, d, large)


def _bias_tables(rel_bias_table):
    bias_d = rel_bias_table[_t5_bucket(jnp.arange(WINDOW))].T.astype(F32)
    period = 3 * WINDOW + 1
    u = jnp.concatenate([jnp.full((N_HEADS, 1), NEG, F32), bias_d[:, ::-1],
                         jnp.full((N_HEADS, period - 1 - WINDOW), NEG, F32)], axis=1)
    rows = jnp.tile(u, (1, WINDOW))[:, :WINDOW * (period - 1)].reshape(N_HEADS, WINDOW, period - 1)
    prompt = rows[:, :, :2 * WINDOW].reshape(N_HEADS * WINDOW, 2 * WINDOW).astype(BF16)
    dec = rows[:, :DEC_SEQ, :WINDOW + DEC_SEQ].reshape(N_HEADS * DEC_SEQ, WINDOW + DEC_SEQ)
    return prompt, dec[:, :WINDOW], dec[:, WINDOW:]


def kernel(x_prompt, x_sample, cache_k, cache_v, state_h, meta_tokens, w_in, w_out, norm_w, final_norm_w,
           attn_sinks, rel_bias_table, hgrn_lb_logits, hgrn_norm_w):
    pl_ = jax.nn.softmax(hgrn_lb_logits.astype(F32), axis=0)
    lb = (jnp.cumsum(pl_, axis=0) - pl_[0:1]).reshape(DEPTH, 1, D_REC)
    oml = 1.0 - lb
    bias_p, bias_dc, bias_dn = _bias_tables(rel_bias_table)
    nw = norm_w.astype(F32).reshape(DEPTH, 1, D_MODEL)
    rw = hgrn_norm_w.astype(F32).reshape(DEPTH, 1, D_REC)
    fw = final_norm_w.astype(F32).reshape(1, D_MODEL)
    sinks = attn_sinks.astype(F32)

    h_meta = meta_tokens.astype(F32)
    h_main = x_prompt
    hs = x_sample.reshape(N_DEC, D_MODEL)
    ck = cache_k.reshape(DEPTH, DEC_BATCH, WINDOW, KV_DIM)
    cv = cache_v.reshape(DEPTH, DEC_BATCH, WINDOW, KV_DIM)

    pk, pv, ps, sk, sv, ss = [], [], [], [], [], []
    for l in range(DEPTH):
        z, w_in_b = _dec_proj(l, hs, nw, w_in)
        y, nk, nv, ns = _dec_mix(l, z, ck, cv, state_h, lb, oml, rw, sinks, bias_dc, bias_dn)
        hs, w_out_b = _dec_out(l, hs, y, w_out)
        sk.append(nk)
        sv.append(nv)
        ss.append(ns)

        outs = _prompt_layer(l, h_meta, h_main, w_in_b, w_out_b, nw, lb, oml, rw, sinks, bias_p, fw)
        if l < DEPTH - 1:
            h_meta, h_main, k_l, v_l, s_l = outs
        else:
            h_main, k_l, v_l, s_l = outs
        pk.append(k_l)
        pv.append(v_l)
        ps.append(s_l)

    y_prompt = h_main
    y_sample = _final_norm_dec(hs, fw).reshape(DEC_BATCH, DEC_SEQ, D_MODEL)
    kv_shape = (DEPTH, -1, WINDOW, N_KV, HEAD_DIM)
    return (y_prompt, y_sample,
            jnp.stack(pk).reshape(kv_shape), jnp.stack(pv).reshape(kv_shape),
            jnp.swapaxes(jnp.stack(ps), -1, -2),
            jnp.stack(sk).reshape(kv_shape), jnp.stack(sv).reshape(kv_shape), jnp.stack(ss))
```

```python
import functools
import math

import jax
import jax.numpy as jnp
from jax import lax
from jax.experimental import pallas as pl
from jax.experimental.pallas import tpu as pltpu

D_MODEL = 2048
BATCH = 4
SEQ = 2048
DEPTH = 4
DEC_BATCH = 32
DEC_SEQ = 8
N_META = 16
D_ATTN = 1024
D_REC = 1024
HEAD_DIM = 64
N_HEADS = 16
N_KV = 4
GROUP = 4
KV_DIM = 256
WINDOW = 128
N_BUCKETS = 32
MAX_DISTANCE = 128
REC_HEAD_DIM = 128
N_REC_HEADS = 8
EPS = 1e-6
D_IN = 2 * D_ATTN + 2 * KV_DIM + 4 * D_REC

OFF_QA = 0
OFF_KA = D_ATTN
OFF_VA = OFF_KA + KV_DIM
OFF_GA = OFF_VA + KV_DIM
OFF_QR = OFF_GA + D_ATTN
OFF_FR = OFF_QR + D_REC
OFF_IR = OFF_FR + D_REC
OFF_GR = OFF_IR + D_REC
ATT_COLS = OFF_QR

TILE = 256
QBLK = WINDOW
CHUNK = 64
HALF = CHUNK // 2
ROW_TILES = SEQ // TILE
PAD = TILE - N_META
PROJ_COLS = 512
REC_GROUP = PROJ_COLS // REC_HEAD_DIM
NEG = -1e30
MAX_FACTOR_EXPONENT = 80.0
N_DEC = DEC_BATCH * DEC_SEQ
V7X_VMEM_BYTES = 64 * 1024 * 1024
VMEM_LIMIT = V7X_VMEM_BYTES - 1024 * 1024

F32 = jnp.float32
BF16 = jnp.bfloat16


def _sigmoid_pair(x):
    e = jnp.exp(-jnp.abs(x))
    inv = 1.0 / (1.0 + e)
    pos = x >= 0
    return jnp.where(pos, inv, e * inv), jnp.where(pos, e * inv, inv)


def _silu(x):
    return x * (0.5 * jnp.tanh(0.5 * x) + 0.5)


def _cumsum_rows(x):
    n = x.shape[0]
    row = lax.broadcasted_iota(jnp.int32, x.shape, 0)
    k = 1
    while k < n:
        x = x + jnp.where(row >= k, pltpu.roll(x, k, axis=0), 0.0)
        k *= 2
    return x


def _rmsnorm_rows(x, w):
    ms = jnp.mean(x * x, axis=-1, keepdims=True)
    return (x * lax.rsqrt(ms + EPS)) * w


def _dot_nt(a, b):
    return lax.dot_general(a, b, (((1,), (1,)), ((), ())), preferred_element_type=F32)


def _dot_tn(a, b):
    return lax.dot_general(a, b, (((0,), (0,)), ((), ())), preferred_element_type=F32)


def _dot(a, b):
    return jnp.dot(a, b, preferred_element_type=F32)


def _gate_features(fr, lb, oml):
    sig, sneg = _sigmoid_pair(fr)
    logf = jnp.log(lb + oml * sig)
    return logf, oml * sneg


def _attention_unit(layer, qb, kh, za_ref, kv_ref, y_ref, bias_ref, sink_ref, c_tile):
    r0 = qb * QBLK
    kpos = c_tile + r0 - QBLK + lax.broadcasted_iota(jnp.int32, (1, 2 * QBLK), 1)
    kvalid = kpos >= 0
    kk = kv_ref[r0:r0 + 2 * QBLK, kh * HEAD_DIM:(kh + 1) * HEAD_DIM]
    q4 = jnp.concatenate(
        [za_ref[r0:r0 + QBLK, (kh * GROUP + g) * HEAD_DIM:(kh * GROUP + g + 1) * HEAD_DIM]
         for g in range(GROUP)], axis=0)
    s = _dot_nt((q4 * (HEAD_DIM ** -0.5)).astype(BF16), kk)
    yield
    s = s + bias_ref[kh * GROUP * QBLK:(kh + 1) * GROUP * QBLK, :].astype(F32)
    s = jnp.where(kvalid, s, NEG)
    es, dens = [], []
    for g in range(GROUP):
        sink = sink_ref[layer, kh * GROUP + g]
        sg = s[g * QBLK:(g + 1) * QBLK]
        m = jnp.maximum(jnp.max(sg, axis=-1, keepdims=True), sink)
        e = jnp.exp(sg - m)
        dens.append(jnp.sum(e, axis=-1, keepdims=True) + jnp.exp(sink - m))
        es.append(e.astype(BF16))
    vv = kv_ref[r0:r0 + 2 * QBLK, KV_DIM + kh * HEAD_DIM:KV_DIM + (kh + 1) * HEAD_DIM]
    o4 = _dot(jnp.concatenate(es, axis=0), vv)
    yield
    outs = []
    for g in range(GROUP):
        h = kh * GROUP + g
        og = o4[g * QBLK:(g + 1) * QBLK] / dens[g]
        ga = za_ref[r0:r0 + QBLK, OFF_GA + h * HEAD_DIM:OFF_GA + (h + 1) * HEAD_DIM]
        outs.append(og * _silu(ga))
    y_ref[r0:r0 + QBLK, kh * GROUP * HEAD_DIM:(kh + 1) * GROUP * HEAD_DIM] = (
        jnp.concatenate(outs, axis=-1).astype(BF16))


def _decay_scores_direct(q, kr, a):
    n = q.shape[0]
    row = lax.broadcasted_iota(jnp.int32, (n, n), 0)
    col = lax.broadcasted_iota(jnp.int32, (n, n), 1)

    def diagonal(d, acc):
        ks = pltpu.roll(kr, d, axis=0)
        a_s = pltpu.roll(a, d, axis=0)
        p = q * ks * jnp.exp(jnp.minimum(a - a_s, 0.0))
        return jnp.where(col == row - d, jnp.sum(p, axis=-1, keepdims=True), acc)

    acc = jnp.zeros((n, n), F32)
    if n <= DEC_SEQ:
        for d in range(n):
            acc = diagonal(d, acc)
        return acc
    return lax.fori_loop(0, n, diagonal, acc)


def _hgrn2_unit(ci, hh, zr_ref, st_ref, y_ref, lb_ref, oml_ref, recw_ref, c_tile, spans=None, heads_in_buf=None):
    heads_in_buf = REC_GROUP if heads_in_buf is None else heads_in_buf
    part = heads_in_buf * REC_HEAD_DIM
    row = lax.broadcasted_iota(jnp.int32, (CHUNK, CHUNK), 0)
    col = lax.broadcasted_iota(jnp.int32, (CHUNK, CHUNK), 1)
    causal = col <= row
    r0 = ci * CHUNK if isinstance(ci, int) else pl.multiple_of(ci * CHUNK, CHUNK)
    rows = pl.ds(r0, CHUNK)
    rpos = c_tile + r0 + lax.broadcasted_iota(jnp.int32, (CHUNK, 1), 0)
    rvalid = rpos >= 0
    if isinstance(hh, int):
        cs = hh * REC_HEAD_DIM
        zc = (hh % heads_in_buf) * REC_HEAD_DIM
    else:
        cs = pl.multiple_of(hh * REC_HEAD_DIM, REC_HEAD_DIM)
        zc = 0
    head_cols = pl.ds(cs, REC_HEAD_DIM)
    lb = lb_ref[:, head_cols]
    oml = oml_ref[:, head_cols]
    logf, kr = _gate_features(zr_ref[rows, part + zc:part + zc + REC_HEAD_DIM], lb, oml)
    q = _silu(zr_ref[rows, zc:zc + REC_HEAD_DIM])
    v = jnp.where(rvalid, zr_ref[rows, 2 * part + zc:2 * part + zc + REC_HEAD_DIM], 0.0)
    vb = v.astype(BF16)
    a = _cumsum_rows(logf)
    a_mid = a[HALF - 1:HALF, :]
    a_end = a[CHUNK - 1:CHUNK, :]
    if spans is None:
        scores = _decay_scores_direct(q, kr, a)
    else:
        qh = (q * jnp.exp(a - a_mid)).astype(BF16)
        kh_ = (kr * jnp.exp(a_mid - a)).astype(BF16)
        scores = _dot_nt(qh, kh_)
        spans.append(jnp.maximum(a[0:1, :] - a_mid, a_mid - a_end))
    qt = (q * jnp.exp(a)).astype(BF16)
    kt = (kr * jnp.exp(a_end - a)).astype(BF16)
    st = st_ref[hh]
    o_state = _dot_nt(qt, st.astype(BF16))
    yield
    att = jnp.where(causal, scores, 0.0)
    o = _dot(att.astype(BF16), vb) + o_state
    st_ref[hh] = st * jnp.exp(a_end) + _dot_tn(vb, kt)
    yield
    on = o * lax.rsqrt(jnp.mean(o * o, axis=-1, keepdims=True) + EPS)
    gr = zr_ref[rows, 3 * part + zc:3 * part + zc + REC_HEAD_DIM]
    yr = (on * recw_ref[:, head_cols]) * _silu(gr)
    y_ref[rows, pl.ds(D_ATTN + cs, REC_HEAD_DIM)] = yr.astype(BF16)


def _emit_staggered(units, fillers=(), lag=1):
    n = len(units)
    rounds = n + 2 * lag
    at_round = [[] for _ in range(rounds)]
    for k, f in enumerate(fillers):
        at_round[k * rounds // len(fillers)].append(f)
    for i in range(rounds):
        if i < n:
            next(units[i])
        for f in at_round[i]:
            f()
        if 0 <= i - lag < n:
            next(units[i - lag])
        if 0 <= i - 2 * lag < n:
            next(units[i - 2 * lag], None)


def _main_tile(s):
    t = jnp.maximum(s - 1, 0)
    return lax.div(t, ROW_TILES), lax.rem(t, ROW_TILES)


def _prompt_layer_kernel(layer, final, sink_ref, meta_ref, hin_ref, win_ref, wout_ref, normw_ref, lb_ref, oml_ref,
                         recw_ref, bias_ref, finw_ref, *refs):
    if final:
        hout_ref, pk_ref, pv_ref, ps_ref = refs[:4]
        refs = refs[4:]
    else:
        meta_out_ref, hout_ref, pk_ref, pv_ref, ps_ref = refs[:5]
        refs = refs[5:]
    xn_ref, za_ref, zr_ref, kv_ref, st_ref, y_ref, kv_meta_ref, st_meta_ref, st_start_ref, zhead_ref = refs
    s = pl.program_id(0)
    is_meta = s == 0
    _, j = _main_tile(s)

    @pl.when(is_meta)
    def _():
        kv_ref[0:QBLK, :] = jnp.zeros((QBLK, 2 * KV_DIM), BF16)
        st_ref[...] = jnp.zeros(st_ref.shape, F32)

    @pl.when(jnp.logical_and(jnp.logical_not(is_meta), j == 0))
    def _():
        kv_ref[0:QBLK, :] = kv_meta_ref[...]
        st_ref[...] = st_meta_ref[...]

    st_start_ref[...] = st_ref[...]

    def tile_input(cols=slice(None)):
        meta = meta_ref[:, cols]
        meta_rows = jnp.concatenate([jnp.zeros((PAD, meta.shape[1]), F32), meta], axis=0)
        return jnp.where(is_meta, meta_rows, hin_ref[0, :, cols])

    c_tile = jnp.where(is_meta, -PAD, j * TILE + N_META)
    xn_ref[...] = _rmsnorm_rows(tile_input(), normw_ref[...]).astype(BF16)

    def project(dst_ref, dst_col, src_col):
        dst_ref[:, dst_col:dst_col + PROJ_COLS] = _dot(xn_ref[...], win_ref[:, src_col:src_col + PROJ_COLS])

    def project_rec(dst_ref, group):
        return [functools.partial(project, dst_ref, part * PROJ_COLS, off + group * PROJ_COLS)
                for part, off in enumerate((OFF_QR, OFF_FR, OFF_IR, OFF_GR))]

    def out_proj(k0, k1, c0, first_half):
        cols = slice(c0, c0 + PROJ_COLS)
        base = tile_input(cols) if first_half else hout_ref[0, :, cols]
        hout_ref[0, :, cols] = base + _dot(y_ref[:, k0:k1], wout_ref[k0:k1, cols])

    for c0 in range(0, ATT_COLS, PROJ_COLS):
        project(za_ref, c0, c0)
    kv_ref[QBLK:QBLK + TILE, :] = za_ref[:, OFF_KA:OFF_KA + 2 * KV_DIM].astype(BF16)
    pk_ref[0] = za_ref[TILE - WINDOW:TILE, OFF_KA:OFF_KA + KV_DIM]
    pv_ref[0] = za_ref[TILE - WINDOW:TILE, OFF_VA:OFF_VA + KV_DIM]

    att_units = [_attention_unit(layer, qb, kh, za_ref, kv_ref, y_ref, bias_ref, sink_ref, c_tile)
                 for qb in range(TILE // QBLK) for kh in range(N_KV)]
    _emit_staggered(att_units, project_rec(zr_ref, 0))
    kv_ref[0:QBLK, :] = kv_ref[TILE:TILE + QBLK, :]

    n_groups = N_REC_HEADS // REC_GROUP
    z_bufs = [zr_ref, za_ref]
    spans = []

    def rec_units(grp, state_ref, spans_):
        return [_hgrn2_unit(ci, grp * REC_GROUP + hl, z_bufs[grp % 2], state_ref, y_ref, lb_ref, oml_ref,
                            recw_ref, c_tile, spans_)
                for ci in range(TILE // CHUNK) for hl in range(REC_GROUP)]

    for grp in range(n_groups):
        if grp + 1 < n_groups:
            fillers = project_rec(z_bufs[(grp + 1) % 2], grp + 1)
        else:
            fillers = [functools.partial(out_proj, 0, D_ATTN, c0, True) for c0 in range(0, D_MODEL, PROJ_COLS)]
        _emit_staggered(rec_units(grp, st_ref, spans), fillers)
    for c0 in range(0, D_MODEL, PROJ_COLS):
        out_proj(D_ATTN, D_MODEL, c0, False)

    widest = jnp.max(jnp.concatenate(spans, axis=0))

    @pl.when(widest > MAX_FACTOR_EXPONENT)
    def _():
        def replay_chunk(hh, ci, carry):
            for _ in _hgrn2_unit(ci, hh, zhead_ref, st_start_ref, y_ref, lb_ref, oml_ref, recw_ref, c_tile,
                                 spans=None, heads_in_buf=1):
                pass
            return carry

        def replay_head(hh, carry):
            for part, off in enumerate((OFF_QR, OFF_FR, OFF_IR, OFF_GR)):
                src = pl.multiple_of(off + hh * REC_HEAD_DIM, REC_HEAD_DIM)
                zhead_ref[:, part * REC_HEAD_DIM:(part + 1) * REC_HEAD_DIM] = _dot(
                    xn_ref[...], win_ref[:, pl.ds(src, REC_HEAD_DIM)])
            return lax.fori_loop(0, TILE // CHUNK, functools.partial(replay_chunk, hh), carry)

        lax.fori_loop(0, N_REC_HEADS, replay_head, 0)
        for c0 in range(0, D_MODEL, PROJ_COLS):
            cols = slice(c0, c0 + PROJ_COLS)
            hout_ref[0, :, cols] = tile_input(cols) + _dot(y_ref[...], wout_ref[:, cols])

    ps_ref[0] = st_ref[...]

    @pl.when(is_meta)
    def _():
        kv_meta_ref[...] = kv_ref[0:QBLK, :]
        st_meta_ref[...] = st_ref[...]
        if not final:
            meta_out_ref[...] = hout_ref[0, PAD:TILE, :]

    if final:
        @pl.when(jnp.logical_not(is_meta))
        def _():
            hout_ref[0] = _rmsnorm_rows(hout_ref[0], finw_ref[...])


def _layer_block(shape, layer):
    nd = len(shape)
    return pl.BlockSpec((None,) + tuple(shape), lambda s: (layer,) + (0,) * nd, pipeline_mode=pl.Buffered(1))


def _const_block(shape):
    nd = len(shape)
    return pl.BlockSpec(tuple(shape), lambda s: (0,) * nd, pipeline_mode=pl.Buffered(1))


def _prompt_layer(layer, h_meta, h_main, w_in, w_out, norm_w, lb, oml, rec_w, sinks, bias, fin_w):
    final = layer == DEPTH - 1
    main = pl.BlockSpec((1, TILE, D_MODEL), lambda s: _main_tile(s) + (0,))
    meta = pl.BlockSpec((N_META, D_MODEL), lambda s: (0, 0))
    kv_spec = pl.BlockSpec((1, WINDOW, KV_DIM), lambda s: (_main_tile(s)[0], 0, 0))
    st_spec = pl.BlockSpec((1, N_REC_HEADS, REC_HEAD_DIM, REC_HEAD_DIM), lambda s: (_main_tile(s)[0], 0, 0, 0))
    main_shape = jax.ShapeDtypeStruct((BATCH, SEQ, D_MODEL), F32)
    meta_shape = jax.ShapeDtypeStruct((N_META, D_MODEL), F32)
    return pl.pallas_call(
        functools.partial(_prompt_layer_kernel, layer, final),
        grid=(1 + BATCH * ROW_TILES,),
        in_specs=[
            pl.BlockSpec(memory_space=pltpu.SMEM),
            _const_block((N_META, D_MODEL)),
            main,
            _const_block((D_MODEL, D_IN)),
            _const_block((D_MODEL, D_MODEL)),
            _layer_block((1, D_MODEL), layer),
            _layer_block((1, D_REC), layer),
            _layer_block((1, D_REC), layer),
            _layer_block((1, D_REC), layer),
            _const_block((N_HEADS * QBLK, 2 * QBLK)),
            _const_block((1, D_MODEL)),
        ],
        out_specs=([] if final else [meta]) + [main, kv_spec, kv_spec, st_spec],
        out_shape=([] if final else [meta_shape]) + [
            main_shape,
            jax.ShapeDtypeStruct((BATCH, WINDOW, KV_DIM), F32),
            jax.ShapeDtypeStruct((BATCH, WINDOW, KV_DIM), F32),
            jax.ShapeDtypeStruct((BATCH, N_REC_HEADS, REC_HEAD_DIM, REC_HEAD_DIM), F32),
        ],
        scratch_shapes=[
            pltpu.VMEM((TILE, D_MODEL), BF16),
            pltpu.VMEM((TILE, ATT_COLS), F32),
            pltpu.VMEM((TILE, 4 * PROJ_COLS), F32),
            pltpu.VMEM((QBLK + TILE, 2 * KV_DIM), BF16),
            pltpu.VMEM((N_REC_HEADS, REC_HEAD_DIM, REC_HEAD_DIM), F32),
            pltpu.VMEM((TILE, D_MODEL), BF16),
            pltpu.VMEM((QBLK, 2 * KV_DIM), BF16),
            pltpu.VMEM((N_REC_HEADS, REC_HEAD_DIM, REC_HEAD_DIM), F32),
            pltpu.VMEM((N_REC_HEADS, REC_HEAD_DIM, REC_HEAD_DIM), F32),
            pltpu.VMEM((TILE, 4 * REC_HEAD_DIM), F32),
        ],
        compiler_params=pltpu.CompilerParams(
            dimension_semantics=("arbitrary",), vmem_limit_bytes=VMEM_LIMIT),
        name="prompt_layer",
    )(sinks, h_meta, h_main, w_in, w_out, norm_w, lb, oml, rec_w, bias, fin_w)


DEC_PROJ_COLS = D_IN // 4
DEC_OUT_COLS = D_MODEL // 2
DEC_VMEM_LIMIT = 56 * 1024 * 1024


def _dec_proj_kernel(hs_ref, normw_ref, win_ref, z_ref, wb_ref):
    xn = _rmsnorm_rows(hs_ref[...], normw_ref[...]).astype(BF16)
    for c0 in range(0, DEC_PROJ_COLS, REC_HEAD_DIM):
        cols = slice(c0, c0 + REC_HEAD_DIM)
        wb = win_ref[:, cols].astype(BF16)
        wb_ref[:, cols] = wb
        z_ref[:, cols] = _dot(xn, wb)


def _dec_proj(layer, hs, norm_w, w_in):
    return pl.pallas_call(
        _dec_proj_kernel,
        grid=(D_IN // DEC_PROJ_COLS,),
        in_specs=[
            pl.BlockSpec((N_DEC, D_MODEL), lambda n: (0, 0)),
            pl.BlockSpec((None, 1, D_MODEL), lambda n: (layer, 0, 0)),
            pl.BlockSpec((None, D_MODEL, DEC_PROJ_COLS), lambda n: (layer, 0, n)),
        ],
        out_specs=[pl.BlockSpec((N_DEC, DEC_PROJ_COLS), lambda n: (0, n)),
                   pl.BlockSpec((D_MODEL, DEC_PROJ_COLS), lambda n: (0, n))],
        out_shape=[jax.ShapeDtypeStruct((N_DEC, D_IN), F32),
                   jax.ShapeDtypeStruct((D_MODEL, D_IN), BF16)],
        compiler_params=pltpu.CompilerParams(dimension_semantics=("arbitrary",), vmem_limit_bytes=DEC_VMEM_LIMIT),
        name="dec_proj",
    )(hs, norm_w, w_in)


def _dec_out_kernel(hs_ref, y_ref, wout_ref, o_ref, wb_ref):
    for c0 in range(0, DEC_OUT_COLS, 2 * REC_HEAD_DIM):
        cols = slice(c0, c0 + 2 * REC_HEAD_DIM)
        wb = wout_ref[:, cols].astype(BF16)
        wb_ref[:, cols] = wb
        o_ref[:, cols] = hs_ref[:, cols] + _dot(y_ref[...], wb)


def _dec_out(layer, hs, y, w_out):
    return pl.pallas_call(
        _dec_out_kernel,
        grid=(D_MODEL // DEC_OUT_COLS,),
        in_specs=[
            pl.BlockSpec((N_DEC, DEC_OUT_COLS), lambda n: (0, n)),
            pl.BlockSpec((N_DEC, D_MODEL), lambda n: (0, 0)),
            pl.BlockSpec((None, D_MODEL, DEC_OUT_COLS), lambda n: (layer, 0, n)),
        ],
        out_specs=[pl.BlockSpec((N_DEC, DEC_OUT_COLS), lambda n: (0, n)),
                   pl.BlockSpec((D_MODEL, DEC_OUT_COLS), lambda n: (0, n))],
        out_shape=[jax.ShapeDtypeStruct((N_DEC, D_MODEL), F32),
                   jax.ShapeDtypeStruct((D_MODEL, D_MODEL), BF16)],
        compiler_params=pltpu.CompilerParams(dimension_semantics=("arbitrary",), vmem_limit_bytes=DEC_VMEM_LIMIT),
        name="dec_out",
    )(hs, y, w_out)


DEC_SEQS_PER_STEP = 4
DEC_LAG = 8


def _dec_attention_unit(layer, q, kh, sink_ref, z_ref, ck_ref, cv_ref, biasc_ref, biasn_ref, y_ref):
    t = DEC_SEQ
    tok = slice(q * t, (q + 1) * t)
    hs_ = slice(kh * HEAD_DIM, (kh + 1) * HEAD_DIM)
    kc = ck_ref[q, :, hs_].astype(BF16)
    kn = z_ref[tok, OFF_KA + kh * HEAD_DIM:OFF_KA + (kh + 1) * HEAD_DIM]
    q4 = jnp.concatenate(
        [z_ref[tok, (kh * GROUP + g) * HEAD_DIM:(kh * GROUP + g + 1) * HEAD_DIM] for g in range(GROUP)],
        axis=0) * (HEAD_DIM ** -0.5)
    rows = slice(kh * GROUP * t, (kh + 1) * GROUP * t)
    sc = _dot_nt(q4.astype(BF16), kc)
    sn = _dot_nt(q4, kn)
    yield
    sc = sc + biasc_ref[rows, :]
    sn = sn + biasn_ref[rows, :]
    ecs, ens, dens = [], [], []
    for g in range(GROUP):
        sink = sink_ref[layer, kh * GROUP + g]
        scg = sc[g * t:(g + 1) * t]
        sng = sn[g * t:(g + 1) * t]
        m = jnp.maximum(jnp.maximum(jnp.max(scg, axis=-1, keepdims=True),
                                    jnp.max(sng, axis=-1, keepdims=True)), sink)
        ec = jnp.exp(scg - m)
        en = jnp.exp(sng - m)
        dens.append(jnp.sum(ec, axis=-1, keepdims=True) + jnp.sum(en, axis=-1, keepdims=True)
                    + jnp.exp(sink - m))
        ecs.append(ec)
        ens.append(en)
    vc = cv_ref[q, :, hs_]
    vn = z_ref[tok, OFF_VA + kh * HEAD_DIM:OFF_VA + (kh + 1) * HEAD_DIM]
    o4 = (_dot(jnp.concatenate(ecs, axis=0).astype(BF16), vc.astype(BF16))
          + _dot(jnp.concatenate(ens, axis=0), vn))
    yield
    outs = []
    for g in range(GROUP):
        h = kh * GROUP + g
        og = o4[g * t:(g + 1) * t] / dens[g]
        ga = z_ref[tok, OFF_GA + h * HEAD_DIM:OFF_GA + (h + 1) * HEAD_DIM]
        outs.append(og * _silu(ga))
    y_ref[tok, kh * GROUP * HEAD_DIM:(kh + 1) * GROUP * HEAD_DIM] = jnp.concatenate(outs, axis=-1).astype(BF16)


def _dec_hgrn2_unit(q, hh, z_ref, s0_ref, lb_ref, oml_ref, recw_ref, y_ref, ns_ref):
    t = DEC_SEQ
    tok = slice(q * t, (q + 1) * t)
    row = lax.broadcasted_iota(jnp.int32, (t, t), 0)
    col = lax.broadcasted_iota(jnp.int32, (t, t), 1)
    causal = col <= row
    cs = hh * REC_HEAD_DIM
    lb = lb_ref[:, cs:cs + REC_HEAD_DIM]
    oml = oml_ref[:, cs:cs + REC_HEAD_DIM]
    logf, kr = _gate_features(z_ref[tok, OFF_FR + cs:OFF_FR + cs + REC_HEAD_DIM], lb, oml)
    qv = _silu(z_ref[tok, OFF_QR + cs:OFF_QR + cs + REC_HEAD_DIM])
    v = z_ref[tok, OFF_IR + cs:OFF_IR + cs + REC_HEAD_DIM]
    a = _cumsum_rows(logf)
    a_end = a[t - 1:t, :]
    scores = _decay_scores_direct(qv, kr, a)
    qt = qv * jnp.exp(a)
    kt = kr * jnp.exp(a_end - a)
    s0 = s0_ref[q, hh]
    o_state = _dot(qt, s0)
    yield
    att = jnp.where(causal, scores, 0.0)
    o = _dot(att, v) + o_state
    ktd = jnp.concatenate([kt, jnp.exp(a_end), jnp.zeros((t - 1, REC_HEAD_DIM), F32)], axis=0).T
    ns_ref[q, hh] = s0 * ktd[:, t:t + 1] + _dot(ktd[:, 0:t], v)
    yield
    on = o * lax.rsqrt(jnp.mean(o * o, axis=-1, keepdims=True) + EPS)
    gr = z_ref[tok, OFF_GR + cs:OFF_GR + cs + REC_HEAD_DIM]
    yr = (on * recw_ref[:, cs:cs + REC_HEAD_DIM]) * _silu(gr)
    y_ref[tok, D_ATTN + cs:D_ATTN + cs + REC_HEAD_DIM] = yr.astype(BF16)


def _dec_mix_kernel(layer, sink_ref, z_ref, ck_ref, cv_ref, s0_ref, lb_ref, oml_ref, recw_ref, biasc_ref,
                    biasn_ref, y_ref, nk_ref, nv_ref, ns_ref):
    t = DEC_SEQ
    for q in range(DEC_SEQS_PER_STEP):
        tok = slice(q * t, (q + 1) * t)
        nk_ref[q, 0:WINDOW - t, :] = ck_ref[q, t:WINDOW, :]
        nk_ref[q, WINDOW - t:WINDOW, :] = z_ref[tok, OFF_KA:OFF_KA + KV_DIM]
        nv_ref[q, 0:WINDOW - t, :] = cv_ref[q, t:WINDOW, :]
        nv_ref[q, WINDOW - t:WINDOW, :] = z_ref[tok, OFF_VA:OFF_VA + KV_DIM]
    units = []
    for kh in range(N_KV):
        units += [_dec_attention_unit(layer, q, kh, sink_ref, z_ref, ck_ref, cv_ref, biasc_ref, biasn_ref, y_ref)
                  for q in range(DEC_SEQS_PER_STEP)]
    for hh in range(N_REC_HEADS):
        units += [_dec_hgrn2_unit(q, hh, z_ref, s0_ref, lb_ref, oml_ref, recw_ref, y_ref, ns_ref)
                  for q in range(DEC_SEQS_PER_STEP)]
    _emit_staggered(units, lag=DEC_LAG)


def _dec_mix(layer, z, ck, cv, s0, lb, oml, rec_w, sinks, bias_c, bias_n):
    t = DEC_SEQ
    nq = DEC_SEQS_PER_STEP
    const2 = lambda s: (0, 0)
    per_layer = lambda s: (layer, 0, 0)
    kv_in = pl.BlockSpec((None, nq, WINDOW, KV_DIM), lambda s: (layer, s, 0, 0))
    st_in = pl.BlockSpec((None, nq, N_REC_HEADS, REC_HEAD_DIM, REC_HEAD_DIM), lambda s: (layer, s, 0, 0, 0))
    return pl.pallas_call(
        functools.partial(_dec_mix_kernel, layer),
        grid=(DEC_BATCH // nq,),
        in_specs=[
            pl.BlockSpec(memory_space=pltpu.SMEM),
            pl.BlockSpec((nq * t, D_IN), lambda s: (s, 0)),
            kv_in,
            kv_in,
            st_in,
            pl.BlockSpec((None, 1, D_REC), per_layer),
            pl.BlockSpec((None, 1, D_REC), per_layer),
            pl.BlockSpec((None, 1, D_REC), per_layer),
            pl.BlockSpec((N_HEADS * t, WINDOW), const2),
            pl.BlockSpec((N_HEADS * t, t), const2),
        ],
        out_specs=[
            pl.BlockSpec((nq * t, D_MODEL), lambda s: (s, 0)),
            pl.BlockSpec((nq, WINDOW, KV_DIM), lambda s: (s, 0, 0)),
            pl.BlockSpec((nq, WINDOW, KV_DIM), lambda s: (s, 0, 0)),
            pl.BlockSpec((nq, N_REC_HEADS, REC_HEAD_DIM, REC_HEAD_DIM), lambda s: (s, 0, 0, 0)),
        ],
        out_shape=[
            jax.ShapeDtypeStruct((N_DEC, D_MODEL), BF16),
            jax.ShapeDtypeStruct((DEC_BATCH, WINDOW, KV_DIM), F32),
            jax.ShapeDtypeStruct((DEC_BATCH, WINDOW, KV_DIM), F32),
            jax.ShapeDtypeStruct((DEC_BATCH, N_REC_HEADS, REC_HEAD_DIM, REC_HEAD_DIM), F32),
        ],
        compiler_params=pltpu.CompilerParams(dimension_semantics=("arbitrary",)),
        name="dec_mix",
    )(sinks, z, ck, cv, s0, lb, oml, rec_w, bias_c, bias_n)


def _final_norm_kernel(x_ref, w_ref, o_ref):
    o_ref[...] = _rmsnorm_rows(x_ref[...], w_ref[...])


def _final_norm_dec(hs, w):
    return pl.pallas_call(
        _final_norm_kernel,
        out_shape=jax.ShapeDtypeStruct((N_DEC, D_MODEL), F32),
        name="final_norm_dec",
    )(hs, w)


def _t5_bucket(dist):
    max_exact = N_BUCKETS // 2
    d = jnp.maximum(dist, 0)
    df = jnp.maximum(d, 1).astype(F32)
    large = max_exact + (jnp.log(df / max_exact) / math.log(MAX_DISTANCE / max_exact)
                         * (N_BUCKETS - max_exact)).astype(jnp.int32)
    large = jnp.minimum(large, N_BUCKETS - 1)
    return jnp.where(d < max_exact, d, large)


def _bias_tables(rel_bias_table):
    bias_d = rel_bias_table[_t5_bucket(jnp.arange(WINDOW))].T.astype(F32)
    period = 3 * WINDOW + 1
    u = jnp.concatenate([jnp.full((N_HEADS, 1), NEG, F32), bias_d[:, ::-1],
                         jnp.full((N_HEADS, period - 1 - WINDOW), NEG, F32)], axis=1)
    rows = jnp.tile(u, (1, WINDOW))[:, :WINDOW * (period - 1)].reshape(N_HEADS, WINDOW, period - 1)
    prompt = rows[:, :, :2 * WINDOW].reshape(N_HEADS * WINDOW, 2 * WINDOW).astype(BF16)
    dec = rows[:, :DEC_SEQ, :WINDOW + DEC_SEQ].reshape(N_HEADS * DEC_SEQ, WINDOW + DEC_SEQ)
    return prompt, dec[:, :WINDOW], dec[:, WINDOW:]


def kernel(x_prompt, x_sample, cache_k, cache_v, state_h, meta_tokens, w_in, w_out, norm_w, final_norm_w,
           attn_sinks, rel_bias_table, hgrn_lb_logits, hgrn_norm_w):
    pl_ = jax.nn.softmax(hgrn_lb_logits.astype(F32), axis=0)
    lb = (jnp.cumsum(pl_, axis=0) - pl_[0:1]).reshape(DEPTH, 1, D_REC)
    oml = 1.0 - lb
    bias_p, bias_dc, bias_dn = _bias_tables(rel_bias_table)
    nw = norm_w.astype(F32).reshape(DEPTH, 1, D_MODEL)
    rw = hgrn_norm_w.astype(F32).reshape(DEPTH, 1, D_REC)
    fw = final_norm_w.astype(F32).reshape(1, D_MODEL)
    sinks = attn_sinks.astype(F32)

    h_meta = meta_tokens.astype(F32)
    h_main = x_prompt
    hs = x_sample.reshape(N_DEC, D_MODEL)
    ck = cache_k.reshape(DEPTH, DEC_BATCH, WINDOW, KV_DIM)
    cv = cache_v.reshape(DEPTH, DEC_BATCH, WINDOW, KV_DIM)

    pk, pv, ps, sk, sv, ss = [], [], [], [], [], []
    for l in range(DEPTH):
        z, w_in_b = _dec_proj(l, hs, nw, w_in)
        y, nk, nv, ns = _dec_mix(l, z, ck, cv, state_h, lb, oml, rw, sinks, bias_dc, bias_dn)
        hs, w_out_b = _dec_out(l, hs, y, w_out)
        sk.append(nk)
        sv.append(nv)
        ss.append(ns)

        outs = _prompt_layer(l, h_meta, h_main, w_in_b, w_out_b, nw, lb, oml, rw, sinks, bias_p, fw)
        if l < DEPTH - 1:
            h_meta, h_main, k_l, v_l, s_l = outs
        else:
            h_main, k_l, v_l, s_l = outs
        pk.append(k_l)
        pv.append(v_l)
        ps.append(s_l)

    y_prompt = h_main
    y_sample = _final_norm_dec(hs, fw).reshape(DEC_BATCH, DEC_SEQ, D_MODEL)
    kv_shape = (DEPTH, -1, WINDOW, N_KV, HEAD_DIM)
    return (y_prompt, y_sample,
            jnp.stack(pk).reshape(kv_shape), jnp.stack(pv).reshape(kv_shape),
            jnp.swapaxes(jnp.stack(ps), -1, -2),
            jnp.stack(sk).reshape(kv_shape), jnp.stack(sv).reshape(kv_shape), jnp.stack(ss))
```

```python
import functools
import math

import jax
import jax.numpy as jnp
from jax import lax
from jax.experimental import pallas as pl
from jax.experimental.pallas import tpu as pltpu

D_MODEL = 2048
BATCH = 4
SEQ = 2048
DEPTH = 4
DEC_BATCH = 32
DEC_SEQ = 8
N_META = 16
D_ATTN = 1024
D_REC = 1024
HEAD_DIM = 64
N_HEADS = 16
N_KV = 4
GROUP = 4
KV_DIM = 256
WINDOW = 128
N_BUCKETS = 32
MAX_DISTANCE = 128
REC_HEAD_DIM = 128
N_REC_HEADS = 8
EPS = 1e-6
D_IN = 2 * D_ATTN + 2 * KV_DIM + 4 * D_REC

OFF_QA = 0
OFF_KA = D_ATTN
OFF_VA = OFF_KA + KV_DIM
OFF_GA = OFF_VA + KV_DIM
OFF_QR = OFF_GA + D_ATTN
OFF_FR = OFF_QR + D_REC
OFF_IR = OFF_FR + D_REC
OFF_GR = OFF_IR + D_REC
ATT_COLS = OFF_QR

TILE = 256
QBLK = WINDOW
CHUNK = 128
HALF = CHUNK // 2
ROW_TILES = SEQ // TILE
PAD = TILE - N_META
PROJ_COLS = 512
REC_GROUP = PROJ_COLS // REC_HEAD_DIM
NEG = -1e30
MAX_FACTOR_EXPONENT = 80.0
N_DEC = DEC_BATCH * DEC_SEQ
V7X_VMEM_BYTES = 64 * 1024 * 1024
VMEM_LIMIT = V7X_VMEM_BYTES - 1024 * 1024

F32 = jnp.float32
BF16 = jnp.bfloat16


def _sigmoid_pair(x):
    e = jnp.exp(-jnp.abs(x))
    inv = 1.0 / (1.0 + e)
    pos = x >= 0
    return jnp.where(pos, inv, e * inv), jnp.where(pos, e * inv, inv)


def _silu(x):
    return x * (0.5 * jnp.tanh(0.5 * x) + 0.5)


def _cumsum_rows(x):
    n = x.shape[0]
    row = lax.broadcasted_iota(jnp.int32, x.shape, 0)
    k = 1
    while k < n:
        x = x + jnp.where(row >= k, pltpu.roll(x, k, axis=0), 0.0)
        k *= 2
    return x


def _rmsnorm_rows(x, w):
    ms = jnp.mean(x * x, axis=-1, keepdims=True)
    return (x * lax.rsqrt(ms + EPS)) * w


def _dot_nt(a, b):
    return lax.dot_general(a, b, (((1,), (1,)), ((), ())), preferred_element_type=F32)


def _dot_tn(a, b):
    return lax.dot_general(a, b, (((0,), (0,)), ((), ())), preferred_element_type=F32)


def _dot(a, b):
    return jnp.dot(a, b, preferred_element_type=F32)


def _gate_features(fr, lb, oml):
    sig, sneg = _sigmoid_pair(fr)
    logf = jnp.log(lb + oml * sig)
    return logf, oml * sneg


def _attention_unit(layer, qb, kh, za_ref, kv_ref, y_ref, bias_ref, sink_ref, c_tile):
    r0 = qb * QBLK
    kpos = c_tile + r0 - QBLK + lax.broadcasted_iota(jnp.int32, (1, 2 * QBLK), 1)
    kvalid = kpos >= 0
    kk = kv_ref[r0:r0 + 2 * QBLK, kh * HEAD_DIM:(kh + 1) * HEAD_DIM]
    q4 = jnp.concatenate(
        [za_ref[r0:r0 + QBLK, (kh * GROUP + g) * HEAD_DIM:(kh * GROUP + g + 1) * HEAD_DIM]
         for g in range(GROUP)], axis=0)
    s = _dot_nt((q4 * (HEAD_DIM ** -0.5)).astype(BF16), kk)
    yield
    s = s + bias_ref[kh * GROUP * QBLK:(kh + 1) * GROUP * QBLK, :].astype(F32)
    s = jnp.where(kvalid, s, NEG)
    es, dens = [], []
    for g in range(GROUP):
        sink = sink_ref[layer, kh * GROUP + g]
        sg = s[g * QBLK:(g + 1) * QBLK]
        m = jnp.maximum(jnp.max(sg, axis=-1, keepdims=True), sink)
        e = jnp.exp(sg - m)
        dens.append(jnp.sum(e, axis=-1, keepdims=True) + jnp.exp(sink - m))
        es.append(e.astype(BF16))
    vv = kv_ref[r0:r0 + 2 * QBLK, KV_DIM + kh * HEAD_DIM:KV_DIM + (kh + 1) * HEAD_DIM]
    o4 = _dot(jnp.concatenate(es, axis=0), vv)
    yield
    outs = []
    for g in range(GROUP):
        h = kh * GROUP + g
        og = o4[g * QBLK:(g + 1) * QBLK] / dens[g]
        ga = za_ref[r0:r0 + QBLK, OFF_GA + h * HEAD_DIM:OFF_GA + (h + 1) * HEAD_DIM]
        outs.append(og * _silu(ga))
    y_ref[r0:r0 + QBLK, kh * GROUP * HEAD_DIM:(kh + 1) * GROUP * HEAD_DIM] = (
        jnp.concatenate(outs, axis=-1).astype(BF16))


def _decay_scores_direct(q, kr, a):
    n = q.shape[0]
    row = lax.broadcasted_iota(jnp.int32, (n, n), 0)
    col = lax.broadcasted_iota(jnp.int32, (n, n), 1)

    def diagonal(d, acc):
        ks = pltpu.roll(kr, d, axis=0)
        a_s = pltpu.roll(a, d, axis=0)
        p = q * ks * jnp.exp(jnp.minimum(a - a_s, 0.0))
        return jnp.where(col == row - d, jnp.sum(p, axis=-1, keepdims=True), acc)

    acc = jnp.zeros((n, n), F32)
    if n <= DEC_SEQ:
        for d in range(n):
            acc = diagonal(d, acc)
        return acc
    return lax.fori_loop(0, n, diagonal, acc)


def _hgrn2_unit(ci, hh, zr_ref, st_ref, y_ref, lb_ref, oml_ref, recw_ref, c_tile, spans=None, heads_in_buf=None):
    heads_in_buf = REC_GROUP if heads_in_buf is None else heads_in_buf
    part = heads_in_buf * REC_HEAD_DIM
    row = lax.broadcasted_iota(jnp.int32, (CHUNK, CHUNK), 0)
    col = lax.broadcasted_iota(jnp.int32, (CHUNK, CHUNK), 1)
    causal = col <= row
    r0 = ci * CHUNK if isinstance(ci, int) else pl.multiple_of(ci * CHUNK, CHUNK)
    rows = pl.ds(r0, CHUNK)
    rpos = c_tile + r0 + lax.broadcasted_iota(jnp.int32, (CHUNK, 1), 0)
    rvalid = rpos >= 0
    if isinstance(hh, int):
        cs = hh * REC_HEAD_DIM
        zc = (hh % heads_in_buf) * REC_HEAD_DIM
    else:
        cs = pl.multiple_of(hh * REC_HEAD_DIM, REC_HEAD_DIM)
        zc = 0
    head_cols = pl.ds(cs, REC_HEAD_DIM)
    lb = lb_ref[:, head_cols]
    oml = oml_ref[:, head_cols]
    logf, kr = _gate_features(zr_ref[rows, part + zc:part + zc + REC_HEAD_DIM], lb, oml)
    q = _silu(zr_ref[rows, zc:zc + REC_HEAD_DIM])
    v = jnp.where(rvalid, zr_ref[rows, 2 * part + zc:2 * part + zc + REC_HEAD_DIM], 0.0)
    vb = v.astype(BF16)
    a = _cumsum_rows(logf)
    a_mid = a[HALF - 1:HALF, :]
    a_end = a[CHUNK - 1:CHUNK, :]
    if spans is None:
        scores = _decay_scores_direct(q, kr, a)
    else:
        qh = (q * jnp.exp(a - a_mid)).astype(BF16)
        kh_ = (kr * jnp.exp(a_mid - a)).astype(BF16)
        scores = _dot_nt(qh, kh_)
        spans.append(jnp.maximum(a[0:1, :] - a_mid, a_mid - a_end))
    qt = (q * jnp.exp(a)).astype(BF16)
    kt = (kr * jnp.exp(a_end - a)).astype(BF16)
    st = st_ref[hh]
    o_state = _dot_nt(qt, st.astype(BF16))
    yield
    att = jnp.where(causal, scores, 0.0)
    o = _dot(att.astype(BF16), vb) + o_state
    st_ref[hh] = st * jnp.exp(a_end) + _dot_tn(vb, kt)
    yield
    on = o * lax.rsqrt(jnp.mean(o * o, axis=-1, keepdims=True) + EPS)
    gr = zr_ref[rows, 3 * part + zc:3 * part + zc + REC_HEAD_DIM]
    yr = (on * recw_ref[:, head_cols]) * _silu(gr)
    y_ref[rows, pl.ds(D_ATTN + cs, REC_HEAD_DIM)] = yr.astype(BF16)


def _emit_staggered(units, fillers=(), lag=1):
    n = len(units)
    rounds = n + 2 * lag
    at_round = [[] for _ in range(rounds)]
    for k, f in enumerate(fillers):
        at_round[k * rounds // len(fillers)].append(f)
    for i in range(rounds):
        if i < n:
            next(units[i])
        for f in at_round[i]:
            f()
        if 0 <= i - lag < n:
            next(units[i - lag])
        if 0 <= i - 2 * lag < n:
            next(units[i - 2 * lag], None)


def _main_tile(s):
    t = jnp.maximum(s - 1, 0)
    return lax.div(t, ROW_TILES), lax.rem(t, ROW_TILES)


def _prompt_layer_kernel(layer, final, sink_ref, meta_ref, hin_ref, win_ref, wout_ref, normw_ref, lb_ref, oml_ref,
                         recw_ref, bias_ref, finw_ref, *refs):
    if final:
        hout_ref, pk_ref, pv_ref, ps_ref = refs[:4]
        refs = refs[4:]
    else:
        meta_out_ref, hout_ref, pk_ref, pv_ref, ps_ref = refs[:5]
        refs = refs[5:]
    xn_ref, za_ref, zr_ref, kv_ref, st_ref, y_ref, kv_meta_ref, st_meta_ref, st_start_ref, zhead_ref = refs
    s = pl.program_id(0)
    is_meta = s == 0
    _, j = _main_tile(s)

    @pl.when(is_meta)
    def _():
        kv_ref[0:QBLK, :] = jnp.zeros((QBLK, 2 * KV_DIM), BF16)
        st_ref[...] = jnp.zeros(st_ref.shape, F32)

    @pl.when(jnp.logical_and(jnp.logical_not(is_meta), j == 0))
    def _():
        kv_ref[0:QBLK, :] = kv_meta_ref[...]
        st_ref[...] = st_meta_ref[...]

    st_start_ref[...] = st_ref[...]

    def tile_input(cols=slice(None)):
        meta = meta_ref[:, cols]
        meta_rows = jnp.concatenate([jnp.zeros((PAD, meta.shape[1]), F32), meta], axis=0)
        return jnp.where(is_meta, meta_rows, hin_ref[0, :, cols])

    c_tile = jnp.where(is_meta, -PAD, j * TILE + N_META)
    xn_ref[...] = _rmsnorm_rows(tile_input(), normw_ref[...]).astype(BF16)

    def project(dst_ref, dst_col, src_col):
        dst_ref[:, pl.ds(dst_col, PROJ_COLS)] = _dot(xn_ref[...], win_ref[:, pl.ds(src_col, PROJ_COLS)])

    def project_rec(dst_ref, group):
        return [functools.partial(project, dst_ref, part * PROJ_COLS, off + group * PROJ_COLS)
                for part, off in enumerate((OFF_QR, OFF_FR, OFF_IR, OFF_GR))]

    def out_proj(k0, k1, c0, first_half):
        cols = pl.ds(c0, PROJ_COLS)
        base = tile_input(cols) if first_half else hout_ref[0, :, cols]
        hout_ref[0, :, cols] = base + _dot(y_ref[:, k0:k1], wout_ref[k0:k1, cols])

    def column_blocks(n, body):
        def step(i, carry):
            body(pl.multiple_of(i * PROJ_COLS, PROJ_COLS))
            return carry
        lax.fori_loop(0, n, step, 0)

    for c0 in range(0, ATT_COLS, PROJ_COLS):
        project(za_ref, c0, c0)
    kv_ref[QBLK:QBLK + TILE, :] = za_ref[:, OFF_KA:OFF_KA + 2 * KV_DIM].astype(BF16)
    pk_ref[0] = za_ref[TILE - WINDOW:TILE, OFF_KA:OFF_KA + KV_DIM]
    pv_ref[0] = za_ref[TILE - WINDOW:TILE, OFF_VA:OFF_VA + KV_DIM]

    att_units = [_attention_unit(layer, qb, kh, za_ref, kv_ref, y_ref, bias_ref, sink_ref, c_tile)
                 for qb in range(TILE // QBLK) for kh in range(N_KV)]
    _emit_staggered(att_units, project_rec(zr_ref, 0))
    kv_ref[0:QBLK, :] = kv_ref[TILE:TILE + QBLK, :]

    n_groups = N_REC_HEADS // REC_GROUP
    z_bufs = [zr_ref, za_ref]
    spans = []

    def rec_units(grp, state_ref, spans_):
        return [_hgrn2_unit(ci, grp * REC_GROUP + hl, z_bufs[grp % 2], state_ref, y_ref, lb_ref, oml_ref,
                            recw_ref, c_tile, spans_)
                for ci in range(TILE // CHUNK) for hl in range(REC_GROUP)]

    for grp in range(n_groups):
        if grp + 1 < n_groups:
            fillers = project_rec(z_bufs[(grp + 1) % 2], grp + 1)
        else:
            fillers = [functools.partial(out_proj, 0, D_ATTN, c0, True) for c0 in range(0, D_MODEL, PROJ_COLS)]
        _emit_staggered(rec_units(grp, st_ref, spans), fillers)
    widest = jnp.max(jnp.concatenate(spans, axis=0))
    for c0 in range(0, D_MODEL, PROJ_COLS):
        out_proj(D_ATTN, D_MODEL, c0, False)

    @pl.when(widest > MAX_FACTOR_EXPONENT)
    def _():
        def replay_chunk(hh, ci, carry):
            for _ in _hgrn2_unit(ci, hh, zhead_ref, st_start_ref, y_ref, lb_ref, oml_ref, recw_ref, c_tile,
                                 spans=None, heads_in_buf=1):
                pass
            return carry

        def replay_head(hh, carry):
            for part, off in enumerate((OFF_QR, OFF_FR, OFF_IR, OFF_GR)):
                src = pl.multiple_of(off + hh * REC_HEAD_DIM, REC_HEAD_DIM)
                zhead_ref[:, part * REC_HEAD_DIM:(part + 1) * REC_HEAD_DIM] = _dot(
                    xn_ref[...], win_ref[:, pl.ds(src, REC_HEAD_DIM)])
            return lax.fori_loop(0, TILE // CHUNK, functools.partial(replay_chunk, hh), carry)

        lax.fori_loop(0, N_REC_HEADS, replay_head, 0)
        column_blocks(D_MODEL // PROJ_COLS, lambda c0: out_proj(0, D_MODEL, c0, True))

    ps_ref[0] = st_ref[...]

    @pl.when(is_meta)
    def _():
        kv_meta_ref[...] = kv_ref[0:QBLK, :]
        st_meta_ref[...] = st_ref[...]
        if not final:
            meta_out_ref[...] = hout_ref[0, PAD:TILE, :]

    if final:
        @pl.when(jnp.logical_not(is_meta))
        def _():
            hout_ref[0] = _rmsnorm_rows(hout_ref[0], finw_ref[...])


def _layer_block(shape, layer):
    nd = len(shape)
    return pl.BlockSpec((None,) + tuple(shape), lambda s: (layer,) + (0,) * nd, pipeline_mode=pl.Buffered(1))


def _const_block(shape):
    nd = len(shape)
    return pl.BlockSpec(tuple(shape), lambda s: (0,) * nd, pipeline_mode=pl.Buffered(1))


def _prompt_layer(layer, h_meta, h_main, w_in, w_out, norm_w, lb, oml, rec_w, sinks, bias, fin_w):
    final = layer == DEPTH - 1
    main = pl.BlockSpec((1, TILE, D_MODEL), lambda s: _main_tile(s) + (0,))
    meta = pl.BlockSpec((N_META, D_MODEL), lambda s: (0, 0))
    kv_spec = pl.BlockSpec((1, WINDOW, KV_DIM), lambda s: (_main_tile(s)[0], 0, 0))
    st_spec = pl.BlockSpec((1, N_REC_HEADS, REC_HEAD_DIM, REC_HEAD_DIM), lambda s: (_main_tile(s)[0], 0, 0, 0))
    main_shape = jax.ShapeDtypeStruct((BATCH, SEQ, D_MODEL), F32)
    meta_shape = jax.ShapeDtypeStruct((N_META, D_MODEL), F32)
    return pl.pallas_call(
        functools.partial(_prompt_layer_kernel, layer, final),
        grid=(1 + BATCH * ROW_TILES,),
        in_specs=[
            pl.BlockSpec(memory_space=pltpu.SMEM),
            _const_block((N_META, D_MODEL)),
            main,
            _const_block((D_MODEL, D_IN)),
            _const_block((D_MODEL, D_MODEL)),
            _layer_block((1, D_MODEL), layer),
            _layer_block((1, D_REC), layer),
            _layer_block((1, D_REC), layer),
            _layer_block((1, D_REC), layer),
            _const_block((N_HEADS * QBLK, 2 * QBLK)),
            _const_block((1, D_MODEL)),
        ],
        out_specs=([] if final else [meta]) + [main, kv_spec, kv_spec, st_spec],
        out_shape=([] if final else [meta_shape]) + [
            main_shape,
            jax.ShapeDtypeStruct((BATCH, WINDOW, KV_DIM), F32),
            jax.ShapeDtypeStruct((BATCH, WINDOW, KV_DIM), F32),
            jax.ShapeDtypeStruct((BATCH, N_REC_HEADS, REC_HEAD_DIM, REC_HEAD_DIM), F32),
        ],
        scratch_shapes=[
            pltpu.VMEM((TILE, D_MODEL), BF16),
            pltpu.VMEM((TILE, ATT_COLS), F32),
            pltpu.VMEM((TILE, 4 * PROJ_COLS), F32),
            pltpu.VMEM((QBLK + TILE, 2 * KV_DIM), BF16),
            pltpu.VMEM((N_REC_HEADS, REC_HEAD_DIM, REC_HEAD_DIM), F32),
            pltpu.VMEM((TILE, D_MODEL), BF16),
            pltpu.VMEM((QBLK, 2 * KV_DIM), BF16),
            pltpu.VMEM((N_REC_HEADS, REC_HEAD_DIM, REC_HEAD_DIM), F32),
            pltpu.VMEM((N_REC_HEADS, REC_HEAD_DIM, REC_HEAD_DIM), F32),
            pltpu.VMEM((TILE, 4 * REC_HEAD_DIM), F32),
        ],
        compiler_params=pltpu.CompilerParams(
            dimension_semantics=("arbitrary",), vmem_limit_bytes=VMEM_LIMIT),
        name="prompt_layer",
    )(sinks, h_meta, h_main, w_in, w_out, norm_w, lb, oml, rec_w, bias, fin_w)


DEC_PROJ_COLS = D_IN // 4
DEC_OUT_COLS = D_MODEL // 2
DEC_VMEM_LIMIT = 56 * 1024 * 1024


def _dec_proj_kernel(hs_ref, normw_ref, win_ref, z_ref, wb_ref):
    xn = _rmsnorm_rows(hs_ref[...], normw_ref[...]).astype(BF16)
    for c0 in range(0, DEC_PROJ_COLS, REC_HEAD_DIM):
        cols = slice(c0, c0 + REC_HEAD_DIM)
        wb = win_ref[:, cols].astype(BF16)
        wb_ref[:, cols] = wb
        z_ref[:, cols] = _dot(xn, wb)


def _dec_proj(layer, hs, norm_w, w_in):
    return pl.pallas_call(
        _dec_proj_kernel,
        grid=(D_IN // DEC_PROJ_COLS,),
        in_specs=[
            pl.BlockSpec((N_DEC, D_MODEL), lambda n: (0, 0)),
            pl.BlockSpec((None, 1, D_MODEL), lambda n: (layer, 0, 0)),
            pl.BlockSpec((None, D_MODEL, DEC_PROJ_COLS), lambda n: (layer, 0, n)),
        ],
        out_specs=[pl.BlockSpec((N_DEC, DEC_PROJ_COLS), lambda n: (0, n)),
                   pl.BlockSpec((D_MODEL, DEC_PROJ_COLS), lambda n: (0, n))],
        out_shape=[jax.ShapeDtypeStruct((N_DEC, D_IN), F32),
                   jax.ShapeDtypeStruct((D_MODEL, D_IN), BF16)],
        compiler_params=pltpu.CompilerParams(dimension_semantics=("arbitrary",), vmem_limit_bytes=DEC_VMEM_LIMIT),
        name="dec_proj",
    )(hs, norm_w, w_in)


def _dec_out_kernel(hs_ref, y_ref, wout_ref, o_ref, wb_ref):
    for c0 in range(0, DEC_OUT_COLS, 2 * REC_HEAD_DIM):
        cols = slice(c0, c0 + 2 * REC_HEAD_DIM)
        wb = wout_ref[:, cols].astype(BF16)
        wb_ref[:, cols] = wb
        o_ref[:, cols] = hs_ref[:, cols] + _dot(y_ref[...], wb)


def _dec_out(layer, hs, y, w_out):
    return pl.pallas_call(
        _dec_out_kernel,
        grid=(D_MODEL // DEC_OUT_COLS,),
        in_specs=[
            pl.BlockSpec((N_DEC, DEC_OUT_COLS), lambda n: (0, n)),
            pl.BlockSpec((N_DEC, D_MODEL), lambda n: (0, 0)),
            pl.BlockSpec((None, D_MODEL, DEC_OUT_COLS), lambda n: (layer, 0, n)),
        ],
        out_specs=[pl.BlockSpec((N_DEC, DEC_OUT_COLS), lambda n: (0, n)),
                   pl.BlockSpec((D_MODEL, DEC_OUT_COLS), lambda n: (0, n))],
        out_shape=[jax.ShapeDtypeStruct((N_DEC, D_MODEL), F32),
                   jax.ShapeDtypeStruct((D_MODEL, D_MODEL), BF16)],
        compiler_params=pltpu.CompilerParams(dimension_semantics=("arbitrary",), vmem_limit_bytes=DEC_VMEM_LIMIT),
        name="dec_out",
    )(hs, y, w_out)


DEC_SEQS_PER_STEP = 4
DEC_LAG = 8


def _dec_attention_unit(layer, q, kh, sink_ref, z_ref, ck_ref, cv_ref, biasc_ref, biasn_ref, y_ref):
    t = DEC_SEQ
    tok = slice(q * t, (q + 1) * t)
    hs_ = slice(kh * HEAD_DIM, (kh + 1) * HEAD_DIM)
    kc = ck_ref[q, :, hs_].astype(BF16)
    kn = z_ref[tok, OFF_KA + kh * HEAD_DIM:OFF_KA + (kh + 1) * HEAD_DIM]
    q4 = jnp.concatenate(
        [z_ref[tok, (kh * GROUP + g) * HEAD_DIM:(kh * GROUP + g + 1) * HEAD_DIM] for g in range(GROUP)],
        axis=0) * (HEAD_DIM ** -0.5)
    rows = slice(kh * GROUP * t, (kh + 1) * GROUP * t)
    sc = _dot_nt(q4.astype(BF16), kc)
    sn = _dot_nt(q4, kn)
    yield
    sc = sc + biasc_ref[rows, :]
    sn = sn + biasn_ref[rows, :]
    ecs, ens, dens = [], [], []
    for g in range(GROUP):
        sink = sink_ref[layer, kh * GROUP + g]
        scg = sc[g * t:(g + 1) * t]
        sng = sn[g * t:(g + 1) * t]
        m = jnp.maximum(jnp.maximum(jnp.max(scg, axis=-1, keepdims=True),
                                    jnp.max(sng, axis=-1, keepdims=True)), sink)
        ec = jnp.exp(scg - m)
        en = jnp.exp(sng - m)
        dens.append(jnp.sum(ec, axis=-1, keepdims=True) + jnp.sum(en, axis=-1, keepdims=True)
                    + jnp.exp(sink - m))
        ecs.append(ec)
        ens.append(en)
    vc = cv_ref[q, :, hs_]
    vn = z_ref[tok, OFF_VA + kh * HEAD_DIM:OFF_VA + (kh + 1) * HEAD_DIM]
    o4 = (_dot(jnp.concatenate(ecs, axis=0).astype(BF16), vc.astype(BF16))
          + _dot(jnp.concatenate(ens, axis=0), vn))
    yield
    outs = []
    for g in range(GROUP):
        h = kh * GROUP + g
        og = o4[g * t:(g + 1) * t] / dens[g]
        ga = z_ref[tok, OFF_GA + h * HEAD_DIM:OFF_GA + (h + 1) * HEAD_DIM]
        outs.append(og * _silu(ga))
    y_ref[tok, kh * GROUP * HEAD_DIM:(kh + 1) * GROUP * HEAD_DIM] = jnp.concatenate(outs, axis=-1).astype(BF16)


def _dec_hgrn2_unit(q, hh, z_ref, s0_ref, lb_ref, oml_ref, recw_ref, y_ref, ns_ref):
    t = DEC_SEQ
    tok = slice(q * t, (q + 1) * t)
    row = lax.broadcasted_iota(jnp.int32, (t, t), 0)
    col = lax.broadcasted_iota(jnp.int32, (t, t), 1)
    causal = col <= row
    cs = hh * REC_HEAD_DIM
    lb = lb_ref[:, cs:cs + REC_HEAD_DIM]
    oml = oml_ref[:, cs:cs + REC_HEAD_DIM]
    logf, kr = _gate_features(z_ref[tok, OFF_FR + cs:OFF_FR + cs + REC_HEAD_DIM], lb, oml)
    qv = _silu(z_ref[tok, OFF_QR + cs:OFF_QR + cs + REC_HEAD_DIM])
    v = z_ref[tok, OFF_IR + cs:OFF_IR + cs + REC_HEAD_DIM]
    a = _cumsum_rows(logf)
    a_end = a[t - 1:t, :]
    scores = _decay_scores_direct(qv, kr, a)
    qt = qv * jnp.exp(a)
    kt = kr * jnp.exp(a_end - a)
    s0 = s0_ref[q, hh]
    o_state = _dot(qt, s0)
    yield
    att = jnp.where(causal, scores, 0.0)
    o = _dot(att, v) + o_state
    ktd = jnp.concatenate([kt, jnp.exp(a_end), jnp.zeros((t - 1, REC_HEAD_DIM), F32)], axis=0).T
    ns_ref[q, hh] = s0 * ktd[:, t:t + 1] + _dot(ktd[:, 0:t], v)
    yield
    on = o * lax.rsqrt(jnp.mean(o * o, axis=-1, keepdims=True) + EPS)
    gr = z_ref[tok, OFF_GR + cs:OFF_GR + cs + REC_HEAD_DIM]
    yr = (on * recw_ref[:, cs:cs + REC_HEAD_DIM]) * _silu(gr)
    y_ref[tok, D_ATTN + cs:D_ATTN + cs + REC_HEAD_DIM] = yr.astype(BF16)


def _dec_mix_kernel(layer, sink_ref, z_ref, ck_ref, cv_ref, s0_ref, lb_ref, oml_ref, recw_ref, biasc_ref,
                    biasn_ref, y_ref, nk_ref, nv_ref, ns_ref):
    t = DEC_SEQ
    for q in range(DEC_SEQS_PER_STEP):
        tok = slice(q * t, (q + 1) * t)
        nk_ref[q, 0:WINDOW - t, :] = ck_ref[q, t:WINDOW, :]
        nk_ref[q, WINDOW - t:WINDOW, :] = z_ref[tok, OFF_KA:OFF_KA + KV_DIM]
        nv_ref[q, 0:WINDOW - t, :] = cv_ref[q, t:WINDOW, :]
        nv_ref[q, WINDOW - t:WINDOW, :] = z_ref[tok, OFF_VA:OFF_VA + KV_DIM]
    units = []
    for kh in range(N_KV):
        units += [_dec_attention_unit(layer, q, kh, sink_ref, z_ref, ck_ref, cv_ref, biasc_ref, biasn_ref, y_ref)
                  for q in range(DEC_SEQS_PER_STEP)]
    for hh in range(N_REC_HEADS):
        units += [_dec_hgrn2_unit(q, hh, z_ref, s0_ref, lb_ref, oml_ref, recw_ref, y_ref, ns_ref)
                  for q in range(DEC_SEQS_PER_STEP)]
    _emit_staggered(units, lag=DEC_LAG)


def _dec_mix(layer, z, ck, cv, s0, lb, oml, rec_w, sinks, bias_c, bias_n):
    t = DEC_SEQ
    nq = DEC_SEQS_PER_STEP
    const2 = lambda s: (0, 0)
    per_layer = lambda s: (layer, 0, 0)
    kv_in = pl.BlockSpec((None, nq, WINDOW, KV_DIM), lambda s: (layer, s, 0, 0))
    st_in = pl.BlockSpec((None, nq, N_REC_HEADS, REC_HEAD_DIM, REC_HEAD_DIM), lambda s: (layer, s, 0, 0, 0))
    return pl.pallas_call(
        functools.partial(_dec_mix_kernel, layer),
        grid=(DEC_BATCH // nq,),
        in_specs=[
            pl.BlockSpec(memory_space=pltpu.SMEM),
            pl.BlockSpec((nq * t, D_IN), lambda s: (s, 0)),
            kv_in,
            kv_in,
            st_in,
            pl.BlockSpec((None, 1, D_REC), per_layer),
            pl.BlockSpec((None, 1, D_REC), per_layer),
            pl.BlockSpec((None, 1, D_REC), per_layer),
            pl.BlockSpec((N_HEADS * t, WINDOW), const2),
            pl.BlockSpec((N_HEADS * t, t), const2),
        ],
        out_specs=[
            pl.BlockSpec((nq * t, D_MODEL), lambda s: (s, 0)),
            pl.BlockSpec((nq, WINDOW, KV_DIM), lambda s: (s, 0, 0)),
            pl.BlockSpec((nq, WINDOW, KV_DIM), lambda s: (s, 0, 0)),
            pl.BlockSpec((nq, N_REC_HEADS, REC_HEAD_DIM, REC_HEAD_DIM), lambda s: (s, 0, 0, 0)),
        ],
        out_shape=[
            jax.ShapeDtypeStruct((N_DEC, D_MODEL), BF16),
            jax.ShapeDtypeStruct((DEC_BATCH, WINDOW, KV_DIM), F32),
            jax.ShapeDtypeStruct((DEC_BATCH, WINDOW, KV_DIM), F32),
            jax.ShapeDtypeStruct((DEC_BATCH, N_REC_HEADS, REC_HEAD_DIM, REC_HEAD_DIM), F32),
        ],
        compiler_params=pltpu.CompilerParams(dimension_semantics=("arbitrary",)),
        name="dec_mix",
    )(sinks, z, ck, cv, s0, lb, oml, rec_w, bias_c, bias_n)


def _final_norm_kernel(x_ref, w_ref, o_ref):
    o_ref[...] = _rmsnorm_rows(x_ref[...], w_ref[...])


def _final_norm_dec(hs, w):
    return pl.pallas_call(
        _final_norm_kernel,
        out_shape=jax.ShapeDtypeStruct((N_DEC, D_MODEL), F32),
        name="final_norm_dec",
    )(hs, w)


def _t5_bucket(dist):
    max_exact = N_BUCKETS // 2
    d = jnp.maximum(dist, 0)
    df = jnp.maximum(d, 1).astype(F32)
    large = max_exact + (jnp.log(df / max_exact) / math.log(MAX_DISTANCE / max_exact)
                         * (N_BUCKETS - max_exact)).astype(jnp.int32)
    large = jnp.minimum(large, N_BUCKETS - 1)
    return jnp.where(d < max_exact, d, large)


def _bias_tables(rel_bias_table):
    bias_d = rel_bias_table[_t5_bucket(jnp.arange(WINDOW))].T.astype(F32)
    period = 3 * WINDOW + 1
    u = jnp.concatenate([jnp.full((N_HEADS, 1), NEG, F32), bias_d[:, ::-1],
                         jnp.full((N_HEADS, period - 1 - WINDOW), NEG, F32)], axis=1)
    rows = jnp.tile(u, (1, WINDOW))[:, :WINDOW * (period - 1)].reshape(N_HEADS, WINDOW, period - 1)
    prompt = rows[:, :, :2 * WINDOW].reshape(N_HEADS * WINDOW, 2 * WINDOW).astype(BF16)
    dec = rows[:, :DEC_SEQ, :WINDOW + DEC_SEQ].reshape(N_HEADS * DEC_SEQ, WINDOW + DEC_SEQ)
    return prompt, dec[:, :WINDOW], dec[:, WINDOW:]


def kernel(x_prompt, x_sample, cache_k, cache_v, state_h, meta_tokens, w_in, w_out, norm_w, final_norm_w,
           attn_sinks, rel_bias_table, hgrn_lb_logits, hgrn_norm_w):
    pl_ = jax.nn.softmax(hgrn_lb_logits.astype(F32), axis=0)
    lb = (jnp.cumsum(pl_, axis=0) - pl_[0:1]).reshape(DEPTH, 1, D_REC)
    oml = 1.0 - lb
    bias_p, bias_dc, bias_dn = _bias_tables(rel_bias_table)
    nw = norm_w.astype(F32).reshape(DEPTH, 1, D_MODEL)
    rw = hgrn_norm_w.astype(F32).reshape(DEPTH, 1, D_REC)
    fw = final_norm_w.astype(F32).reshape(1, D_MODEL)
    sinks = attn_sinks.astype(F32)

    h_meta = meta_tokens.astype(F32)
    h_main = x_prompt
    hs = x_sample.reshape(N_DEC, D_MODEL)
    ck = cache_k.reshape(DEPTH, DEC_BATCH, WINDOW, KV_DIM)
    cv = cache_v.reshape(DEPTH, DEC_BATCH, WINDOW, KV_DIM)

    pk, pv, ps, sk, sv, ss = [], [], [], [], [], []
    for l in range(DEPTH):
        z, w_in_b = _dec_proj(l, hs, nw, w_in)
        y, nk, nv, ns = _dec_mix(l, z, ck, cv, state_h, lb, oml, rw, sinks, bias_dc, bias_dn)
        hs, w_out_b = _dec_out(l, hs, y, w_out)
        sk.append(nk)
        sv.append(nv)
        ss.append(ns)

        outs = _prompt_layer(l, h_meta, h_main, w_in_b, w_out_b, nw, lb, oml, rw, sinks, bias_p, fw)
        if l < DEPTH - 1:
            h_meta, h_main, k_l, v_l, s_l = outs
        else:
            h_main, k_l, v_l, s_l = outs
        pk.append(k_l)
        pv.append(v_l)
        ps.append(s_l)

    y_prompt = h_main
    y_sample = _final_norm_dec(hs, fw).reshape(DEC_BATCH, DEC_SEQ, D_MODEL)
    kv_shape = (DEPTH, -1, WINDOW, N_KV, HEAD_DIM)
    return (y_prompt, y_sample,
            jnp.stack(pk).reshape(kv_shape), jnp.stack(pv).reshape(kv_shape),
            jnp.swapaxes(jnp.stack(ps), -1, -2),
            jnp.stack(sk).reshape(kv_shape), jnp.stack(sv).reshape(kv_shape), jnp.stack(ss))
```

```python
import functools
import math

import jax
import jax.numpy as jnp
from jax import lax
from jax.experimental import pallas as pl
from jax.experimental.pallas import tpu as pltpu

D_MODEL = 2048
BATCH = 4
SEQ = 2048
DEPTH = 4
DEC_BATCH = 32
DEC_SEQ = 8
N_META = 16
D_ATTN = 1024
D_REC = 1024
HEAD_DIM = 64
N_HEADS = 16
N_KV = 4
GROUP = 4
KV_DIM = 256
WINDOW = 128
N_BUCKETS = 32
MAX_DISTANCE = 128
REC_HEAD_DIM = 128
N_REC_HEADS = 8
EPS = 1e-6
D_IN = 2 * D_ATTN + 2 * KV_DIM + 4 * D_REC

OFF_QA = 0
OFF_KA = D_ATTN
OFF_VA = OFF_KA + KV_DIM
OFF_GA = OFF_VA + KV_DIM
OFF_QR = OFF_GA + D_ATTN
OFF_FR = OFF_QR + D_REC
OFF_IR = OFF_FR + D_REC
OFF_GR = OFF_IR + D_REC
ATT_COLS = OFF_QR

TILE = 256
QBLK = WINDOW
CHUNK = 128
HALF = CHUNK // 2
ROW_TILES = SEQ // TILE
PAD = TILE - N_META
PROJ_COLS = 512
REC_GROUP = PROJ_COLS // REC_HEAD_DIM
NEG = -1e30
MAX_FACTOR_EXPONENT = 80.0
N_DEC = DEC_BATCH * DEC_SEQ
V7X_VMEM_BYTES = 64 * 1024 * 1024
VMEM_LIMIT = V7X_VMEM_BYTES - 1024 * 1024

F32 = jnp.float32
BF16 = jnp.bfloat16


def _sigmoid_pair(x):
    e = jnp.exp(-jnp.abs(x))
    inv = 1.0 / (1.0 + e)
    pos = x >= 0
    return jnp.where(pos, inv, e * inv), jnp.where(pos, e * inv, inv)


def _silu(x):
    return x * (0.5 * jnp.tanh(0.5 * x) + 0.5)


def _cumsum_rows(x):
    n = x.shape[0]
    row = lax.broadcasted_iota(jnp.int32, x.shape, 0)
    k = 1
    while k < n:
        x = x + jnp.where(row >= k, pltpu.roll(x, k, axis=0), 0.0)
        k *= 2
    return x


def _rmsnorm_rows(x, w):
    ms = jnp.mean(x * x, axis=-1, keepdims=True)
    return (x * lax.rsqrt(ms + EPS)) * w


def _dot_nt(a, b):
    return lax.dot_general(a, b, (((1,), (1,)), ((), ())), preferred_element_type=F32)


def _dot_tn(a, b):
    return lax.dot_general(a, b, (((0,), (0,)), ((), ())), preferred_element_type=F32)


def _dot(a, b):
    return jnp.dot(a, b, preferred_element_type=F32)


def _gate_features(fr, lb, oml):
    sig, sneg = _sigmoid_pair(fr)
    logf = jnp.log(lb + oml * sig)
    return logf, oml * sneg


def _attention_unit(layer, qb, kh, za_ref, kv_ref, y_ref, bias_ref, sink_ref, c_tile):
    r0 = qb * QBLK
    kpos = c_tile + r0 - QBLK + lax.broadcasted_iota(jnp.int32, (1, 2 * QBLK), 1)
    kvalid = kpos >= 0
    kk = kv_ref[r0:r0 + 2 * QBLK, kh * HEAD_DIM:(kh + 1) * HEAD_DIM]
    q4 = jnp.concatenate(
        [za_ref[r0:r0 + QBLK, (kh * GROUP + g) * HEAD_DIM:(kh * GROUP + g + 1) * HEAD_DIM]
         for g in range(GROUP)], axis=0)
    s = _dot_nt((q4 * (HEAD_DIM ** -0.5)).astype(BF16), kk)
    yield
    s = s + bias_ref[kh * GROUP * QBLK:(kh + 1) * GROUP * QBLK, :].astype(F32)
    s = jnp.where(kvalid, s, NEG)
    es, dens = [], []
    for g in range(GROUP):
        sink = sink_ref[layer, kh * GROUP + g]
        sg = s[g * QBLK:(g + 1) * QBLK]
        m = jnp.maximum(jnp.max(sg, axis=-1, keepdims=True), sink)
        e = jnp.exp(sg - m)
        dens.append(jnp.sum(e, axis=-1, keepdims=True) + jnp.exp(sink - m))
        es.append(e.astype(BF16))
    vv = kv_ref[r0:r0 + 2 * QBLK, KV_DIM + kh * HEAD_DIM:KV_DIM + (kh + 1) * HEAD_DIM]
    o4 = _dot(jnp.concatenate(es, axis=0), vv)
    yield
    outs = []
    for g in range(GROUP):
        h = kh * GROUP + g
        og = o4[g * QBLK:(g + 1) * QBLK] / dens[g]
        ga = za_ref[r0:r0 + QBLK, OFF_GA + h * HEAD_DIM:OFF_GA + (h + 1) * HEAD_DIM]
        outs.append(og * _silu(ga))
    y_ref[r0:r0 + QBLK, kh * GROUP * HEAD_DIM:(kh + 1) * GROUP * HEAD_DIM] = (
        jnp.concatenate(outs, axis=-1).astype(BF16))


def _decay_scores_direct(q, kr, a):
    n = q.shape[0]
    row = lax.broadcasted_iota(jnp.int32, (n, n), 0)
    col = lax.broadcasted_iota(jnp.int32, (n, n), 1)

    def diagonal(d, acc):
        ks = pltpu.roll(kr, d, axis=0)
        a_s = pltpu.roll(a, d, axis=0)
        p = q * ks * jnp.exp(jnp.minimum(a - a_s, 0.0))
        return jnp.where(col == row - d, jnp.sum(p, axis=-1, keepdims=True), acc)

    acc = jnp.zeros((n, n), F32)
    if n <= DEC_SEQ:
        for d in range(n):
            acc = diagonal(d, acc)
        return acc
    return lax.fori_loop(0, n, diagonal, acc)


def _hgrn2_unit(ci, hh, zr_ref, st_ref, y_ref, lb_ref, oml_ref, recw_ref, c_tile, spans=None, heads_in_buf=None):
    heads_in_buf = REC_GROUP if heads_in_buf is None else heads_in_buf
    part = heads_in_buf * REC_HEAD_DIM
    row = lax.broadcasted_iota(jnp.int32, (CHUNK, CHUNK), 0)
    col = lax.broadcasted_iota(jnp.int32, (CHUNK, CHUNK), 1)
    causal = col <= row
    r0 = ci * CHUNK if isinstance(ci, int) else pl.multiple_of(ci * CHUNK, CHUNK)
    rows = pl.ds(r0, CHUNK)
    rpos = c_tile + r0 + lax.broadcasted_iota(jnp.int32, (CHUNK, 1), 0)
    rvalid = rpos >= 0
    if isinstance(hh, int):
        cs = hh * REC_HEAD_DIM
        zc = (hh % heads_in_buf) * REC_HEAD_DIM
    else:
        cs = pl.multiple_of(hh * REC_HEAD_DIM, REC_HEAD_DIM)
        zc = 0
    head_cols = pl.ds(cs, REC_HEAD_DIM)
    lb = lb_ref[:, head_cols]
    oml = oml_ref[:, head_cols]
    logf, kr = _gate_features(zr_ref[rows, part + zc:part + zc + REC_HEAD_DIM], lb, oml)
    q = _silu(zr_ref[rows, zc:zc + REC_HEAD_DIM])
    v = jnp.where(rvalid, zr_ref[rows, 2 * part + zc:2 * part + zc + REC_HEAD_DIM], 0.0)
    vb = v.astype(BF16)
    a = _cumsum_rows(logf)
    a_mid = a[HALF - 1:HALF, :]
    a_end = a[CHUNK - 1:CHUNK, :]
    if spans is None:
        scores = _decay_scores_direct(q, kr, a)
    else:
        qh = (q * jnp.exp(a - a_mid)).astype(BF16)
        kh_ = (kr * jnp.exp(a_mid - a)).astype(BF16)
        scores = _dot_nt(qh, kh_)
        spans.append(jnp.maximum(a[0:1, :] - a_mid, a_mid - a_end))
    qt = (q * jnp.exp(a)).astype(BF16)
    kt = (kr * jnp.exp(a_end - a)).astype(BF16)
    st = st_ref[hh]
    o_state = _dot_nt(qt, st.astype(BF16))
    yield
    att = jnp.where(causal, scores, 0.0)
    o = _dot(att.astype(BF16), vb) + o_state
    st_ref[hh] = st * jnp.exp(a_end) + _dot_tn(vb, kt)
    yield
    on = o * lax.rsqrt(jnp.mean(o * o, axis=-1, keepdims=True) + EPS)
    gr = zr_ref[rows, 3 * part + zc:3 * part + zc + REC_HEAD_DIM]
    yr = (on * recw_ref[:, head_cols]) * _silu(gr)
    y_ref[rows, pl.ds(D_ATTN + cs, REC_HEAD_DIM)] = yr.astype(BF16)


def _emit_staggered(units, fillers=(), lag=1):
    n = len(units)
    rounds = n + 2 * lag
    at_round = [[] for _ in range(rounds)]
    for k, f in enumerate(fillers):
        at_round[k * rounds // len(fillers)].append(f)
    for i in range(rounds):
        if i < n:
            next(units[i])
        for f in at_round[i]:
            f()
        if 0 <= i - lag < n:
            next(units[i - lag])
        if 0 <= i - 2 * lag < n:
            next(units[i - 2 * lag], None)


def _main_tile(s):
    t = jnp.maximum(s - 1, 0)
    return lax.div(t, ROW_TILES), lax.rem(t, ROW_TILES)


def _prompt_layer_kernel(layer, final, sink_ref, meta_ref, hin_ref, win_ref, wout_ref, normw_ref, lb_ref, oml_ref,
                         recw_ref, bias_ref, finw_ref, *refs):
    if final:
        hout_ref, pk_ref, pv_ref, ps_ref = refs[:4]
        refs = refs[4:]
    else:
        meta_out_ref, hout_ref, pk_ref, pv_ref, ps_ref = refs[:5]
        refs = refs[5:]
    xn_ref, za_ref, zr_ref, kv_ref, st_ref, y_ref, kv_meta_ref, st_meta_ref, st_start_ref, zhead_ref = refs
    s = pl.program_id(0)
    is_meta = s == 0
    _, j = _main_tile(s)

    @pl.when(is_meta)
    def _():
        kv_ref[0:QBLK, :] = jnp.zeros((QBLK, 2 * KV_DIM), BF16)
        st_ref[...] = jnp.zeros(st_ref.shape, F32)

    @pl.when(jnp.logical_and(jnp.logical_not(is_meta), j == 0))
    def _():
        kv_ref[0:QBLK, :] = kv_meta_ref[...]
        st_ref[...] = st_meta_ref[...]

    st_start_ref[...] = st_ref[...]

    def tile_input(cols=slice(None)):
        meta = meta_ref[:, cols]
        meta_rows = jnp.concatenate([jnp.zeros((PAD, meta.shape[1]), F32), meta], axis=0)
        return jnp.where(is_meta, meta_rows, hin_ref[0, :, cols])

    c_tile = jnp.where(is_meta, -PAD, j * TILE + N_META)
    xn_ref[...] = _rmsnorm_rows(tile_input(), normw_ref[...]).astype(BF16)

    def project(dst_ref, dst_col, src_col):
        dst_ref[:, pl.ds(dst_col, PROJ_COLS)] = _dot(xn_ref[...], win_ref[:, pl.ds(src_col, PROJ_COLS)])

    def project_rec(dst_ref, group):
        return [functools.partial(project, dst_ref, part * PROJ_COLS, off + group * PROJ_COLS)
                for part, off in enumerate((OFF_QR, OFF_FR, OFF_IR, OFF_GR))]

    def out_proj(k0, k1, c0, first_half):
        cols = pl.ds(c0, PROJ_COLS)
        base = tile_input(cols) if first_half else hout_ref[0, :, cols]
        hout_ref[0, :, cols] = base + _dot(y_ref[:, k0:k1], wout_ref[k0:k1, cols])

    def column_blocks(n, body):
        def step(i, carry):
            body(pl.multiple_of(i * PROJ_COLS, PROJ_COLS))
            return carry
        lax.fori_loop(0, n, step, 0)

    for c0 in range(0, ATT_COLS, PROJ_COLS):
        project(za_ref, c0, c0)
    kv_ref[QBLK:QBLK + TILE, :] = za_ref[:, OFF_KA:OFF_KA + 2 * KV_DIM].astype(BF16)
    pk_ref[0] = za_ref[TILE - WINDOW:TILE, OFF_KA:OFF_KA + KV_DIM]
    pv_ref[0] = za_ref[TILE - WINDOW:TILE, OFF_VA:OFF_VA + KV_DIM]

    att_units = [_attention_unit(layer, qb, kh, za_ref, kv_ref, y_ref, bias_ref, sink_ref, c_tile)
                 for qb in range(TILE // QBLK) for kh in range(N_KV)]
    _emit_staggered(att_units, project_rec(zr_ref, 0))
    kv_ref[0:QBLK, :] = kv_ref[TILE:TILE + QBLK, :]

    n_groups = N_REC_HEADS // REC_GROUP
    z_bufs = [zr_ref, za_ref]
    spans = []

    def rec_units(grp, state_ref, spans_):
        return [_hgrn2_unit(ci, grp * REC_GROUP + hl, z_bufs[grp % 2], state_ref, y_ref, lb_ref, oml_ref,
                            recw_ref, c_tile, spans_)
                for ci in range(TILE // CHUNK) for hl in range(REC_GROUP)]

    for grp in range(n_groups):
        if grp + 1 < n_groups:
            fillers = project_rec(z_bufs[(grp + 1) % 2], grp + 1)
        else:
            fillers = [functools.partial(out_proj, 0, D_ATTN, c0, True) for c0 in range(0, D_MODEL, PROJ_COLS)]
        _emit_staggered(rec_units(grp, st_ref, spans), fillers)
    widest = jnp.max(jnp.concatenate(spans, axis=0))
    for c0 in range(0, D_MODEL, PROJ_COLS):
        out_proj(D_ATTN, D_MODEL, c0, False)

    @pl.when(widest > MAX_FACTOR_EXPONENT)
    def _():
        def replay_chunk(hh, ci, carry):
            for _ in _hgrn2_unit(ci, hh, zhead_ref, st_start_ref, y_ref, lb_ref, oml_ref, recw_ref, c_tile,
                                 spans=None, heads_in_buf=1):
                pass
            return carry

        def replay_head(hh, carry):
            for part, off in enumerate((OFF_QR, OFF_FR, OFF_IR, OFF_GR)):
                src = pl.multiple_of(off + hh * REC_HEAD_DIM, REC_HEAD_DIM)
                zhead_ref[:, part * REC_HEAD_DIM:(part + 1) * REC_HEAD_DIM] = _dot(
                    xn_ref[...], win_ref[:, pl.ds(src, REC_HEAD_DIM)])
            return lax.fori_loop(0, TILE // CHUNK, functools.partial(replay_chunk, hh), carry)

        lax.fori_loop(0, N_REC_HEADS, replay_head, 0)
        column_blocks(D_MODEL // PROJ_COLS, lambda c0: out_proj(0, D_MODEL, c0, True))

    ps_ref[0] = st_ref[...]

    @pl.when(is_meta)
    def _():
        kv_meta_ref[...] = kv_ref[0:QBLK, :]
        st_meta_ref[...] = st_ref[...]
        if not final:
            meta_out_ref[...] = hout_ref[0, PAD:TILE, :]

    if final:
        @pl.when(jnp.logical_not(is_meta))
        def _():
            hout_ref[0] = _rmsnorm_rows(hout_ref[0], finw_ref[...])


def _layer_block(shape, layer):
    nd = len(shape)
    return pl.BlockSpec((None,) + tuple(shape), lambda s: (layer,) + (0,) * nd, pipeline_mode=pl.Buffered(1))


def _const_block(shape):
    nd = len(shape)
    return pl.BlockSpec(tuple(shape), lambda s: (0,) * nd, pipeline_mode=pl.Buffered(1))


def _prompt_layer(layer, h_meta, h_main, w_in, w_out, norm_w, lb, oml, rec_w, sinks, bias, fin_w):
    final = layer == DEPTH - 1
    main = pl.BlockSpec((1, TILE, D_MODEL), lambda s: _main_tile(s) + (0,))
    meta = pl.BlockSpec((N_META, D_MODEL), lambda s: (0, 0))
    kv_spec = pl.BlockSpec((1, WINDOW, KV_DIM), lambda s: (_main_tile(s)[0], 0, 0))
    st_spec = pl.BlockSpec((1, N_REC_HEADS, REC_HEAD_DIM, REC_HEAD_DIM), lambda s: (_main_tile(s)[0], 0, 0, 0))
    main_shape = jax.ShapeDtypeStruct((BATCH, SEQ, D_MODEL), F32)
    meta_shape = jax.ShapeDtypeStruct((N_META, D_MODEL), F32)
    return pl.pallas_call(
        functools.partial(_prompt_layer_kernel, layer, final),
        grid=(1 + BATCH * ROW_TILES,),
        in_specs=[
            pl.BlockSpec(memory_space=pltpu.SMEM),
            _const_block((N_META, D_MODEL)),
            main,
            _const_block((D_MODEL, D_IN)),
            _const_block((D_MODEL, D_MODEL)),
            _layer_block((1, D_MODEL), layer),
            _layer_block((1, D_REC), layer),
            _layer_block((1, D_REC), layer),
            _layer_block((1, D_REC), layer),
            _const_block((N_HEADS * QBLK, 2 * QBLK)),
            _const_block((1, D_MODEL)),
        ],
        out_specs=([] if final else [meta]) + [main, kv_spec, kv_spec, st_spec],
        out_shape=([] if final else [meta_shape]) + [
            main_shape,
            jax.ShapeDtypeStruct((BATCH, WINDOW, KV_DIM), F32),
            jax.ShapeDtypeStruct((BATCH, WINDOW, KV_DIM), F32),
            jax.ShapeDtypeStruct((BATCH, N_REC_HEADS, REC_HEAD_DIM, REC_HEAD_DIM), F32),
        ],
        scratch_shapes=[
            pltpu.VMEM((TILE, D_MODEL), BF16),
            pltpu.VMEM((TILE, ATT_COLS), F32),
            pltpu.VMEM((TILE, 4 * PROJ_COLS), F32),
            pltpu.VMEM((QBLK + TILE, 2 * KV_DIM), BF16),
            pltpu.VMEM((N_REC_HEADS, REC_HEAD_DIM, REC_HEAD_DIM), F32),
            pltpu.VMEM((TILE, D_MODEL), BF16),
            pltpu.VMEM((QBLK, 2 * KV_DIM), BF16),
            pltpu.VMEM((N_REC_HEADS, REC_HEAD_DIM, REC_HEAD_DIM), F32),
            pltpu.VMEM((N_REC_HEADS, REC_HEAD_DIM, REC_HEAD_DIM), F32),
            pltpu.VMEM((TILE, 4 * REC_HEAD_DIM), F32),
        ],
        compiler_params=pltpu.CompilerParams(
            dimension_semantics=("arbitrary",), vmem_limit_bytes=VMEM_LIMIT),
        name="prompt_layer",
    )(sinks, h_meta, h_main, w_in, w_out, norm_w, lb, oml, rec_w, bias, fin_w)


DEC_PROJ_COLS = D_IN // 4
DEC_OUT_COLS = D_MODEL // 2
DEC_VMEM_LIMIT = 56 * 1024 * 1024


def _dec_proj_kernel(hs_ref, normw_ref, win_ref, z_ref, wb_ref):
    xn = _rmsnorm_rows(hs_ref[...], normw_ref[...]).astype(BF16)
    for c0 in range(0, DEC_PROJ_COLS, REC_HEAD_DIM):
        cols = slice(c0, c0 + REC_HEAD_DIM)
        wb = win_ref[:, cols].astype(BF16)
        wb_ref[:, cols] = wb
        z_ref[:, cols] = _dot(xn, wb)


def _dec_proj(layer, hs, norm_w, w_in):
    return pl.pallas_call(
        _dec_proj_kernel,
        grid=(D_IN // DEC_PROJ_COLS,),
        in_specs=[
            pl.BlockSpec((N_DEC, D_MODEL), lambda n: (0, 0)),
            pl.BlockSpec((None, 1, D_MODEL), lambda n: (layer, 0, 0)),
            pl.BlockSpec((None, D_MODEL, DEC_PROJ_COLS), lambda n: (layer, 0, n)),
        ],
        out_specs=[pl.BlockSpec((N_DEC, DEC_PROJ_COLS), lambda n: (0, n)),
                   pl.BlockSpec((D_MODEL, DEC_PROJ_COLS), lambda n: (0, n))],
        out_shape=[jax.ShapeDtypeStruct((N_DEC, D_IN), F32),
                   jax.ShapeDtypeStruct((D_MODEL, D_IN), BF16)],
        compiler_params=pltpu.CompilerParams(dimension_semantics=("arbitrary",), vmem_limit_bytes=DEC_VMEM_LIMIT),
        name="dec_proj",
    )(hs, norm_w, w_in)


def _dec_out_kernel(hs_ref, y_ref, wout_ref, o_ref, wb_ref):
    for c0 in range(0, DEC_OUT_COLS, 2 * REC_HEAD_DIM):
        cols = slice(c0, c0 + 2 * REC_HEAD_DIM)
        wb = wout_ref[:, cols].astype(BF16)
        wb_ref[:, cols] = wb
        o_ref[:, cols] = hs_ref[:, cols] + _dot(y_ref[...], wb)


def _dec_out(layer, hs, y, w_out):
    return pl.pallas_call(
        _dec_out_kernel,
        grid=(D_MODEL // DEC_OUT_COLS,),
        in_specs=[
            pl.BlockSpec((N_DEC, DEC_OUT_COLS), lambda n: (0, n)),
            pl.BlockSpec((N_DEC, D_MODEL), lambda n: (0, 0)),
            pl.BlockSpec((None, D_MODEL, DEC_OUT_COLS), lambda n: (layer, 0, n)),
        ],
        out_specs=[pl.BlockSpec((N_DEC, DEC_OUT_COLS), lambda n: (0, n)),
                   pl.BlockSpec((D_MODEL, DEC_OUT_COLS), lambda n: (0, n))],
        out_shape=[jax.ShapeDtypeStruct((N_DEC, D_MODEL), F32),
                   jax.ShapeDtypeStruct((D_MODEL, D_MODEL), BF16)],
        compiler_params=pltpu.CompilerParams(dimension_semantics=("arbitrary",), vmem_limit_bytes=DEC_VMEM_LIMIT),
        name="dec_out",
    )(hs, y, w_out)


DEC_SEQS_PER_STEP = 4
DEC_LAG = DEC_SEQS_PER_STEP * (N_KV + N_REC_HEADS)


def _dec_attention_unit(layer, q, kh, sink_ref, z_ref, ck_ref, cv_ref, biasc_ref, biasn_ref, y_ref):
    t = DEC_SEQ
    tok = slice(q * t, (q + 1) * t)
    hs_ = slice(kh * HEAD_DIM, (kh + 1) * HEAD_DIM)
    kc = ck_ref[q, :, hs_].astype(BF16)
    kn = z_ref[tok, OFF_KA + kh * HEAD_DIM:OFF_KA + (kh + 1) * HEAD_DIM]
    q4 = jnp.concatenate(
        [z_ref[tok, (kh * GROUP + g) * HEAD_DIM:(kh * GROUP + g + 1) * HEAD_DIM] for g in range(GROUP)],
        axis=0) * (HEAD_DIM ** -0.5)
    rows = slice(kh * GROUP * t, (kh + 1) * GROUP * t)
    sc = _dot_nt(q4.astype(BF16), kc)
    sn = _dot_nt(q4, kn)
    yield
    sc = sc + biasc_ref[rows, :]
    sn = sn + biasn_ref[rows, :]
    ecs, ens, dens = [], [], []
    for g in range(GROUP):
        sink = sink_ref[layer, kh * GROUP + g]
        scg = sc[g * t:(g + 1) * t]
        sng = sn[g * t:(g + 1) * t]
        m = jnp.maximum(jnp.maximum(jnp.max(scg, axis=-1, keepdims=True),
                                    jnp.max(sng, axis=-1, keepdims=True)), sink)
        ec = jnp.exp(scg - m)
        en = jnp.exp(sng - m)
        dens.append(jnp.sum(ec, axis=-1, keepdims=True) + jnp.sum(en, axis=-1, keepdims=True)
                    + jnp.exp(sink - m))
        ecs.append(ec)
        ens.append(en)
    vc = cv_ref[q, :, hs_]
    vn = z_ref[tok, OFF_VA + kh * HEAD_DIM:OFF_VA + (kh + 1) * HEAD_DIM]
    o4 = (_dot(jnp.concatenate(ecs, axis=0).astype(BF16), vc.astype(BF16))
          + _dot(jnp.concatenate(ens, axis=0), vn))
    yield
    outs = []
    for g in range(GROUP):
        h = kh * GROUP + g
        og = o4[g * t:(g + 1) * t] / dens[g]
        ga = z_ref[tok, OFF_GA + h * HEAD_DIM:OFF_GA + (h + 1) * HEAD_DIM]
        outs.append(og * _silu(ga))
    y_ref[tok, kh * GROUP * HEAD_DIM:(kh + 1) * GROUP * HEAD_DIM] = jnp.concatenate(outs, axis=-1).astype(BF16)


def _dec_hgrn2_unit(q, hh, z_ref, s0_ref, lb_ref, oml_ref, recw_ref, y_ref, ns_ref):
    t = DEC_SEQ
    tok = slice(q * t, (q + 1) * t)
    row = lax.broadcasted_iota(jnp.int32, (t, t), 0)
    col = lax.broadcasted_iota(jnp.int32, (t, t), 1)
    causal = col <= row
    cs = hh * REC_HEAD_DIM
    lb = lb_ref[:, cs:cs + REC_HEAD_DIM]
    oml = oml_ref[:, cs:cs + REC_HEAD_DIM]
    logf, kr = _gate_features(z_ref[tok, OFF_FR + cs:OFF_FR + cs + REC_HEAD_DIM], lb, oml)
    qv = _silu(z_ref[tok, OFF_QR + cs:OFF_QR + cs + REC_HEAD_DIM])
    v = z_ref[tok, OFF_IR + cs:OFF_IR + cs + REC_HEAD_DIM]
    a = _cumsum_rows(logf)
    a_end = a[t - 1:t, :]
    scores = _decay_scores_direct(qv, kr, a)
    qt = qv * jnp.exp(a)
    kt = kr * jnp.exp(a_end - a)
    s0 = s0_ref[q, hh]
    o_state = _dot(qt, s0)
    yield
    att = jnp.where(causal, scores, 0.0)
    o = _dot(att, v) + o_state
    ktd = jnp.concatenate([kt, jnp.exp(a_end), jnp.zeros((t - 1, REC_HEAD_DIM), F32)], axis=0).T
    ns_ref[q, hh] = s0 * ktd[:, t:t + 1] + _dot(ktd[:, 0:t], v)
    yield
    on = o * lax.rsqrt(jnp.mean(o * o, axis=-1, keepdims=True) + EPS)
    gr = z_ref[tok, OFF_GR + cs:OFF_GR + cs + REC_HEAD_DIM]
    yr = (on * recw_ref[:, cs:cs + REC_HEAD_DIM]) * _silu(gr)
    y_ref[tok, D_ATTN + cs:D_ATTN + cs + REC_HEAD_DIM] = yr.astype(BF16)


def _dec_mix_kernel(layer, sink_ref, z_ref, ck_ref, cv_ref, s0_ref, lb_ref, oml_ref, recw_ref, biasc_ref,
                    biasn_ref, y_ref, nk_ref, nv_ref, ns_ref):
    t = DEC_SEQ
    for q in range(DEC_SEQS_PER_STEP):
        tok = slice(q * t, (q + 1) * t)
        nk_ref[q, 0:WINDOW - t, :] = ck_ref[q, t:WINDOW, :]
        nk_ref[q, WINDOW - t:WINDOW, :] = z_ref[tok, OFF_KA:OFF_KA + KV_DIM]
        nv_ref[q, 0:WINDOW - t, :] = cv_ref[q, t:WINDOW, :]
        nv_ref[q, WINDOW - t:WINDOW, :] = z_ref[tok, OFF_VA:OFF_VA + KV_DIM]
    units = []
    for kh in range(N_KV):
        units += [_dec_attention_unit(layer, q, kh, sink_ref, z_ref, ck_ref, cv_ref, biasc_ref, biasn_ref, y_ref)
                  for q in range(DEC_SEQS_PER_STEP)]
    for hh in range(N_REC_HEADS):
        units += [_dec_hgrn2_unit(q, hh, z_ref, s0_ref, lb_ref, oml_ref, recw_ref, y_ref, ns_ref)
                  for q in range(DEC_SEQS_PER_STEP)]
    _emit_staggered(units, lag=DEC_LAG)


def _dec_mix(layer, z, ck, cv, s0, lb, oml, rec_w, sinks, bias_c, bias_n):
    t = DEC_SEQ
    nq = DEC_SEQS_PER_STEP
    const2 = lambda s: (0, 0)
    per_layer = lambda s: (layer, 0, 0)
    kv_in = pl.BlockSpec((None, nq, WINDOW, KV_DIM), lambda s: (layer, s, 0, 0))
    st_in = pl.BlockSpec((None, nq, N_REC_HEADS, REC_HEAD_DIM, REC_HEAD_DIM), lambda s: (layer, s, 0, 0, 0))
    return pl.pallas_call(
        functools.partial(_dec_mix_kernel, layer),
        grid=(DEC_BATCH // nq,),
        in_specs=[
            pl.BlockSpec(memory_space=pltpu.SMEM),
            pl.BlockSpec((nq * t, D_IN), lambda s: (s, 0)),
            kv_in,
            kv_in,
            st_in,
            pl.BlockSpec((None, 1, D_REC), per_layer),
            pl.BlockSpec((None, 1, D_REC), per_layer),
            pl.BlockSpec((None, 1, D_REC), per_layer),
            pl.BlockSpec((N_HEADS * t, WINDOW), const2),
            pl.BlockSpec((N_HEADS * t, t), const2),
        ],
        out_specs=[
            pl.BlockSpec((nq * t, D_MODEL), lambda s: (s, 0)),
            pl.BlockSpec((nq, WINDOW, KV_DIM), lambda s: (s, 0, 0)),
            pl.BlockSpec((nq, WINDOW, KV_DIM), lambda s: (s, 0, 0)),
            pl.BlockSpec((nq, N_REC_HEADS, REC_HEAD_DIM, REC_HEAD_DIM), lambda s: (s, 0, 0, 0)),
        ],
        out_shape=[
            jax.ShapeDtypeStruct((N_DEC, D_MODEL), BF16),
            jax.ShapeDtypeStruct((DEC_BATCH, WINDOW, KV_DIM), F32),
            jax.ShapeDtypeStruct((DEC_BATCH, WINDOW, KV_DIM), F32),
            jax.ShapeDtypeStruct((DEC_BATCH, N_REC_HEADS, REC_HEAD_DIM, REC_HEAD_DIM), F32),
        ],
        compiler_params=pltpu.CompilerParams(dimension_semantics=("arbitrary",)),
        name="dec_mix",
    )(sinks, z, ck, cv, s0, lb, oml, rec_w, bias_c, bias_n)


def _final_norm_kernel(x_ref, w_ref, o_ref):
    o_ref[...] = _rmsnorm_rows(x_ref[...], w_ref[...])


def _final_norm_dec(hs, w):
    return pl.pallas_call(
        _final_norm_kernel,
        out_shape=jax.ShapeDtypeStruct((N_DEC, D_MODEL), F32),
        name="final_norm_dec",
    )(hs, w)


def _t5_bucket(dist):
    max_exact = N_BUCKETS // 2
    d = jnp.maximum(dist, 0)
    df = jnp.maximum(d, 1).astype(F32)
    large = max_exact + (jnp.log(df / max_exact) / math.log(MAX_DISTANCE / max_exact)
                         * (N_BUCKETS - max_exact)).astype(jnp.int32)
    large = jnp.minimum(large, N_BUCKETS - 1)
    return jnp.where(d < max_exact, d, large)


def _bias_tables(rel_bias_table):
    bias_d = rel_bias_table[_t5_bucket(jnp.arange(WINDOW))].T.astype(F32)
    period = 3 * WINDOW + 1
    u = jnp.concatenate([jnp.full((N_HEADS, 1), NEG, F32), bias_d[:, ::-1],
                         jnp.full((N_HEADS, period - 1 - WINDOW), NEG, F32)], axis=1)
    rows = jnp.tile(u, (1, WINDOW))[:, :WINDOW * (period - 1)].reshape(N_HEADS, WINDOW, period - 1)
    prompt = rows[:, :, :2 * WINDOW].reshape(N_HEADS * WINDOW, 2 * WINDOW).astype(BF16)
    dec = rows[:, :DEC_SEQ, :WINDOW + DEC_SEQ].reshape(N_HEADS * DEC_SEQ, WINDOW + DEC_SEQ)
    return prompt, dec[:, :WINDOW], dec[:, WINDOW:]


def kernel(x_prompt, x_sample, cache_k, cache_v, state_h, meta_tokens, w_in, w_out, norm_w, final_norm_w,
           attn_sinks, rel_bias_table, hgrn_lb_logits, hgrn_norm_w):
    pl_ = jax.nn.softmax(hgrn_lb_logits.astype(F32), axis=0)
    lb = (jnp.cumsum(pl_, axis=0) - pl_[0:1]).reshape(DEPTH, 1, D_REC)
    oml = 1.0 - lb
    bias_p, bias_dc, bias_dn = _bias_tables(rel_bias_table)
    nw = norm_w.astype(F32).reshape(DEPTH, 1, D_MODEL)
    rw = hgrn_norm_w.astype(F32).reshape(DEPTH, 1, D_REC)
    fw = final_norm_w.astype(F32).reshape(1, D_MODEL)
    sinks = attn_sinks.astype(F32)

    h_meta = meta_tokens.astype(F32)
    h_main = x_prompt
    hs = x_sample.reshape(N_DEC, D_MODEL)
    ck = cache_k.reshape(DEPTH, DEC_BATCH, WINDOW, KV_DIM)
    cv = cache_v.reshape(DEPTH, DEC_BATCH, WINDOW, KV_DIM)

    pk, pv, ps, sk, sv, ss = [], [], [], [], [], []
    for l in range(DEPTH):
        z, w_in_b = _dec_proj(l, hs, nw, w_in)
        y, nk, nv, ns = _dec_mix(l, z, ck, cv, state_h, lb, oml, rw, sinks, bias_dc, bias_dn)
        hs, w_out_b = _dec_out(l, hs, y, w_out)
        sk.append(nk)
        sv.append(nv)
        ss.append(ns)

        outs = _prompt_layer(l, h_meta, h_main, w_in_b, w_out_b, nw, lb, oml, rw, sinks, bias_p, fw)
        if l < DEPTH - 1:
            h_meta, h_main, k_l, v_l, s_l = outs
        else:
            h_main, k_l, v_l, s_l = outs
        pk.append(k_l)
        pv.append(v_l)
        ps.append(s_l)

    y_prompt = h_main
    y_sample = _final_norm_dec(hs, fw).reshape(DEC_BATCH, DEC_SEQ, D_MODEL)
    kv_shape = (DEPTH, -1, WINDOW, N_KV, HEAD_DIM)
    return (y_prompt, y_sample,
            jnp.stack(pk).reshape(kv_shape), jnp.stack(pv).reshape(kv_shape),
            jnp.swapaxes(jnp.stack(ps), -1, -2),
            jnp.stack(sk).reshape(kv_shape), jnp.stack(sv).reshape(kv_shape), jnp.stack(ss))
```

```python
import functools
import math

import jax
import jax.numpy as jnp
from jax import lax
from jax.experimental import pallas as pl
from jax.experimental.pallas import tpu as pltpu

D_MODEL = 2048
BATCH = 4
SEQ = 2048
DEPTH = 4
DEC_BATCH = 32
DEC_SEQ = 8
N_META = 16
D_ATTN = 1024
D_REC = 1024
HEAD_DIM = 64
N_HEADS = 16
N_KV = 4
GROUP = 4
KV_DIM = 256
WINDOW = 128
N_BUCKETS = 32
MAX_DISTANCE = 128
REC_HEAD_DIM = 128
N_REC_HEADS = 8
EPS = 1e-6
D_IN = 2 * D_ATTN + 2 * KV_DIM + 4 * D_REC

OFF_KA = D_ATTN
OFF_VA = OFF_KA + KV_DIM
OFF_GA = OFF_VA + KV_DIM
OFF_QR = OFF_GA + D_ATTN
OFF_FR = OFF_QR + D_REC
OFF_IR = OFF_FR + D_REC
OFF_GR = OFF_IR + D_REC
ATT_COLS = OFF_QR

TILE = 256
QBLK = WINDOW
CHUNK = 128
HALF = CHUNK // 2
ROW_TILES = SEQ // TILE
PAD = TILE - N_META
PROJ_COLS = 512
REC_GROUP = PROJ_COLS // REC_HEAD_DIM
NEG = -1e30
MAX_FACTOR_EXPONENT = 80.0
N_DEC = DEC_BATCH * DEC_SEQ
V7X_LANES = 128
V7X_VMEM_BYTES = 64 * 1024 * 1024
MIB = 1024 * 1024
VMEM_LIMIT = V7X_VMEM_BYTES - MIB
DEC_VMEM_LIMIT = V7X_VMEM_BYTES - 8 * MIB

F32 = jnp.float32
BF16 = jnp.bfloat16


def _sigmoid_pair(x):
    e = jnp.exp(-jnp.abs(x))
    inv = 1.0 / (1.0 + e)
    pos = x >= 0
    return jnp.where(pos, inv, e * inv), jnp.where(pos, e * inv, inv)


def _silu(x):
    return x * (0.5 * jnp.tanh(0.5 * x) + 0.5)


def _cumsum_rows(x):
    n = x.shape[0]
    row = lax.broadcasted_iota(jnp.int32, x.shape, 0)
    k = 1
    while k < n:
        x = x + jnp.where(row >= k, pltpu.roll(x, k, axis=0), 0.0)
        k *= 2
    return x


def _rmsnorm_rows(x, w):
    ms = jnp.mean(x * x, axis=-1, keepdims=True)
    return (x * lax.rsqrt(ms + EPS)) * w


def _dot_nt(a, b):
    return lax.dot_general(a, b, (((1,), (1,)), ((), ())), preferred_element_type=F32)


def _dot_tn(a, b):
    return lax.dot_general(a, b, (((0,), (0,)), ((), ())), preferred_element_type=F32)


def _dot(a, b):
    return jnp.dot(a, b, preferred_element_type=F32)


def _gate_features(fr, lb, oml):
    sig, sneg = _sigmoid_pair(fr)
    logf = jnp.log(lb + oml * sig)
    return logf, oml * sneg


def _attention_unit(layer, qb, kh, za_ref, kv_ref, y_ref, bias_ref, sink_ref, c_tile):
    r0 = qb * QBLK
    kpos = c_tile + r0 - QBLK + lax.broadcasted_iota(jnp.int32, (1, 2 * QBLK), 1)
    kvalid = kpos >= 0
    kk = kv_ref[r0:r0 + 2 * QBLK, kh * HEAD_DIM:(kh + 1) * HEAD_DIM]
    q4 = jnp.concatenate(
        [za_ref[r0:r0 + QBLK, (kh * GROUP + g) * HEAD_DIM:(kh * GROUP + g + 1) * HEAD_DIM]
         for g in range(GROUP)], axis=0)
    s = _dot_nt((q4 * (HEAD_DIM ** -0.5)).astype(BF16), kk)
    yield
    s = s + bias_ref[kh * GROUP * QBLK:(kh + 1) * GROUP * QBLK, :].astype(F32)
    s = jnp.where(kvalid, s, NEG)
    es, dens = [], []
    for g in range(GROUP):
        sink = sink_ref[layer, kh * GROUP + g]
        sg = s[g * QBLK:(g + 1) * QBLK]
        m = jnp.maximum(jnp.max(sg, axis=-1, keepdims=True), sink)
        e = jnp.exp(sg - m)
        dens.append(jnp.sum(e, axis=-1, keepdims=True) + jnp.exp(sink - m))
        es.append(e.astype(BF16))
    vv = kv_ref[r0:r0 + 2 * QBLK, KV_DIM + kh * HEAD_DIM:KV_DIM + (kh + 1) * HEAD_DIM]
    o4 = _dot(jnp.concatenate(es, axis=0), vv)
    yield
    outs = []
    for g in range(GROUP):
        h = kh * GROUP + g
        og = o4[g * QBLK:(g + 1) * QBLK] / dens[g]
        ga = za_ref[r0:r0 + QBLK, OFF_GA + h * HEAD_DIM:OFF_GA + (h + 1) * HEAD_DIM]
        outs.append(og * _silu(ga))
    y_ref[r0:r0 + QBLK, kh * GROUP * HEAD_DIM:(kh + 1) * GROUP * HEAD_DIM] = (
        jnp.concatenate(outs, axis=-1).astype(BF16))


def _decay_scores_direct(q, kr, a):
    n = q.shape[0]
    row = lax.broadcasted_iota(jnp.int32, (n, n), 0)
    col = lax.broadcasted_iota(jnp.int32, (n, n), 1)

    def diagonal(d, acc):
        ks = pltpu.roll(kr, d, axis=0)
        a_s = pltpu.roll(a, d, axis=0)
        p = q * ks * jnp.exp(jnp.minimum(a - a_s, 0.0))
        return jnp.where(col == row - d, jnp.sum(p, axis=-1, keepdims=True), acc)

    acc = jnp.zeros((n, n), F32)
    if n <= DEC_SEQ:
        for d in range(n):
            acc = diagonal(d, acc)
        return acc
    return lax.fori_loop(0, n, diagonal, acc)


def _hgrn2_unit(ci, hh, zr_ref, st_ref, y_ref, lb_ref, oml_ref, recw_ref, c_tile, spans=None, heads_in_buf=None):
    heads_in_buf = REC_GROUP if heads_in_buf is None else heads_in_buf
    part = heads_in_buf * REC_HEAD_DIM
    row = lax.broadcasted_iota(jnp.int32, (CHUNK, CHUNK), 0)
    col = lax.broadcasted_iota(jnp.int32, (CHUNK, CHUNK), 1)
    causal = col <= row
    r0 = ci * CHUNK if isinstance(ci, int) else pl.multiple_of(ci * CHUNK, CHUNK)
    rows = pl.ds(r0, CHUNK)
    rpos = c_tile + r0 + lax.broadcasted_iota(jnp.int32, (CHUNK, 1), 0)
    rvalid = rpos >= 0
    if isinstance(hh, int):
        cs = hh * REC_HEAD_DIM
        zc = (hh % heads_in_buf) * REC_HEAD_DIM
    else:
        cs = pl.multiple_of(hh * REC_HEAD_DIM, REC_HEAD_DIM)
        zc = 0
    head_cols = pl.ds(cs, REC_HEAD_DIM)
    lb = lb_ref[:, head_cols]
    oml = oml_ref[:, head_cols]
    logf, kr = _gate_features(zr_ref[rows, part + zc:part + zc + REC_HEAD_DIM], lb, oml)
    q = _silu(zr_ref[rows, zc:zc + REC_HEAD_DIM])
    v = jnp.where(rvalid, zr_ref[rows, 2 * part + zc:2 * part + zc + REC_HEAD_DIM], 0.0)
    vb = v.astype(BF16)
    a = _cumsum_rows(logf)
    a_mid = a[HALF - 1:HALF, :]
    a_end = a[CHUNK - 1:CHUNK, :]
    if spans is None:
        scores = _decay_scores_direct(q, kr, a)
    else:
        qh = (q * jnp.exp(a - a_mid)).astype(BF16)
        kh_ = (kr * jnp.exp(a_mid - a)).astype(BF16)
        scores = _dot_nt(qh, kh_)
        spans.append(jnp.maximum(a[0:1, :] - a_mid, a_mid - a_end))
    qt = (q * jnp.exp(a)).astype(BF16)
    kt = (kr * jnp.exp(a_end - a)).astype(BF16)
    st = st_ref[hh]
    o_state = _dot_nt(qt, st.astype(BF16))
    yield
    att = jnp.where(causal, scores, 0.0)
    o = _dot(att.astype(BF16), vb) + o_state
    st_ref[hh] = st * jnp.exp(a_end) + _dot_tn(vb, kt)
    yield
    on = o * lax.rsqrt(jnp.mean(o * o, axis=-1, keepdims=True) + EPS)
    gr = zr_ref[rows, 3 * part + zc:3 * part + zc + REC_HEAD_DIM]
    yr = (on * recw_ref[:, head_cols]) * _silu(gr)
    y_ref[rows, pl.ds(D_ATTN + cs, REC_HEAD_DIM)] = yr.astype(BF16)


def _emit_staggered(units, fillers=(), lag=1):
    n = len(units)
    rounds = n + 2 * lag
    at_round = [[] for _ in range(rounds)]
    for k, f in enumerate(fillers):
        at_round[k * rounds // len(fillers)].append(f)
    for i in range(rounds):
        if i < n:
            next(units[i])
        for f in at_round[i]:
            f()
        if 0 <= i - lag < n:
            next(units[i - lag])
        if 0 <= i - 2 * lag < n:
            next(units[i - 2 * lag], None)


def _main_tile(s):
    t = jnp.maximum(s - 1, 0)
    return lax.div(t, ROW_TILES), lax.rem(t, ROW_TILES)


def _prompt_layer_kernel(layer, final, sink_ref, meta_ref, hin_ref, win_ref, wout_ref, normw_ref, lb_ref, oml_ref,
                         recw_ref, bias_ref, finw_ref, *refs):
    if final:
        hout_ref, pk_ref, pv_ref, ps_ref = refs[:4]
        refs = refs[4:]
    else:
        meta_out_ref, hout_ref, pk_ref, pv_ref, ps_ref = refs[:5]
        refs = refs[5:]
    xn_ref, za_ref, zr_ref, kv_ref, st_ref, y_ref, kv_meta_ref, st_meta_ref, st_start_ref, zhead_ref = refs
    s = pl.program_id(0)
    is_meta = s == 0
    _, j = _main_tile(s)

    @pl.when(is_meta)
    def _():
        kv_ref[0:QBLK, :] = jnp.zeros((QBLK, 2 * KV_DIM), BF16)
        st_ref[...] = jnp.zeros(st_ref.shape, F32)

    @pl.when(jnp.logical_and(jnp.logical_not(is_meta), j == 0))
    def _():
        kv_ref[0:QBLK, :] = kv_meta_ref[...]
        st_ref[...] = st_meta_ref[...]

    st_start_ref[...] = st_ref[...]

    def tile_input(cols=slice(None)):
        meta = meta_ref[:, cols]
        meta_rows = jnp.concatenate([jnp.zeros((PAD, meta.shape[1]), F32), meta], axis=0)
        return jnp.where(is_meta, meta_rows, hin_ref[0, :, cols])

    c_tile = jnp.where(is_meta, -PAD, j * TILE + N_META)
    xn_ref[...] = _rmsnorm_rows(tile_input(), normw_ref[...]).astype(BF16)

    def project(dst_ref, dst_col, src_col):
        dst_ref[:, pl.ds(dst_col, PROJ_COLS)] = _dot(xn_ref[...], win_ref[:, pl.ds(src_col, PROJ_COLS)])

    def project_rec(dst_ref, group):
        return [functools.partial(project, dst_ref, part * PROJ_COLS, off + group * PROJ_COLS)
                for part, off in enumerate((OFF_QR, OFF_FR, OFF_IR, OFF_GR))]

    def out_proj(k0, k1, c0, first_half):
        cols = pl.ds(c0, PROJ_COLS)
        base = tile_input(cols) if first_half else hout_ref[0, :, cols]
        hout_ref[0, :, cols] = base + _dot(y_ref[:, k0:k1], wout_ref[k0:k1, cols])

    def column_blocks(n, body):
        def step(i, carry):
            body(pl.multiple_of(i * PROJ_COLS, PROJ_COLS))
            return carry
        lax.fori_loop(0, n, step, 0)

    for c0 in range(0, ATT_COLS, PROJ_COLS):
        project(za_ref, c0, c0)
    kv_ref[QBLK:QBLK + TILE, :] = za_ref[:, OFF_KA:OFF_KA + 2 * KV_DIM].astype(BF16)
    pk_ref[0] = za_ref[TILE - WINDOW:TILE, OFF_KA:OFF_KA + KV_DIM]
    pv_ref[0] = za_ref[TILE - WINDOW:TILE, OFF_VA:OFF_VA + KV_DIM]

    att_units = [_attention_unit(layer, qb, kh, za_ref, kv_ref, y_ref, bias_ref, sink_ref, c_tile)
                 for qb in range(TILE // QBLK) for kh in range(N_KV)]
    _emit_staggered(att_units, project_rec(zr_ref, 0))
    kv_ref[0:QBLK, :] = kv_ref[TILE:TILE + QBLK, :]

    n_groups = N_REC_HEADS // REC_GROUP
    z_bufs = [zr_ref, za_ref]
    spans = []

    def rec_units(grp, state_ref, spans_):
        return [_hgrn2_unit(ci, grp * REC_GROUP + hl, z_bufs[grp % 2], state_ref, y_ref, lb_ref, oml_ref,
                            recw_ref, c_tile, spans_)
                for ci in range(TILE // CHUNK) for hl in range(REC_GROUP)]

    for grp in range(n_groups):
        if grp + 1 < n_groups:
            fillers = project_rec(z_bufs[(grp + 1) % 2], grp + 1)
        else:
            fillers = [functools.partial(out_proj, 0, D_ATTN, c0, True) for c0 in range(0, D_MODEL, PROJ_COLS)]
        _emit_staggered(rec_units(grp, st_ref, spans), fillers)
    widest = jnp.max(jnp.concatenate(spans, axis=0))
    for c0 in range(0, D_MODEL, PROJ_COLS):
        out_proj(D_ATTN, D_MODEL, c0, False)

    @pl.when(widest > MAX_FACTOR_EXPONENT)
    def _():
        def replay_chunk(hh, ci, carry):
            for _ in _hgrn2_unit(ci, hh, zhead_ref, st_start_ref, y_ref, lb_ref, oml_ref, recw_ref, c_tile,
                                 spans=None, heads_in_buf=1):
                pass
            return carry

        def replay_head(hh, carry):
            for part, off in enumerate((OFF_QR, OFF_FR, OFF_IR, OFF_GR)):
                src = pl.multiple_of(off + hh * REC_HEAD_DIM, REC_HEAD_DIM)
                zhead_ref[:, part * REC_HEAD_DIM:(part + 1) * REC_HEAD_DIM] = _dot(
                    xn_ref[...], win_ref[:, pl.ds(src, REC_HEAD_DIM)])
            return lax.fori_loop(0, TILE // CHUNK, functools.partial(replay_chunk, hh), carry)

        lax.fori_loop(0, N_REC_HEADS, replay_head, 0)
        column_blocks(D_MODEL // PROJ_COLS, lambda c0: out_proj(0, D_MODEL, c0, True))

    ps_ref[0] = st_ref[...]

    @pl.when(is_meta)
    def _():
        kv_meta_ref[...] = kv_ref[0:QBLK, :]
        st_meta_ref[...] = st_ref[...]
        if not final:
            meta_out_ref[...] = hout_ref[0, PAD:TILE, :]

    if final:
        @pl.when(jnp.logical_not(is_meta))
        def _():
            hout_ref[0] = _rmsnorm_rows(hout_ref[0], finw_ref[...])


def _layer_block(shape, layer):
    nd = len(shape)
    return pl.BlockSpec((None,) + tuple(shape), lambda s: (layer,) + (0,) * nd, pipeline_mode=pl.Buffered(1))


def _const_block(shape):
    nd = len(shape)
    return pl.BlockSpec(tuple(shape), lambda s: (0,) * nd, pipeline_mode=pl.Buffered(1))


def _prompt_layer(layer, h_meta, h_main, w_in, w_out, norm_w, lb, oml, rec_w, sinks, bias, fin_w):
    final = layer == DEPTH - 1
    main = pl.BlockSpec((1, TILE, D_MODEL), lambda s: _main_tile(s) + (0,))
    meta = pl.BlockSpec((N_META, D_MODEL), lambda s: (0, 0))
    kv_spec = pl.BlockSpec((1, WINDOW, KV_DIM), lambda s: (_main_tile(s)[0], 0, 0))
    st_spec = pl.BlockSpec((1, N_REC_HEADS, REC_HEAD_DIM, REC_HEAD_DIM), lambda s: (_main_tile(s)[0], 0, 0, 0))
    main_shape = jax.ShapeDtypeStruct((BATCH, SEQ, D_MODEL), F32)
    meta_shape = jax.ShapeDtypeStruct((N_META, D_MODEL), F32)
    return pl.pallas_call(
        functools.partial(_prompt_layer_kernel, layer, final),
        grid=(1 + BATCH * ROW_TILES,),
        in_specs=[
            pl.BlockSpec(memory_space=pltpu.SMEM),
            _const_block((N_META, D_MODEL)),
            main,
            _const_block((D_MODEL, D_IN)),
            _const_block((D_MODEL, D_MODEL)),
            _layer_block((1, D_MODEL), layer),
            _layer_block((1, D_REC), layer),
            _layer_block((1, D_REC), layer),
            _layer_block((1, D_REC), layer),
            _const_block((N_HEADS * QBLK, 2 * QBLK)),
            _const_block((1, D_MODEL)),
        ],
        out_specs=([] if final else [meta]) + [main, kv_spec, kv_spec, st_spec],
        out_shape=([] if final else [meta_shape]) + [
            main_shape,
            jax.ShapeDtypeStruct((BATCH, WINDOW, KV_DIM), F32),
            jax.ShapeDtypeStruct((BATCH, WINDOW, KV_DIM), F32),
            jax.ShapeDtypeStruct((BATCH, N_REC_HEADS, REC_HEAD_DIM, REC_HEAD_DIM), F32),
        ],
        scratch_shapes=[
            pltpu.VMEM((TILE, D_MODEL), BF16),
            pltpu.VMEM((TILE, ATT_COLS), F32),
            pltpu.VMEM((TILE, 4 * PROJ_COLS), F32),
            pltpu.VMEM((QBLK + TILE, 2 * KV_DIM), BF16),
            pltpu.VMEM((N_REC_HEADS, REC_HEAD_DIM, REC_HEAD_DIM), F32),
            pltpu.VMEM((TILE, D_MODEL), BF16),
            pltpu.VMEM((QBLK, 2 * KV_DIM), BF16),
            pltpu.VMEM((N_REC_HEADS, REC_HEAD_DIM, REC_HEAD_DIM), F32),
            pltpu.VMEM((N_REC_HEADS, REC_HEAD_DIM, REC_HEAD_DIM), F32),
            pltpu.VMEM((TILE, 4 * REC_HEAD_DIM), F32),
        ],
        compiler_params=pltpu.CompilerParams(
            dimension_semantics=("arbitrary",), vmem_limit_bytes=VMEM_LIMIT),
        name="prompt_layer",
    )(sinks, h_meta, h_main, w_in, w_out, norm_w, lb, oml, rec_w, bias, fin_w)


DEC_PROJ_COLS = D_IN // 4
DEC_OUT_COLS = D_MODEL // 2


def _dec_proj_kernel(hs_ref, normw_ref, win_ref, z_ref, wb_ref):
    xn = _rmsnorm_rows(hs_ref[...], normw_ref[...]).astype(BF16)
    for c0 in range(0, DEC_PROJ_COLS, V7X_LANES):
        cols = slice(c0, c0 + V7X_LANES)
        wb = win_ref[:, cols].astype(BF16)
        wb_ref[:, cols] = wb
        z_ref[:, cols] = _dot(xn, wb)


def _dec_proj(layer, hs, norm_w, w_in):
    return pl.pallas_call(
        _dec_proj_kernel,
        grid=(D_IN // DEC_PROJ_COLS,),
        in_specs=[
            pl.BlockSpec((N_DEC, D_MODEL), lambda n: (0, 0)),
            pl.BlockSpec((None, 1, D_MODEL), lambda n: (layer, 0, 0)),
            pl.BlockSpec((None, D_MODEL, DEC_PROJ_COLS), lambda n: (layer, 0, n)),
        ],
        out_specs=[pl.BlockSpec((N_DEC, DEC_PROJ_COLS), lambda n: (0, n)),
                   pl.BlockSpec((D_MODEL, DEC_PROJ_COLS), lambda n: (0, n))],
        out_shape=[jax.ShapeDtypeStruct((N_DEC, D_IN), F32),
                   jax.ShapeDtypeStruct((D_MODEL, D_IN), BF16)],
        compiler_params=pltpu.CompilerParams(dimension_semantics=("arbitrary",), vmem_limit_bytes=DEC_VMEM_LIMIT),
        name="dec_proj",
    )(hs, norm_w, w_in)


def _dec_out_kernel(hs_ref, y_ref, wout_ref, o_ref, wb_ref):
    for c0 in range(0, DEC_OUT_COLS, 2 * V7X_LANES):
        cols = slice(c0, c0 + 2 * V7X_LANES)
        wb = wout_ref[:, cols].astype(BF16)
        wb_ref[:, cols] = wb
        o_ref[:, cols] = hs_ref[:, cols] + _dot(y_ref[...], wb)


def _dec_out(layer, hs, y, w_out):
    return pl.pallas_call(
        _dec_out_kernel,
        grid=(D_MODEL // DEC_OUT_COLS,),
        in_specs=[
            pl.BlockSpec((N_DEC, DEC_OUT_COLS), lambda n: (0, n)),
            pl.BlockSpec((N_DEC, D_MODEL), lambda n: (0, 0)),
            pl.BlockSpec((None, D_MODEL, DEC_OUT_COLS), lambda n: (layer, 0, n)),
        ],
        out_specs=[pl.BlockSpec((N_DEC, DEC_OUT_COLS), lambda n: (0, n)),
                   pl.BlockSpec((D_MODEL, DEC_OUT_COLS), lambda n: (0, n))],
        out_shape=[jax.ShapeDtypeStruct((N_DEC, D_MODEL), F32),
                   jax.ShapeDtypeStruct((D_MODEL, D_MODEL), BF16)],
        compiler_params=pltpu.CompilerParams(dimension_semantics=("arbitrary",), vmem_limit_bytes=DEC_VMEM_LIMIT),
        name="dec_out",
    )(hs, y, w_out)


DEC_SEQS_PER_STEP = 4
DEC_LAG = DEC_SEQS_PER_STEP * (N_KV + N_REC_HEADS)


def _dec_attention_unit(layer, q, kh, sink_ref, z_ref, ck_ref, cv_ref, biasc_ref, biasn_ref, y_ref):
    t = DEC_SEQ
    tok = slice(q * t, (q + 1) * t)
    hs_ = slice(kh * HEAD_DIM, (kh + 1) * HEAD_DIM)
    kc = ck_ref[q, :, hs_].astype(BF16)
    kn = z_ref[tok, OFF_KA + kh * HEAD_DIM:OFF_KA + (kh + 1) * HEAD_DIM]
    q4 = jnp.concatenate(
        [z_ref[tok, (kh * GROUP + g) * HEAD_DIM:(kh * GROUP + g + 1) * HEAD_DIM] for g in range(GROUP)],
        axis=0) * (HEAD_DIM ** -0.5)
    rows = slice(kh * GROUP * t, (kh + 1) * GROUP * t)
    sc = _dot_nt(q4.astype(BF16), kc)
    sn = _dot_nt(q4, kn)
    yield
    sc = sc + biasc_ref[rows, :]
    sn = sn + biasn_ref[rows, :]
    ecs, ens, dens = [], [], []
    for g in range(GROUP):
        sink = sink_ref[layer, kh * GROUP + g]
        scg = sc[g * t:(g + 1) * t]
        sng = sn[g * t:(g + 1) * t]
        m = jnp.maximum(jnp.maximum(jnp.max(scg, axis=-1, keepdims=True),
                                    jnp.max(sng, axis=-1, keepdims=True)), sink)
        ec = jnp.exp(scg - m)
        en = jnp.exp(sng - m)
        dens.append(jnp.sum(ec, axis=-1, keepdims=True) + jnp.sum(en, axis=-1, keepdims=True)
                    + jnp.exp(sink - m))
        ecs.append(ec)
        ens.append(en)
    vc = cv_ref[q, :, hs_]
    vn = z_ref[tok, OFF_VA + kh * HEAD_DIM:OFF_VA + (kh + 1) * HEAD_DIM]
    o4 = (_dot(jnp.concatenate(ecs, axis=0).astype(BF16), vc.astype(BF16))
          + _dot(jnp.concatenate(ens, axis=0), vn))
    yield
    outs = []
    for g in range(GROUP):
        h = kh * GROUP + g
        og = o4[g * t:(g + 1) * t] / dens[g]
        ga = z_ref[tok, OFF_GA + h * HEAD_DIM:OFF_GA + (h + 1) * HEAD_DIM]
        outs.append(og * _silu(ga))
    y_ref[tok, kh * GROUP * HEAD_DIM:(kh + 1) * GROUP * HEAD_DIM] = jnp.concatenate(outs, axis=-1).astype(BF16)


def _dec_hgrn2_unit(q, hh, z_ref, s0_ref, lb_ref, oml_ref, recw_ref, y_ref, ns_ref):
    t = DEC_SEQ
    tok = slice(q * t, (q + 1) * t)
    row = lax.broadcasted_iota(jnp.int32, (t, t), 0)
    col = lax.broadcasted_iota(jnp.int32, (t, t), 1)
    causal = col <= row
    cs = hh * REC_HEAD_DIM
    lb = lb_ref[:, cs:cs + REC_HEAD_DIM]
    oml = oml_ref[:, cs:cs + REC_HEAD_DIM]
    logf, kr = _gate_features(z_ref[tok, OFF_FR + cs:OFF_FR + cs + REC_HEAD_DIM], lb, oml)
    qv = _silu(z_ref[tok, OFF_QR + cs:OFF_QR + cs + REC_HEAD_DIM])
    v = z_ref[tok, OFF_IR + cs:OFF_IR + cs + REC_HEAD_DIM]
    a = _cumsum_rows(logf)
    a_end = a[t - 1:t, :]
    scores = _decay_scores_direct(qv, kr, a)
    qt = qv * jnp.exp(a)
    kt = kr * jnp.exp(a_end - a)
    s0 = s0_ref[q, hh]
    o_state = _dot(qt, s0)
    yield
    att = jnp.where(causal, scores, 0.0)
    o = _dot(att, v) + o_state
    ktd = jnp.concatenate([kt, jnp.exp(a_end), jnp.zeros((t - 1, REC_HEAD_DIM), F32)], axis=0).T
    ns_ref[q, hh] = s0 * ktd[:, t:t + 1] + _dot(ktd[:, 0:t], v)
    yield
    on = o * lax.rsqrt(jnp.mean(o * o, axis=-1, keepdims=True) + EPS)
    gr = z_ref[tok, OFF_GR + cs:OFF_GR + cs + REC_HEAD_DIM]
    yr = (on * recw_ref[:, cs:cs + REC_HEAD_DIM]) * _silu(gr)
    y_ref[tok, D_ATTN + cs:D_ATTN + cs + REC_HEAD_DIM] = yr.astype(BF16)


def _dec_mix_kernel(layer, sink_ref, z_ref, ck_ref, cv_ref, s0_ref, lb_ref, oml_ref, recw_ref, biasc_ref,
                    biasn_ref, y_ref, nk_ref, nv_ref, ns_ref):
    t = DEC_SEQ
    for q in range(DEC_SEQS_PER_STEP):
        tok = slice(q * t, (q + 1) * t)
        nk_ref[q, 0:WINDOW - t, :] = ck_ref[q, t:WINDOW, :]
        nk_ref[q, WINDOW - t:WINDOW, :] = z_ref[tok, OFF_KA:OFF_KA + KV_DIM]
        nv_ref[q, 0:WINDOW - t, :] = cv_ref[q, t:WINDOW, :]
        nv_ref[q, WINDOW - t:WINDOW, :] = z_ref[tok, OFF_VA:OFF_VA + KV_DIM]
    units = []
    for kh in range(N_KV):
        units += [_dec_attention_unit(layer, q, kh, sink_ref, z_ref, ck_ref, cv_ref, biasc_ref, biasn_ref, y_ref)
                  for q in range(DEC_SEQS_PER_STEP)]
    for hh in range(N_REC_HEADS):
        units += [_dec_hgrn2_unit(q, hh, z_ref, s0_ref, lb_ref, oml_ref, recw_ref, y_ref, ns_ref)
                  for q in range(DEC_SEQS_PER_STEP)]
    _emit_staggered(units, lag=DEC_LAG)


def _dec_mix(layer, z, ck, cv, s0, lb, oml, rec_w, sinks, bias_c, bias_n):
    t = DEC_SEQ
    nq = DEC_SEQS_PER_STEP
    const2 = lambda s: (0, 0)
    per_layer = lambda s: (layer, 0, 0)
    kv_in = pl.BlockSpec((None, nq, WINDOW, KV_DIM), lambda s: (layer, s, 0, 0))
    st_in = pl.BlockSpec((None, nq, N_REC_HEADS, REC_HEAD_DIM, REC_HEAD_DIM), lambda s: (layer, s, 0, 0, 0))
    return pl.pallas_call(
        functools.partial(_dec_mix_kernel, layer),
        grid=(DEC_BATCH // nq,),
        in_specs=[
            pl.BlockSpec(memory_space=pltpu.SMEM),
            pl.BlockSpec((nq * t, D_IN), lambda s: (s, 0)),
            kv_in,
            kv_in,
            st_in,
            pl.BlockSpec((None, 1, D_REC), per_layer),
            pl.BlockSpec((None, 1, D_REC), per_layer),
            pl.BlockSpec((None, 1, D_REC), per_layer),
            pl.BlockSpec((N_HEADS * t, WINDOW), const2),
            pl.BlockSpec((N_HEADS * t, t), const2),
        ],
        out_specs=[
            pl.BlockSpec((nq * t, D_MODEL), lambda s: (s, 0)),
            pl.BlockSpec((nq, WINDOW, KV_DIM), lambda s: (s, 0, 0)),
            pl.BlockSpec((nq, WINDOW, KV_DIM), lambda s: (s, 0, 0)),
            pl.BlockSpec((nq, N_REC_HEADS, REC_HEAD_DIM, REC_HEAD_DIM), lambda s: (s, 0, 0, 0)),
        ],
        out_shape=[
            jax.ShapeDtypeStruct((N_DEC, D_MODEL), BF16),
            jax.ShapeDtypeStruct((DEC_BATCH, WINDOW, KV_DIM), F32),
            jax.ShapeDtypeStruct((DEC_BATCH, WINDOW, KV_DIM), F32),
            jax.ShapeDtypeStruct((DEC_BATCH, N_REC_HEADS, REC_HEAD_DIM, REC_HEAD_DIM), F32),
        ],
        compiler_params=pltpu.CompilerParams(dimension_semantics=("arbitrary",)),
        name="dec_mix",
    )(sinks, z, ck, cv, s0, lb, oml, rec_w, bias_c, bias_n)


def _final_norm_kernel(x_ref, w_ref, o_ref):
    o_ref[...] = _rmsnorm_rows(x_ref[...], w_ref[...])


def _final_norm_dec(hs, w):
    return pl.pallas_call(
        _final_norm_kernel,
        out_shape=jax.ShapeDtypeStruct((N_DEC, D_MODEL), F32),
        name="final_norm_dec",
    )(hs, w)


def _t5_bucket(dist):
    max_exact = N_BUCKETS // 2
    d = jnp.maximum(dist, 0)
    df = jnp.maximum(d, 1).astype(F32)
    large = max_exact + (jnp.log(df / max_exact) / math.log(MAX_DISTANCE / max_exact)
                         * (N_BUCKETS - max_exact)).astype(jnp.int32)
    large = jnp.minimum(large, N_BUCKETS - 1)
    return jnp.where(d < max_exact, d, large)


def _bias_tables(rel_bias_table):
    bias_d = rel_bias_table[_t5_bucket(jnp.arange(WINDOW))].T.astype(F32)
    period = 3 * WINDOW + 1
    u = jnp.concatenate([jnp.full((N_HEADS, 1), NEG, F32), bias_d[:, ::-1],
                         jnp.full((N_HEADS, period - 1 - WINDOW), NEG, F32)], axis=1)
    rows = jnp.tile(u, (1, WINDOW))[:, :WINDOW * (period - 1)].reshape(N_HEADS, WINDOW, period - 1)
    prompt = rows[:, :, :2 * WINDOW].reshape(N_HEADS * WINDOW, 2 * WINDOW).astype(BF16)
    dec = rows[:, :DEC_SEQ, :WINDOW + DEC_SEQ].reshape(N_HEADS * DEC_SEQ, WINDOW + DEC_SEQ)
    return prompt, dec[:, :WINDOW], dec[:, WINDOW:]


def kernel(x_prompt, x_sample, cache_k, cache_v, state_h, meta_tokens, w_in, w_out, norm_w, final_norm_w,
           attn_sinks, rel_bias_table, hgrn_lb_logits, hgrn_norm_w):
    pl_ = jax.nn.softmax(hgrn_lb_logits.astype(F32), axis=0)
    lb = (jnp.cumsum(pl_, axis=0) - pl_[0:1]).reshape(DEPTH, 1, D_REC)
    oml = 1.0 - lb
    bias_p, bias_dc, bias_dn = _bias_tables(rel_bias_table)
    nw = norm_w.astype(F32).reshape(DEPTH, 1, D_MODEL)
    rw = hgrn_norm_w.astype(F32).reshape(DEPTH, 1, D_REC)
    fw = final_norm_w.astype(F32).reshape(1, D_MODEL)
    sinks = attn_sinks.astype(F32)

    h_meta = meta_tokens.astype(F32)
    h_main = x_prompt
    hs = x_sample.reshape(N_DEC, D_MODEL)
    ck = cache_k.reshape(DEPTH, DEC_BATCH, WINDOW, KV_DIM)
    cv = cache_v.reshape(DEPTH, DEC_BATCH, WINDOW, KV_DIM)

    pk, pv, ps, sk, sv, ss = [], [], [], [], [], []
    for l in range(DEPTH):
        z, w_in_b = _dec_proj(l, hs, nw, w_in)
        y, nk, nv, ns = _dec_mix(l, z, ck, cv, state_h, lb, oml, rw, sinks, bias_dc, bias_dn)
        hs, w_out_b = _dec_out(l, hs, y, w_out)
        sk.append(nk)
        sv.append(nv)
        ss.append(ns)

        outs = _prompt_layer(l, h_meta, h_main, w_in_b, w_out_b, nw, lb, oml, rw, sinks, bias_p, fw)
        if l < DEPTH - 1:
            h_meta, h_main, k_l, v_l, s_l = outs
        else:
            h_main, k_l, v_l, s_l = outs
        pk.append(k_l)
        pv.append(v_l)
        ps.append(s_l)

    y_prompt = h_main
    y_sample = _final_norm_dec(hs, fw).reshape(DEC_BATCH, DEC_SEQ, D_MODEL)
    kv_shape = (DEPTH, -1, WINDOW, N_KV, HEAD_DIM)
    return (y_prompt, y_sample,
            jnp.stack(pk).reshape(kv_shape), jnp.stack(pv).reshape(kv_shape),
            jnp.swapaxes(jnp.stack(ps), -1, -2),
            jnp.stack(sk).reshape(kv_shape), jnp.stack(sv).reshape(kv_shape), jnp.stack(ss))
```

```python
import functools
import math

import jax
import jax.numpy as jnp
from jax import lax
from jax.experimental import pallas as pl
from jax.experimental.pallas import tpu as pltpu

D_MODEL = 2048
BATCH = 4
SEQ = 2048
DEPTH = 4
DEC_BATCH = 32
DEC_SEQ = 8
N_META = 16
D_ATTN = 1024
D_REC = 1024
HEAD_DIM = 64
N_HEADS = 16
N_KV = 4
GROUP = 4
KV_DIM = 256
WINDOW = 128
N_BUCKETS = 32
MAX_DISTANCE = 128
REC_HEAD_DIM = 128
N_REC_HEADS = 8
EPS = 1e-6
D_IN = 2 * D_ATTN + 2 * KV_DIM + 4 * D_REC

OFF_KA = D_ATTN
OFF_VA = OFF_KA + KV_DIM
OFF_GA = OFF_VA + KV_DIM
OFF_QR = OFF_GA + D_ATTN
OFF_FR = OFF_QR + D_REC
OFF_IR = OFF_FR + D_REC
OFF_GR = OFF_IR + D_REC
ATT_COLS = OFF_QR

TILE = 256
QBLK = WINDOW
CHUNK = 128
HALF = CHUNK // 2
ROW_TILES = SEQ // TILE
PAD = TILE - N_META
PROJ_COLS = 512
REC_GROUP = PROJ_COLS // REC_HEAD_DIM
NEG = -1e30
MAX_FACTOR_EXPONENT = 80.0
N_DEC = DEC_BATCH * DEC_SEQ
V7X_LANES = 128
V7X_VMEM_BYTES = 64 * 1024 * 1024
MIB = 1024 * 1024
VMEM_LIMIT = V7X_VMEM_BYTES - MIB
DEC_VMEM_LIMIT = V7X_VMEM_BYTES - 8 * MIB

F32 = jnp.float32
BF16 = jnp.bfloat16


def _sigmoid_pair(x):
    e = jnp.exp(-jnp.abs(x))
    inv = 1.0 / (1.0 + e)
    pos = x >= 0
    return jnp.where(pos, inv, e * inv), jnp.where(pos, e * inv, inv)


def _silu(x):
    return x * (0.5 * jnp.tanh(0.5 * x) + 0.5)


def _cumsum_rows(x):
    n = x.shape[0]
    row = lax.broadcasted_iota(jnp.int32, x.shape, 0)
    k = 1
    while k < n:
        x = x + jnp.where(row >= k, pltpu.roll(x, k, axis=0), 0.0)
        k *= 2
    return x


def _rmsnorm_rows(x, w):
    ms = jnp.mean(x * x, axis=-1, keepdims=True)
    return (x * lax.rsqrt(ms + EPS)) * w


def _dot_nt(a, b):
    return lax.dot_general(a, b, (((1,), (1,)), ((), ())), preferred_element_type=F32)


def _dot_tn(a, b):
    return lax.dot_general(a, b, (((0,), (0,)), ((), ())), preferred_element_type=F32)


def _dot(a, b):
    return jnp.dot(a, b, preferred_element_type=F32)


def _gate_features(fr, lb, oml):
    sig, sneg = _sigmoid_pair(fr)
    logf = jnp.log(lb + oml * sig)
    return logf, oml * sneg


def _attention_unit(layer, qb, kh, za_ref, kv_ref, y_ref, bias_ref, sink_ref, c_tile):
    r0 = qb * QBLK
    kpos = c_tile + r0 - QBLK + lax.broadcasted_iota(jnp.int32, (1, 2 * QBLK), 1)
    kvalid = kpos >= 0
    kk = kv_ref[r0:r0 + 2 * QBLK, kh * HEAD_DIM:(kh + 1) * HEAD_DIM]
    q4 = jnp.concatenate(
        [za_ref[r0:r0 + QBLK, (kh * GROUP + g) * HEAD_DIM:(kh * GROUP + g + 1) * HEAD_DIM]
         for g in range(GROUP)], axis=0)
    s = _dot_nt((q4 * (HEAD_DIM ** -0.5)).astype(BF16), kk)
    yield
    s = s + bias_ref[kh * GROUP * QBLK:(kh + 1) * GROUP * QBLK, :].astype(F32)
    s = jnp.where(kvalid, s, NEG)
    es, dens = [], []
    for g in range(GROUP):
        sink = sink_ref[layer, kh * GROUP + g]
        sg = s[g * QBLK:(g + 1) * QBLK]
        m = jnp.maximum(jnp.max(sg, axis=-1, keepdims=True), sink)
        e = jnp.exp(sg - m)
        dens.append(jnp.sum(e, axis=-1, keepdims=True) + jnp.exp(sink - m))
        es.append(e.astype(BF16))
    vv = kv_ref[r0:r0 + 2 * QBLK, KV_DIM + kh * HEAD_DIM:KV_DIM + (kh + 1) * HEAD_DIM]
    o4 = _dot(jnp.concatenate(es, axis=0), vv)
    yield
    outs = []
    for g in range(GROUP):
        h = kh * GROUP + g
        og = o4[g * QBLK:(g + 1) * QBLK] / dens[g]
        ga = za_ref[r0:r0 + QBLK, OFF_GA + h * HEAD_DIM:OFF_GA + (h + 1) * HEAD_DIM]
        outs.append(og * _silu(ga))
    y_ref[r0:r0 + QBLK, kh * GROUP * HEAD_DIM:(kh + 1) * GROUP * HEAD_DIM] = (
        jnp.concatenate(outs, axis=-1).astype(BF16))


def _decay_scores_direct(q, kr, a):
    n = q.shape[0]
    row = lax.broadcasted_iota(jnp.int32, (n, n), 0)
    col = lax.broadcasted_iota(jnp.int32, (n, n), 1)

    def diagonal(d, acc):
        ks = pltpu.roll(kr, d, axis=0)
        a_s = pltpu.roll(a, d, axis=0)
        p = q * ks * jnp.exp(jnp.minimum(a - a_s, 0.0))
        return jnp.where(col == row - d, jnp.sum(p, axis=-1, keepdims=True), acc)

    acc = jnp.zeros((n, n), F32)
    if n <= DEC_SEQ:
        for d in range(n):
            acc = diagonal(d, acc)
        return acc
    return lax.fori_loop(0, n, diagonal, acc)


def _hgrn2_unit(ci, hh, zr_ref, st_ref, y_ref, lb_ref, oml_ref, recw_ref, c_tile, spans=None, heads_in_buf=None):
    heads_in_buf = REC_GROUP if heads_in_buf is None else heads_in_buf
    part = heads_in_buf * REC_HEAD_DIM
    row = lax.broadcasted_iota(jnp.int32, (CHUNK, CHUNK), 0)
    col = lax.broadcasted_iota(jnp.int32, (CHUNK, CHUNK), 1)
    causal = col <= row
    r0 = ci * CHUNK if isinstance(ci, int) else pl.multiple_of(ci * CHUNK, CHUNK)
    rows = pl.ds(r0, CHUNK)
    rpos = c_tile + r0 + lax.broadcasted_iota(jnp.int32, (CHUNK, 1), 0)
    rvalid = rpos >= 0
    if isinstance(hh, int):
        cs = hh * REC_HEAD_DIM
        zc = (hh % heads_in_buf) * REC_HEAD_DIM
    else:
        cs = pl.multiple_of(hh * REC_HEAD_DIM, REC_HEAD_DIM)
        zc = 0
    head_cols = pl.ds(cs, REC_HEAD_DIM)
    lb = lb_ref[:, head_cols]
    oml = oml_ref[:, head_cols]
    logf, kr = _gate_features(zr_ref[rows, part + zc:part + zc + REC_HEAD_DIM], lb, oml)
    q = _silu(zr_ref[rows, zc:zc + REC_HEAD_DIM])
    v = jnp.where(rvalid, zr_ref[rows, 2 * part + zc:2 * part + zc + REC_HEAD_DIM], 0.0)
    vb = v.astype(BF16)
    a = _cumsum_rows(logf)
    a_mid = a[HALF - 1:HALF, :]
    a_end = a[CHUNK - 1:CHUNK, :]
    if spans is None:
        scores = _decay_scores_direct(q, kr, a)
    else:
        qh = (q * jnp.exp(a - a_mid)).astype(BF16)
        kh_ = (kr * jnp.exp(a_mid - a)).astype(BF16)
        scores = _dot_nt(qh, kh_)
        spans.append(jnp.maximum(a[0:1, :] - a_mid, a_mid - a_end))
    qt = (q * jnp.exp(a)).astype(BF16)
    kt = (kr * jnp.exp(a_end - a)).astype(BF16)
    st = st_ref[hh]
    o_state = _dot_nt(qt, st.astype(BF16))
    yield
    att = jnp.where(causal, scores, 0.0)
    o = _dot(att.astype(BF16), vb) + o_state
    st_ref[hh] = st * jnp.exp(a_end) + _dot_tn(vb, kt)
    yield
    on = o * lax.rsqrt(jnp.mean(o * o, axis=-1, keepdims=True) + EPS)
    gr = zr_ref[rows, 3 * part + zc:3 * part + zc + REC_HEAD_DIM]
    yr = (on * recw_ref[:, head_cols]) * _silu(gr)
    y_ref[rows, pl.ds(D_ATTN + cs, REC_HEAD_DIM)] = yr.astype(BF16)


def _emit_staggered(units, fillers=(), lag=1):
    n = len(units)
    rounds = n + 2 * lag
    at_round = [[] for _ in range(rounds)]
    for k, f in enumerate(fillers):
        at_round[k * rounds // len(fillers)].append(f)
    for i in range(rounds):
        if i < n:
            next(units[i])
        for f in at_round[i]:
            f()
        if 0 <= i - lag < n:
            next(units[i - lag])
        if 0 <= i - 2 * lag < n:
            next(units[i - 2 * lag], None)


def _main_tile(s):
    t = jnp.maximum(s - 1, 0)
    return lax.div(t, ROW_TILES), lax.rem(t, ROW_TILES)


def _prompt_layer_kernel(layer, final, sink_ref, meta_ref, hin_ref, win_ref, wout_ref, normw_ref, lb_ref, oml_ref,
                         recw_ref, bias_ref, finw_ref, *refs):
    if final:
        hout_ref, pk_ref, pv_ref, ps_ref = refs[:4]
        refs = refs[4:]
    else:
        meta_out_ref, hout_ref, pk_ref, pv_ref, ps_ref = refs[:5]
        refs = refs[5:]
    xn_ref, za_ref, zr_ref, kv_ref, st_ref, y_ref, kv_meta_ref, st_meta_ref, st_start_ref, zhead_ref = refs
    s = pl.program_id(0)
    is_meta = s == 0
    _, j = _main_tile(s)

    @pl.when(is_meta)
    def _():
        kv_ref[0:QBLK, :] = jnp.zeros((QBLK, 2 * KV_DIM), BF16)
        st_ref[...] = jnp.zeros(st_ref.shape, F32)

    @pl.when(jnp.logical_and(jnp.logical_not(is_meta), j == 0))
    def _():
        kv_ref[0:QBLK, :] = kv_meta_ref[...]
        st_ref[...] = st_meta_ref[...]

    st_start_ref[...] = st_ref[...]

    def tile_input(cols=slice(None)):
        meta = meta_ref[:, cols]
        meta_rows = jnp.concatenate([jnp.zeros((PAD, meta.shape[1]), F32), meta], axis=0)
        return jnp.where(is_meta, meta_rows, hin_ref[0, :, cols])

    c_tile = jnp.where(is_meta, -PAD, j * TILE + N_META)
    xn_ref[...] = _rmsnorm_rows(tile_input(), normw_ref[...]).astype(BF16)

    def project(dst_ref, dst_col, src_col):
        dst_ref[:, pl.ds(dst_col, PROJ_COLS)] = _dot(xn_ref[...], win_ref[:, pl.ds(src_col, PROJ_COLS)])

    def project_rec(dst_ref, group):
        return [functools.partial(project, dst_ref, part * PROJ_COLS, off + group * PROJ_COLS)
                for part, off in enumerate((OFF_QR, OFF_FR, OFF_IR, OFF_GR))]

    def out_proj(k0, k1, c0, first_half):
        cols = pl.ds(c0, PROJ_COLS)
        base = tile_input(cols) if first_half else hout_ref[0, :, cols]
        hout_ref[0, :, cols] = base + _dot(y_ref[:, k0:k1], wout_ref[k0:k1, cols])

    def column_blocks(n, body):
        def step(i, carry):
            body(pl.multiple_of(i * PROJ_COLS, PROJ_COLS))
            return carry
        lax.fori_loop(0, n, step, 0)

    for c0 in range(0, ATT_COLS, PROJ_COLS):
        project(za_ref, c0, c0)
    kv_ref[QBLK:QBLK + TILE, :] = za_ref[:, OFF_KA:OFF_KA + 2 * KV_DIM].astype(BF16)
    pk_ref[0] = za_ref[TILE - WINDOW:TILE, OFF_KA:OFF_KA + KV_DIM]
    pv_ref[0] = za_ref[TILE - WINDOW:TILE, OFF_VA:OFF_VA + KV_DIM]

    att_units = [_attention_unit(layer, qb, kh, za_ref, kv_ref, y_ref, bias_ref, sink_ref, c_tile)
                 for qb in range(TILE // QBLK) for kh in range(N_KV)]
    _emit_staggered(att_units, project_rec(zr_ref, 0))
    kv_ref[0:QBLK, :] = kv_ref[TILE:TILE + QBLK, :]

    n_groups = N_REC_HEADS // REC_GROUP
    z_bufs = [zr_ref, za_ref]
    spans = []

    def rec_units(grp, state_ref, spans_):
        return [_hgrn2_unit(ci, grp * REC_GROUP + hl, z_bufs[grp % 2], state_ref, y_ref, lb_ref, oml_ref,
                            recw_ref, c_tile, spans_)
                for ci in range(TILE // CHUNK) for hl in range(REC_GROUP)]

    for grp in range(n_groups):
        if grp + 1 < n_groups:
            fillers = project_rec(z_bufs[(grp + 1) % 2], grp + 1)
        else:
            fillers = [functools.partial(out_proj, 0, D_ATTN, c0, True) for c0 in range(0, D_MODEL, PROJ_COLS)]
        _emit_staggered(rec_units(grp, st_ref, spans), fillers)
    widest = jnp.max(jnp.concatenate(spans, axis=0))
    for c0 in range(0, D_MODEL, PROJ_COLS):
        out_proj(D_ATTN, D_MODEL, c0, False)

    @pl.when(widest > MAX_FACTOR_EXPONENT)
    def _():
        def replay_chunk(hh, ci, carry):
            for _ in _hgrn2_unit(ci, hh, zhead_ref, st_start_ref, y_ref, lb_ref, oml_ref, recw_ref, c_tile,
                                 spans=None, heads_in_buf=1):
                pass
            return carry

        def replay_head(hh, carry):
            for part, off in enumerate((OFF_QR, OFF_FR, OFF_IR, OFF_GR)):
                src = pl.multiple_of(off + hh * REC_HEAD_DIM, REC_HEAD_DIM)
                zhead_ref[:, part * REC_HEAD_DIM:(part + 1) * REC_HEAD_DIM] = _dot(
                    xn_ref[...], win_ref[:, pl.ds(src, REC_HEAD_DIM)])
            return lax.fori_loop(0, TILE // CHUNK, functools.partial(replay_chunk, hh), carry)

        lax.fori_loop(0, N_REC_HEADS, replay_head, 0)
        column_blocks(D_MODEL // PROJ_COLS, lambda c0: out_proj(0, D_MODEL, c0, True))

    ps_ref[0] = st_ref[...]

    @pl.when(is_meta)
    def _():
        kv_meta_ref[...] = kv_ref[0:QBLK, :]
        st_meta_ref[...] = st_ref[...]
        if not final:
            meta_out_ref[...] = hout_ref[0, PAD:TILE, :]

    if final:
        @pl.when(jnp.logical_not(is_meta))
        def _():
            hout_ref[0] = _rmsnorm_rows(hout_ref[0], finw_ref[...])


def _layer_block(shape, layer):
    nd = len(shape)
    return pl.BlockSpec((None,) + tuple(shape), lambda s: (layer,) + (0,) * nd, pipeline_mode=pl.Buffered(1))


def _const_block(shape):
    nd = len(shape)
    return pl.BlockSpec(tuple(shape), lambda s: (0,) * nd, pipeline_mode=pl.Buffered(1))


def _prompt_layer(layer, h_meta, h_main, w_in, w_out, norm_w, lb, oml, rec_w, sinks, bias, fin_w):
    final = layer == DEPTH - 1
    main = pl.BlockSpec((1, TILE, D_MODEL), lambda s: _main_tile(s) + (0,))
    meta = pl.BlockSpec((N_META, D_MODEL), lambda s: (0, 0))
    kv_spec = pl.BlockSpec((1, WINDOW, KV_DIM), lambda s: (_main_tile(s)[0], 0, 0))
    st_spec = pl.BlockSpec((1, N_REC_HEADS, REC_HEAD_DIM, REC_HEAD_DIM), lambda s: (_main_tile(s)[0], 0, 0, 0))
    main_shape = jax.ShapeDtypeStruct((BATCH, SEQ, D_MODEL), F32)
    meta_shape = jax.ShapeDtypeStruct((N_META, D_MODEL), F32)
    return pl.pallas_call(
        functools.partial(_prompt_layer_kernel, layer, final),
        grid=(1 + BATCH * ROW_TILES,),
        in_specs=[
            pl.BlockSpec(memory_space=pltpu.SMEM),
            _const_block((N_META, D_MODEL)),
            main,
            _const_block((D_MODEL, D_IN)),
            _const_block((D_MODEL, D_MODEL)),
            _layer_block((1, D_MODEL), layer),
            _layer_block((1, D_REC), layer),
            _layer_block((1, D_REC), layer),
            _layer_block((1, D_REC), layer),
            _const_block((N_HEADS * QBLK, 2 * QBLK)),
            _const_block((1, D_MODEL)),
        ],
        out_specs=([] if final else [meta]) + [main, kv_spec, kv_spec, st_spec],
        out_shape=([] if final else [meta_shape]) + [
            main_shape,
            jax.ShapeDtypeStruct((BATCH, WINDOW, KV_DIM), F32),
            jax.ShapeDtypeStruct((BATCH, WINDOW, KV_DIM), F32),
            jax.ShapeDtypeStruct((BATCH, N_REC_HEADS, REC_HEAD_DIM, REC_HEAD_DIM), F32),
        ],
        scratch_shapes=[
            pltpu.VMEM((TILE, D_MODEL), BF16),
            pltpu.VMEM((TILE, ATT_COLS), F32),
            pltpu.VMEM((TILE, 4 * PROJ_COLS), F32),
            pltpu.VMEM((QBLK + TILE, 2 * KV_DIM), BF16),
            pltpu.VMEM((N_REC_HEADS, REC_HEAD_DIM, REC_HEAD_DIM), F32),
            pltpu.VMEM((TILE, D_MODEL), BF16),
            pltpu.VMEM((QBLK, 2 * KV_DIM), BF16),
            pltpu.VMEM((N_REC_HEADS, REC_HEAD_DIM, REC_HEAD_DIM), F32),
            pltpu.VMEM((N_REC_HEADS, REC_HEAD_DIM, REC_HEAD_DIM), F32),
            pltpu.VMEM((TILE, 4 * REC_HEAD_DIM), F32),
        ],
        compiler_params=pltpu.CompilerParams(
            dimension_semantics=("arbitrary",), vmem_limit_bytes=VMEM_LIMIT),
        name="prompt_layer",
    )(sinks, h_meta, h_main, w_in, w_out, norm_w, lb, oml, rec_w, bias, fin_w)


DEC_PROJ_COLS = D_IN // 4
DEC_OUT_COLS = D_MODEL // 2


def _dec_proj_kernel(hs_ref, normw_ref, win_ref, z_ref, wb_ref):
    xn = _rmsnorm_rows(hs_ref[...], normw_ref[...]).astype(BF16)
    for c0 in range(0, DEC_PROJ_COLS, V7X_LANES):
        cols = slice(c0, c0 + V7X_LANES)
        wb = win_ref[:, cols].astype(BF16)
        wb_ref[:, cols] = wb
        z_ref[:, cols] = _dot(xn, wb)


def _dec_proj(layer, hs, norm_w, w_in):
    return pl.pallas_call(
        _dec_proj_kernel,
        grid=(D_IN // DEC_PROJ_COLS,),
        in_specs=[
            pl.BlockSpec((N_DEC, D_MODEL), lambda n: (0, 0)),
            pl.BlockSpec((None, 1, D_MODEL), lambda n: (layer, 0, 0)),
            pl.BlockSpec((None, D_MODEL, DEC_PROJ_COLS), lambda n: (layer, 0, n)),
        ],
        out_specs=[pl.BlockSpec((N_DEC, DEC_PROJ_COLS), lambda n: (0, n)),
                   pl.BlockSpec((D_MODEL, DEC_PROJ_COLS), lambda n: (0, n))],
        out_shape=[jax.ShapeDtypeStruct((N_DEC, D_IN), F32),
                   jax.ShapeDtypeStruct((D_MODEL, D_IN), BF16)],
        compiler_params=pltpu.CompilerParams(dimension_semantics=("arbitrary",), vmem_limit_bytes=DEC_VMEM_LIMIT),
        name="dec_proj",
    )(hs, norm_w, w_in)


def _dec_out_kernel(hs_ref, y_ref, wout_ref, o_ref, wb_ref):
    for c0 in range(0, DEC_OUT_COLS, 2 * V7X_LANES):
        cols = slice(c0, c0 + 2 * V7X_LANES)
        wb = wout_ref[:, cols].astype(BF16)
        wb_ref[:, cols] = wb
        o_ref[:, cols] = hs_ref[:, cols] + _dot(y_ref[...], wb)


def _dec_out(layer, hs, y, w_out):
    return pl.pallas_call(
        _dec_out_kernel,
        grid=(D_MODEL // DEC_OUT_COLS,),
        in_specs=[
            pl.BlockSpec((N_DEC, DEC_OUT_COLS), lambda n: (0, n)),
            pl.BlockSpec((N_DEC, D_MODEL), lambda n: (0, 0)),
            pl.BlockSpec((None, D_MODEL, DEC_OUT_COLS), lambda n: (layer, 0, n)),
        ],
        out_specs=[pl.BlockSpec((N_DEC, DEC_OUT_COLS), lambda n: (0, n)),
                   pl.BlockSpec((D_MODEL, DEC_OUT_COLS), lambda n: (0, n))],
        out_shape=[jax.ShapeDtypeStruct((N_DEC, D_MODEL), F32),
                   jax.ShapeDtypeStruct((D_MODEL, D_MODEL), BF16)],
        compiler_params=pltpu.CompilerParams(dimension_semantics=("arbitrary",), vmem_limit_bytes=DEC_VMEM_LIMIT),
        name="dec_out",
    )(hs, y, w_out)


DEC_SEQS_PER_STEP = 4
DEC_LAG = DEC_SEQS_PER_STEP * (N_KV + N_REC_HEADS)


def _dec_attention_unit(layer, q, kh, sink_ref, z_ref, ck_ref, cv_ref, biasc_ref, biasn_ref, y_ref):
    t = DEC_SEQ
    tok = slice(q * t, (q + 1) * t)
    hs_ = slice(kh * HEAD_DIM, (kh + 1) * HEAD_DIM)
    kc = ck_ref[q, :, hs_].astype(BF16)
    kn = z_ref[tok, OFF_KA + kh * HEAD_DIM:OFF_KA + (kh + 1) * HEAD_DIM]
    q4 = jnp.concatenate(
        [z_ref[tok, (kh * GROUP + g) * HEAD_DIM:(kh * GROUP + g + 1) * HEAD_DIM] for g in range(GROUP)],
        axis=0) * (HEAD_DIM ** -0.5)
    rows = slice(kh * GROUP * t, (kh + 1) * GROUP * t)
    sc = _dot_nt(q4.astype(BF16), kc)
    sn = _dot_nt(q4, kn)
    yield
    sc = sc + biasc_ref[rows, :]
    sn = sn + biasn_ref[rows, :]
    ecs, ens, dens = [], [], []
    for g in range(GROUP):
        sink = sink_ref[layer, kh * GROUP + g]
        scg = sc[g * t:(g + 1) * t]
        sng = sn[g * t:(g + 1) * t]
        m = jnp.maximum(jnp.maximum(jnp.max(scg, axis=-1, keepdims=True),
                                    jnp.max(sng, axis=-1, keepdims=True)), sink)
        ec = jnp.exp(scg - m)
        en = jnp.exp(sng - m)
        dens.append(jnp.sum(ec, axis=-1, keepdims=True) + jnp.sum(en, axis=-1, keepdims=True)
                    + jnp.exp(sink - m))
        ecs.append(ec)
        ens.append(en)
    vc = cv_ref[q, :, hs_]
    vn = z_ref[tok, OFF_VA + kh * HEAD_DIM:OFF_VA + (kh + 1) * HEAD_DIM]
    o4 = (_dot(jnp.concatenate(ecs, axis=0).astype(BF16), vc.astype(BF16))
          + _dot(jnp.concatenate(ens, axis=0), vn))
    yield
    outs = []
    for g in range(GROUP):
        h = kh * GROUP + g
        og = o4[g * t:(g + 1) * t] / dens[g]
        ga = z_ref[tok, OFF_GA + h * HEAD_DIM:OFF_GA + (h + 1) * HEAD_DIM]
        outs.append(og * _silu(ga))
    y_ref[tok, kh * GROUP * HEAD_DIM:(kh + 1) * GROUP * HEAD_DIM] = jnp.concatenate(outs, axis=-1).astype(BF16)


def _dec_hgrn2_unit(q, hh, z_ref, s0_ref, lb_ref, oml_ref, recw_ref, y_ref, ns_ref):
    t = DEC_SEQ
    tok = slice(q * t, (q + 1) * t)
    row = lax.broadcasted_iota(jnp.int32, (t, t), 0)
    col = lax.broadcasted_iota(jnp.int32, (t, t), 1)
    causal = col <= row
    cs = hh * REC_HEAD_DIM
    lb = lb_ref[:, cs:cs + REC_HEAD_DIM]
    oml = oml_ref[:, cs:cs + REC_HEAD_DIM]
    logf, kr = _gate_features(z_ref[tok, OFF_FR + cs:OFF_FR + cs + REC_HEAD_DIM], lb, oml)
    qv = _silu(z_ref[tok, OFF_QR + cs:OFF_QR + cs + REC_HEAD_DIM])
    v = z_ref[tok, OFF_IR + cs:OFF_IR + cs + REC_HEAD_DIM]
    a = _cumsum_rows(logf)
    a_end = a[t - 1:t, :]
    scores = _decay_scores_direct(qv, kr, a)
    qt = qv * jnp.exp(a)
    kt = kr * jnp.exp(a_end - a)
    s0 = s0_ref[q, hh]
    o_state = _dot(qt, s0)
    yield
    att = jnp.where(causal, scores, 0.0)
    o = _dot(att, v) + o_state
    ktd = jnp.concatenate([kt, jnp.exp(a_end), jnp.zeros((t - 1, REC_HEAD_DIM), F32)], axis=0).T
    ns_ref[q, hh] = s0 * ktd[:, t:t + 1] + _dot(ktd[:, 0:t], v)
    yield
    on = o * lax.rsqrt(jnp.mean(o * o, axis=-1, keepdims=True) + EPS)
    gr = z_ref[tok, OFF_GR + cs:OFF_GR + cs + REC_HEAD_DIM]
    yr = (on * recw_ref[:, cs:cs + REC_HEAD_DIM]) * _silu(gr)
    y_ref[tok, D_ATTN + cs:D_ATTN + cs + REC_HEAD_DIM] = yr.astype(BF16)


def _dec_mix_kernel(layer, sink_ref, z_ref, ck_ref, cv_ref, s0_ref, lb_ref, oml_ref, recw_ref, biasc_ref,
                    biasn_ref, out_k_hbm, out_v_hbm, out_s_hbm, y_ref, nk_ref, nv_ref, ns_ref):
    del out_k_hbm, out_v_hbm, out_s_hbm
    t = DEC_SEQ
    for q in range(DEC_SEQS_PER_STEP):
        tok = slice(q * t, (q + 1) * t)
        nk_ref[q, 0:WINDOW - t, :] = ck_ref[q, t:WINDOW, :]
        nk_ref[q, WINDOW - t:WINDOW, :] = z_ref[tok, OFF_KA:OFF_KA + KV_DIM]
        nv_ref[q, 0:WINDOW - t, :] = cv_ref[q, t:WINDOW, :]
        nv_ref[q, WINDOW - t:WINDOW, :] = z_ref[tok, OFF_VA:OFF_VA + KV_DIM]
    units = []
    for kh in range(N_KV):
        units += [_dec_attention_unit(layer, q, kh, sink_ref, z_ref, ck_ref, cv_ref, biasc_ref, biasn_ref, y_ref)
                  for q in range(DEC_SEQS_PER_STEP)]
    for hh in range(N_REC_HEADS):
        units += [_dec_hgrn2_unit(q, hh, z_ref, s0_ref, lb_ref, oml_ref, recw_ref, y_ref, ns_ref)
                  for q in range(DEC_SEQS_PER_STEP)]
    _emit_staggered(units, lag=DEC_LAG)


def _dec_mix(layer, z, ck, cv, s0, lb, oml, rec_w, sinks, bias_c, bias_n, out_k, out_v, out_s):
    t = DEC_SEQ
    nq = DEC_SEQS_PER_STEP
    const2 = lambda s: (0, 0)
    per_layer = lambda s: (layer, 0, 0)
    kv_in = pl.BlockSpec((None, nq, WINDOW, KV_DIM), lambda s: (layer, s, 0, 0))
    st_in = pl.BlockSpec((None, nq, N_REC_HEADS, REC_HEAD_DIM, REC_HEAD_DIM), lambda s: (layer, s, 0, 0, 0))
    return pl.pallas_call(
        functools.partial(_dec_mix_kernel, layer),
        grid=(DEC_BATCH // nq,),
        in_specs=[
            pl.BlockSpec(memory_space=pltpu.SMEM),
            pl.BlockSpec((nq * t, D_IN), lambda s: (s, 0)),
            kv_in,
            kv_in,
            st_in,
            pl.BlockSpec((None, 1, D_REC), per_layer),
            pl.BlockSpec((None, 1, D_REC), per_layer),
            pl.BlockSpec((None, 1, D_REC), per_layer),
            pl.BlockSpec((N_HEADS * t, WINDOW), const2),
            pl.BlockSpec((N_HEADS * t, t), const2),
            pl.BlockSpec(memory_space=pl.ANY),
            pl.BlockSpec(memory_space=pl.ANY),
            pl.BlockSpec(memory_space=pl.ANY),
        ],
        out_specs=[
            pl.BlockSpec((nq * t, D_MODEL), lambda s: (s, 0)),
            kv_in,
            kv_in,
            st_in,
        ],
        out_shape=[
            jax.ShapeDtypeStruct((N_DEC, D_MODEL), BF16),
            jax.ShapeDtypeStruct(out_k.shape, F32),
            jax.ShapeDtypeStruct(out_v.shape, F32),
            jax.ShapeDtypeStruct(out_s.shape, F32),
        ],
        input_output_aliases={10: 1, 11: 2, 12: 3},
        compiler_params=pltpu.CompilerParams(dimension_semantics=("arbitrary",)),
        name="dec_mix",
    )(sinks, z, ck, cv, s0, lb, oml, rec_w, bias_c, bias_n, out_k, out_v, out_s)


def _final_norm_kernel(x_ref, w_ref, o_ref):
    o_ref[...] = _rmsnorm_rows(x_ref[...], w_ref[...])


def _final_norm_dec(hs, w):
    return pl.pallas_call(
        _final_norm_kernel,
        out_shape=jax.ShapeDtypeStruct((N_DEC, D_MODEL), F32),
        name="final_norm_dec",
    )(hs, w)


def _t5_bucket(dist):
    max_exact = N_BUCKETS // 2
    d = jnp.maximum(dist, 0)
    df = jnp.maximum(d, 1).astype(F32)
    large = max_exact + (jnp.log(df / max_exact) / math.log(MAX_DISTANCE / max_exact)
                         * (N_BUCKETS - max_exact)).astype(jnp.int32)
    large = jnp.minimum(large, N_BUCKETS - 1)
    return jnp.where(d < max_exact, d, large)


def _bias_tables(rel_bias_table):
    bias_d = rel_bias_table[_t5_bucket(jnp.arange(WINDOW))].T.astype(F32)
    period = 3 * WINDOW + 1
    u = jnp.concatenate([jnp.full((N_HEADS, 1), NEG, F32), bias_d[:, ::-1],
                         jnp.full((N_HEADS, period - 1 - WINDOW), NEG, F32)], axis=1)
    rows = jnp.tile(u, (1, WINDOW))[:, :WINDOW * (period - 1)].reshape(N_HEADS, WINDOW, period - 1)
    prompt = rows[:, :, :2 * WINDOW].reshape(N_HEADS * WINDOW, 2 * WINDOW).astype(BF16)
    dec = rows[:, :DEC_SEQ, :WINDOW + DEC_SEQ].reshape(N_HEADS * DEC_SEQ, WINDOW + DEC_SEQ)
    return prompt, dec[:, :WINDOW], dec[:, WINDOW:]


def kernel(x_prompt, x_sample, cache_k, cache_v, state_h, meta_tokens, w_in, w_out, norm_w, final_norm_w,
           attn_sinks, rel_bias_table, hgrn_lb_logits, hgrn_norm_w):
    pl_ = jax.nn.softmax(hgrn_lb_logits.astype(F32), axis=0)
    lb = (jnp.cumsum(pl_, axis=0) - pl_[0:1]).reshape(DEPTH, 1, D_REC)
    oml = 1.0 - lb
    bias_p, bias_dc, bias_dn = _bias_tables(rel_bias_table)
    nw = norm_w.astype(F32).reshape(DEPTH, 1, D_MODEL)
    rw = hgrn_norm_w.astype(F32).reshape(DEPTH, 1, D_REC)
    fw = final_norm_w.astype(F32).reshape(1, D_MODEL)
    sinks = attn_sinks.astype(F32)

    h_meta = meta_tokens.astype(F32)
    h_main = x_prompt
    hs = x_sample.reshape(N_DEC, D_MODEL)
    ck = cache_k.reshape(DEPTH, DEC_BATCH, WINDOW, KV_DIM)
    cv = cache_v.reshape(DEPTH, DEC_BATCH, WINDOW, KV_DIM)

    pk, pv, ps = [], [], []
    sk = jnp.zeros(ck.shape, F32)
    sv = jnp.zeros(cv.shape, F32)
    ss = jnp.zeros(state_h.shape, F32)
    for l in range(DEPTH):
        z, w_in_b = _dec_proj(l, hs, nw, w_in)
        y, sk, sv, ss = _dec_mix(l, z, ck, cv, state_h, lb, oml, rw, sinks, bias_dc, bias_dn, sk, sv, ss)
        hs, w_out_b = _dec_out(l, hs, y, w_out)

        outs = _prompt_layer(l, h_meta, h_main, w_in_b, w_out_b, nw, lb, oml, rw, sinks, bias_p, fw)
        if l < DEPTH - 1:
            h_meta, h_main, k_l, v_l, s_l = outs
        else:
            h_main, k_l, v_l, s_l = outs
        pk.append(k_l)
        pv.append(v_l)
        ps.append(s_l)

    y_prompt = h_main
    y_sample = _final_norm_dec(hs, fw).reshape(DEC_BATCH, DEC_SEQ, D_MODEL)
    kv_shape = (DEPTH, -1, WINDOW, N_KV, HEAD_DIM)
    return (y_prompt, y_sample,
            jnp.stack(pk).reshape(kv_shape), jnp.stack(pv).reshape(kv_shape),
            jnp.swapaxes(jnp.stack(ps), -1, -2),
            sk.reshape(kv_shape), sv.reshape(kv_shape), ss)
```

```python
import functools
import math

import jax
import jax.numpy as jnp
from jax import lax
from jax.experimental import pallas as pl
from jax.experimental.pallas import tpu as pltpu

D_MODEL = 2048
BATCH = 4
SEQ = 2048
DEPTH = 4
DEC_BATCH = 32
DEC_SEQ = 8
N_META = 16
D_ATTN = 1024
D_REC = 1024
HEAD_DIM = 64
N_HEADS = 16
N_KV = 4
GROUP = 4
KV_DIM = 256
WINDOW = 128
N_BUCKETS = 32
MAX_DISTANCE = 128
REC_HEAD_DIM = 128
N_REC_HEADS = 8
EPS = 1e-6
D_IN = 2 * D_ATTN + 2 * KV_DIM + 4 * D_REC

OFF_KA = D_ATTN
OFF_VA = OFF_KA + KV_DIM
OFF_GA = OFF_VA + KV_DIM
OFF_QR = OFF_GA + D_ATTN
OFF_FR = OFF_QR + D_REC
OFF_IR = OFF_FR + D_REC
OFF_GR = OFF_IR + D_REC
ATT_COLS = OFF_QR

TILE = 256
QBLK = WINDOW
CHUNK = 128
HALF = CHUNK // 2
ROW_TILES = SEQ // TILE
PAD = TILE - N_META
PROJ_COLS = 512
REC_GROUP = PROJ_COLS // REC_HEAD_DIM
NEG = -1e30
MAX_FACTOR_EXPONENT = 80.0
N_DEC = DEC_BATCH * DEC_SEQ
V7X_LANES = 128
V7X_VMEM_BYTES = 64 * 1024 * 1024
MIB = 1024 * 1024
VMEM_LIMIT = V7X_VMEM_BYTES - MIB
DEC_VMEM_LIMIT = V7X_VMEM_BYTES - 8 * MIB

F32 = jnp.float32
BF16 = jnp.bfloat16


def _sigmoid_pair(x):
    e = jnp.exp(-jnp.abs(x))
    inv = 1.0 / (1.0 + e)
    pos = x >= 0
    return jnp.where(pos, inv, e * inv), jnp.where(pos, e * inv, inv)


def _silu(x):
    return x * (0.5 * jnp.tanh(0.5 * x) + 0.5)


def _cumsum_rows(x):
    n = x.shape[0]
    row = lax.broadcasted_iota(jnp.int32, x.shape, 0)
    k = 1
    while k < n:
        x = x + jnp.where(row >= k, pltpu.roll(x, k, axis=0), 0.0)
        k *= 2
    return x


def _rmsnorm_rows(x, w):
    ms = jnp.mean(x * x, axis=-1, keepdims=True)
    return (x * lax.rsqrt(ms + EPS)) * w


def _dot_nt(a, b):
    return lax.dot_general(a, b, (((1,), (1,)), ((), ())), preferred_element_type=F32)


def _dot_tn(a, b):
    return lax.dot_general(a, b, (((0,), (0,)), ((), ())), preferred_element_type=F32)


def _dot(a, b):
    return jnp.dot(a, b, preferred_element_type=F32)


def _gate_features(fr, lb, oml):
    sig, sneg = _sigmoid_pair(fr)
    logf = jnp.log(lb + oml * sig)
    return logf, oml * sneg


def _attention_unit(layer, qb, kh, za_ref, kv_ref, y_ref, bias_ref, sink_ref, c_tile):
    r0 = qb * QBLK
    kpos = c_tile + r0 - QBLK + lax.broadcasted_iota(jnp.int32, (1, 2 * QBLK), 1)
    kvalid = kpos >= 0
    kk = kv_ref[r0:r0 + 2 * QBLK, kh * HEAD_DIM:(kh + 1) * HEAD_DIM]
    q4 = jnp.concatenate(
        [za_ref[r0:r0 + QBLK, (kh * GROUP + g) * HEAD_DIM:(kh * GROUP + g + 1) * HEAD_DIM]
         for g in range(GROUP)], axis=0)
    s = _dot_nt((q4 * (HEAD_DIM ** -0.5)).astype(BF16), kk)
    yield
    s = s + bias_ref[kh * GROUP * QBLK:(kh + 1) * GROUP * QBLK, :].astype(F32)
    s = jnp.where(kvalid, s, NEG)
    es, dens = [], []
    for g in range(GROUP):
        sink = sink_ref[layer, kh * GROUP + g]
        sg = s[g * QBLK:(g + 1) * QBLK]
        m = jnp.maximum(jnp.max(sg, axis=-1, keepdims=True), sink)
        e = jnp.exp(sg - m)
        dens.append(jnp.sum(e, axis=-1, keepdims=True) + jnp.exp(sink - m))
        es.append(e.astype(BF16))
    vv = kv_ref[r0:r0 + 2 * QBLK, KV_DIM + kh * HEAD_DIM:KV_DIM + (kh + 1) * HEAD_DIM]
    o4 = _dot(jnp.concatenate(es, axis=0), vv)
    yield
    outs = []
    for g in range(GROUP):
        h = kh * GROUP + g
        og = o4[g * QBLK:(g + 1) * QBLK] / dens[g]
        ga = za_ref[r0:r0 + QBLK, OFF_GA + h * HEAD_DIM:OFF_GA + (h + 1) * HEAD_DIM]
        outs.append(og * _silu(ga))
    y_ref[r0:r0 + QBLK, kh * GROUP * HEAD_DIM:(kh + 1) * GROUP * HEAD_DIM] = (
        jnp.concatenate(outs, axis=-1).astype(BF16))


def _decay_scores_direct(q, kr, a):
    n = q.shape[0]
    row = lax.broadcasted_iota(jnp.int32, (n, n), 0)
    col = lax.broadcasted_iota(jnp.int32, (n, n), 1)

    def diagonal(d, acc):
        ks = pltpu.roll(kr, d, axis=0)
        a_s = pltpu.roll(a, d, axis=0)
        p = q * ks * jnp.exp(jnp.minimum(a - a_s, 0.0))
        return jnp.where(col == row - d, jnp.sum(p, axis=-1, keepdims=True), acc)

    acc = jnp.zeros((n, n), F32)
    if n <= DEC_SEQ:
        for d in range(n):
            acc = diagonal(d, acc)
        return acc
    return lax.fori_loop(0, n, diagonal, acc)


def _hgrn2_unit(ci, hh, zr_ref, st_ref, y_ref, lb_ref, oml_ref, recw_ref, c_tile, spans=None, heads_in_buf=None):
    heads_in_buf = REC_GROUP if heads_in_buf is None else heads_in_buf
    part = heads_in_buf * REC_HEAD_DIM
    row = lax.broadcasted_iota(jnp.int32, (CHUNK, CHUNK), 0)
    col = lax.broadcasted_iota(jnp.int32, (CHUNK, CHUNK), 1)
    causal = col <= row
    r0 = ci * CHUNK if isinstance(ci, int) else pl.multiple_of(ci * CHUNK, CHUNK)
    rows = pl.ds(r0, CHUNK)
    rpos = c_tile + r0 + lax.broadcasted_iota(jnp.int32, (CHUNK, 1), 0)
    rvalid = rpos >= 0
    if isinstance(hh, int):
        cs = hh * REC_HEAD_DIM
        zc = (hh % heads_in_buf) * REC_HEAD_DIM
    else:
        cs = pl.multiple_of(hh * REC_HEAD_DIM, REC_HEAD_DIM)
        zc = 0
    head_cols = pl.ds(cs, REC_HEAD_DIM)
    lb = lb_ref[:, head_cols]
    oml = oml_ref[:, head_cols]
    logf, kr = _gate_features(zr_ref[rows, part + zc:part + zc + REC_HEAD_DIM], lb, oml)
    q = _silu(zr_ref[rows, zc:zc + REC_HEAD_DIM])
    v = jnp.where(rvalid, zr_ref[rows, 2 * part + zc:2 * part + zc + REC_HEAD_DIM], 0.0)
    vb = v.astype(BF16)
    a = _cumsum_rows(logf)
    a_mid = a[HALF - 1:HALF, :]
    a_end = a[CHUNK - 1:CHUNK, :]
    if spans is None:
        scores = _decay_scores_direct(q, kr, a)
    else:
        qh = (q * jnp.exp(a - a_mid)).astype(BF16)
        kh_ = (kr * jnp.exp(a_mid - a)).astype(BF16)
        scores = _dot_nt(qh, kh_)
        spans.append(jnp.maximum(a[0:1, :] - a_mid, a_mid - a_end))
    qt = (q * jnp.exp(a)).astype(BF16)
    kt = (kr * jnp.exp(a_end - a)).astype(BF16)
    st = st_ref[hh]
    o_state = _dot_nt(qt, st.astype(BF16))
    yield
    att = jnp.where(causal, scores, 0.0)
    o = _dot(att.astype(BF16), vb) + o_state
    st_ref[hh] = st * jnp.exp(a_end) + _dot_tn(vb, kt)
    yield
    on = o * lax.rsqrt(jnp.mean(o * o, axis=-1, keepdims=True) + EPS)
    gr = zr_ref[rows, 3 * part + zc:3 * part + zc + REC_HEAD_DIM]
    yr = (on * recw_ref[:, head_cols]) * _silu(gr)
    y_ref[rows, pl.ds(D_ATTN + cs, REC_HEAD_DIM)] = yr.astype(BF16)


def _emit_staggered(units, fillers=(), lag=1):
    n = len(units)
    rounds = n + 2 * lag
    at_round = [[] for _ in range(rounds)]
    for k, f in enumerate(fillers):
        at_round[k * rounds // len(fillers)].append(f)
    for i in range(rounds):
        if i < n:
            next(units[i])
        for f in at_round[i]:
            f()
        if 0 <= i - lag < n:
            next(units[i - lag])
        if 0 <= i - 2 * lag < n:
            next(units[i - 2 * lag], None)


def _main_tile(s):
    t = jnp.maximum(s - 1, 0)
    return lax.div(t, ROW_TILES), lax.rem(t, ROW_TILES)


def _prompt_layer_kernel(layer, final, sink_ref, meta_ref, hin_ref, win_ref, wout_ref, normw_ref, lb_ref, oml_ref,
                         recw_ref, bias_ref, finw_ref, *refs):
    if final:
        hout_ref, pk_ref, pv_ref, ps_ref = refs[:4]
        refs = refs[4:]
    else:
        meta_out_ref, hout_ref, pk_ref, pv_ref, ps_ref = refs[:5]
        refs = refs[5:]
    xn_ref, za_ref, zr_ref, kv_ref, st_ref, y_ref, kv_meta_ref, st_meta_ref, st_start_ref, zhead_ref = refs
    s = pl.program_id(0)
    is_meta = s == 0
    _, j = _main_tile(s)

    @pl.when(is_meta)
    def _():
        kv_ref[0:QBLK, :] = jnp.zeros((QBLK, 2 * KV_DIM), BF16)
        st_ref[...] = jnp.zeros(st_ref.shape, F32)

    @pl.when(jnp.logical_and(jnp.logical_not(is_meta), j == 0))
    def _():
        kv_ref[0:QBLK, :] = kv_meta_ref[...]
        st_ref[...] = st_meta_ref[...]

    st_start_ref[...] = st_ref[...]

    def tile_input(cols=slice(None)):
        meta = meta_ref[:, cols]
        meta_rows = jnp.concatenate([jnp.zeros((PAD, meta.shape[1]), F32), meta], axis=0)
        return jnp.where(is_meta, meta_rows, hin_ref[0, :, cols])

    c_tile = jnp.where(is_meta, -PAD, j * TILE + N_META)
    xn_ref[...] = _rmsnorm_rows(tile_input(), normw_ref[...]).astype(BF16)

    def project(dst_ref, dst_col, src_col):
        dst_ref[:, pl.ds(dst_col, PROJ_COLS)] = _dot(xn_ref[...], win_ref[:, pl.ds(src_col, PROJ_COLS)])

    def project_rec(dst_ref, group):
        return [functools.partial(project, dst_ref, part * PROJ_COLS, off + group * PROJ_COLS)
                for part, off in enumerate((OFF_QR, OFF_FR, OFF_IR, OFF_GR))]

    def out_proj(k0, k1, c0, first_half):
        cols = pl.ds(c0, PROJ_COLS)
        base = tile_input(cols) if first_half else hout_ref[0, :, cols]
        hout_ref[0, :, cols] = base + _dot(y_ref[:, k0:k1], wout_ref[k0:k1, cols])

    def column_blocks(n, body):
        def step(i, carry):
            body(pl.multiple_of(i * PROJ_COLS, PROJ_COLS))
            return carry
        lax.fori_loop(0, n, step, 0)

    for c0 in range(0, ATT_COLS, PROJ_COLS):
        project(za_ref, c0, c0)
    kv_ref[QBLK:QBLK + TILE, :] = za_ref[:, OFF_KA:OFF_KA + 2 * KV_DIM].astype(BF16)
    pk_ref[0] = za_ref[TILE - WINDOW:TILE, OFF_KA:OFF_KA + KV_DIM]
    pv_ref[0] = za_ref[TILE - WINDOW:TILE, OFF_VA:OFF_VA + KV_DIM]

    att_units = [_attention_unit(layer, qb, kh, za_ref, kv_ref, y_ref, bias_ref, sink_ref, c_tile)
                 for qb in range(TILE // QBLK) for kh in range(N_KV)]
    _emit_staggered(att_units, project_rec(zr_ref, 0))
    kv_ref[0:QBLK, :] = kv_ref[TILE:TILE + QBLK, :]

    n_groups = N_REC_HEADS // REC_GROUP
    z_bufs = [zr_ref, za_ref]
    spans = []

    def rec_units(grp, state_ref, spans_):
        return [_hgrn2_unit(ci, grp * REC_GROUP + hl, z_bufs[grp % 2], state_ref, y_ref, lb_ref, oml_ref,
                            recw_ref, c_tile, spans_)
                for ci in range(TILE // CHUNK) for hl in range(REC_GROUP)]

    for grp in range(n_groups):
        if grp + 1 < n_groups:
            fillers = project_rec(z_bufs[(grp + 1) % 2], grp + 1)
        else:
            fillers = [functools.partial(out_proj, 0, D_ATTN, c0, True) for c0 in range(0, D_MODEL, PROJ_COLS)]
        _emit_staggered(rec_units(grp, st_ref, spans), fillers)
    widest = jnp.max(jnp.concatenate(spans, axis=0))
    for c0 in range(0, D_MODEL, PROJ_COLS):
        out_proj(D_ATTN, D_MODEL, c0, False)

    @pl.when(widest > MAX_FACTOR_EXPONENT)
    def _():
        def replay_chunk(hh, ci, carry):
            for _ in _hgrn2_unit(ci, hh, zhead_ref, st_start_ref, y_ref, lb_ref, oml_ref, recw_ref, c_tile,
                                 spans=None, heads_in_buf=1):
                pass
            return carry

        def replay_head(hh, carry):
            for part, off in enumerate((OFF_QR, OFF_FR, OFF_IR, OFF_GR)):
                src = pl.multiple_of(off + hh * REC_HEAD_DIM, REC_HEAD_DIM)
                zhead_ref[:, part * REC_HEAD_DIM:(part + 1) * REC_HEAD_DIM] = _dot(
                    xn_ref[...], win_ref[:, pl.ds(src, REC_HEAD_DIM)])
            return lax.fori_loop(0, TILE // CHUNK, functools.partial(replay_chunk, hh), carry)

        lax.fori_loop(0, N_REC_HEADS, replay_head, 0)
        column_blocks(D_MODEL // PROJ_COLS, lambda c0: out_proj(0, D_MODEL, c0, True))

    ps_ref[0] = st_ref[...]

    @pl.when(is_meta)
    def _():
        kv_meta_ref[...] = kv_ref[0:QBLK, :]
        st_meta_ref[...] = st_ref[...]
        if not final:
            meta_out_ref[...] = hout_ref[0, PAD:TILE, :]

    if final:
        @pl.when(jnp.logical_not(is_meta))
        def _():
            hout_ref[0] = _rmsnorm_rows(hout_ref[0], finw_ref[...])


def _layer_block(shape, layer):
    nd = len(shape)
    return pl.BlockSpec((None,) + tuple(shape), lambda s: (layer,) + (0,) * nd, pipeline_mode=pl.Buffered(1))


def _const_block(shape):
    nd = len(shape)
    return pl.BlockSpec(tuple(shape), lambda s: (0,) * nd, pipeline_mode=pl.Buffered(1))


def _prompt_layer(layer, h_meta, h_main, w_in, w_out, norm_w, lb, oml, rec_w, sinks, bias, fin_w):
    final = layer == DEPTH - 1
    main = pl.BlockSpec((1, TILE, D_MODEL), lambda s: _main_tile(s) + (0,))
    meta = pl.BlockSpec((N_META, D_MODEL), lambda s: (0, 0))
    kv_spec = pl.BlockSpec((1, WINDOW, KV_DIM), lambda s: (_main_tile(s)[0], 0, 0))
    st_spec = pl.BlockSpec((1, N_REC_HEADS, REC_HEAD_DIM, REC_HEAD_DIM), lambda s: (_main_tile(s)[0], 0, 0, 0))
    main_shape = jax.ShapeDtypeStruct((BATCH, SEQ, D_MODEL), F32)
    meta_shape = jax.ShapeDtypeStruct((N_META, D_MODEL), F32)
    return pl.pallas_call(
        functools.partial(_prompt_layer_kernel, layer, final),
        grid=(1 + BATCH * ROW_TILES,),
        in_specs=[
            pl.BlockSpec(memory_space=pltpu.SMEM),
            _const_block((N_META, D_MODEL)),
            main,
            _const_block((D_MODEL, D_IN)),
            _const_block((D_MODEL, D_MODEL)),
            _layer_block((1, D_MODEL), layer),
            _layer_block((1, D_REC), layer),
            _layer_block((1, D_REC), layer),
            _layer_block((1, D_REC), layer),
            _const_block((N_HEADS * QBLK, 2 * QBLK)),
            _const_block((1, D_MODEL)),
        ],
        out_specs=([] if final else [meta]) + [main, kv_spec, kv_spec, st_spec],
        out_shape=([] if final else [meta_shape]) + [
            main_shape,
            jax.ShapeDtypeStruct((BATCH, WINDOW, KV_DIM), F32),
            jax.ShapeDtypeStruct((BATCH, WINDOW, KV_DIM), F32),
            jax.ShapeDtypeStruct((BATCH, N_REC_HEADS, REC_HEAD_DIM, REC_HEAD_DIM), F32),
        ],
        scratch_shapes=[
            pltpu.VMEM((TILE, D_MODEL), BF16),
            pltpu.VMEM((TILE, ATT_COLS), F32),
            pltpu.VMEM((TILE, 4 * PROJ_COLS), F32),
            pltpu.VMEM((QBLK + TILE, 2 * KV_DIM), BF16),
            pltpu.VMEM((N_REC_HEADS, REC_HEAD_DIM, REC_HEAD_DIM), F32),
            pltpu.VMEM((TILE, D_MODEL), BF16),
            pltpu.VMEM((QBLK, 2 * KV_DIM), BF16),
            pltpu.VMEM((N_REC_HEADS, REC_HEAD_DIM, REC_HEAD_DIM), F32),
            pltpu.VMEM((N_REC_HEADS, REC_HEAD_DIM, REC_HEAD_DIM), F32),
            pltpu.VMEM((TILE, 4 * REC_HEAD_DIM), F32),
        ],
        compiler_params=pltpu.CompilerParams(
            dimension_semantics=("arbitrary",), vmem_limit_bytes=VMEM_LIMIT),
        name="prompt_layer",
    )(sinks, h_meta, h_main, w_in, w_out, norm_w, lb, oml, rec_w, bias, fin_w)


DEC_PROJ_COLS = D_IN // 4
DEC_OUT_COLS = D_MODEL // 2


def _dec_proj_kernel(hs_ref, normw_ref, win_ref, z_ref, wb_ref):
    xn = _rmsnorm_rows(hs_ref[...], normw_ref[...]).astype(BF16)
    for c0 in range(0, DEC_PROJ_COLS, V7X_LANES):
        cols = slice(c0, c0 + V7X_LANES)
        wb = win_ref[:, cols].astype(BF16)
        wb_ref[:, cols] = wb
        z_ref[:, cols] = _dot(xn, wb)


def _dec_proj(layer, hs, norm_w, w_in):
    return pl.pallas_call(
        _dec_proj_kernel,
        grid=(D_IN // DEC_PROJ_COLS,),
        in_specs=[
            pl.BlockSpec((N_DEC, D_MODEL), lambda n: (0, 0)),
            pl.BlockSpec((None, 1, D_MODEL), lambda n: (layer, 0, 0)),
            pl.BlockSpec((None, D_MODEL, DEC_PROJ_COLS), lambda n: (layer, 0, n)),
        ],
        out_specs=[pl.BlockSpec((N_DEC, DEC_PROJ_COLS), lambda n: (0, n)),
                   pl.BlockSpec((D_MODEL, DEC_PROJ_COLS), lambda n: (0, n))],
        out_shape=[jax.ShapeDtypeStruct((N_DEC, D_IN), F32),
                   jax.ShapeDtypeStruct((D_MODEL, D_IN), BF16)],
        compiler_params=pltpu.CompilerParams(dimension_semantics=("arbitrary",), vmem_limit_bytes=DEC_VMEM_LIMIT),
        name="dec_proj",
    )(hs, norm_w, w_in)


def _dec_out_kernel(hs_ref, y_ref, wout_ref, o_ref, wb_ref):
    for c0 in range(0, DEC_OUT_COLS, 2 * V7X_LANES):
        cols = slice(c0, c0 + 2 * V7X_LANES)
        wb = wout_ref[:, cols].astype(BF16)
        wb_ref[:, cols] = wb
        o_ref[:, cols] = hs_ref[:, cols] + _dot(y_ref[...], wb)


def _dec_out(layer, hs, y, w_out):
    return pl.pallas_call(
        _dec_out_kernel,
        grid=(D_MODEL // DEC_OUT_COLS,),
        in_specs=[
            pl.BlockSpec((N_DEC, DEC_OUT_COLS), lambda n: (0, n)),
            pl.BlockSpec((N_DEC, D_MODEL), lambda n: (0, 0)),
            pl.BlockSpec((None, D_MODEL, DEC_OUT_COLS), lambda n: (layer, 0, n)),
        ],
        out_specs=[pl.BlockSpec((N_DEC, DEC_OUT_COLS), lambda n: (0, n)),
                   pl.BlockSpec((D_MODEL, DEC_OUT_COLS), lambda n: (0, n))],
        out_shape=[jax.ShapeDtypeStruct((N_DEC, D_MODEL), F32),
                   jax.ShapeDtypeStruct((D_MODEL, D_MODEL), BF16)],
        compiler_params=pltpu.CompilerParams(dimension_semantics=("arbitrary",), vmem_limit_bytes=DEC_VMEM_LIMIT),
        name="dec_out",
    )(hs, y, w_out)


DEC_SEQS_PER_STEP = 4
DEC_LAG = DEC_SEQS_PER_STEP * (N_KV + N_REC_HEADS)


def _dec_attention_unit(layer, q, kh, sink_ref, z_ref, ck_ref, cv_ref, biasc_ref, biasn_ref, y_ref):
    t = DEC_SEQ
    tok = slice(q * t, (q + 1) * t)
    hs_ = slice(kh * HEAD_DIM, (kh + 1) * HEAD_DIM)
    kc = ck_ref[q, :, hs_].astype(BF16)
    kn = z_ref[tok, OFF_KA + kh * HEAD_DIM:OFF_KA + (kh + 1) * HEAD_DIM]
    q4 = jnp.concatenate(
        [z_ref[tok, (kh * GROUP + g) * HEAD_DIM:(kh * GROUP + g + 1) * HEAD_DIM] for g in range(GROUP)],
        axis=0) * (HEAD_DIM ** -0.5)
    rows = slice(kh * GROUP * t, (kh + 1) * GROUP * t)
    sc = _dot_nt(q4.astype(BF16), kc)
    sn = _dot_nt(q4, kn)
    yield
    sc = sc + biasc_ref[rows, :]
    sn = sn + biasn_ref[rows, :]
    ecs, ens, dens = [], [], []
    for g in range(GROUP):
        sink = sink_ref[layer, kh * GROUP + g]
        scg = sc[g * t:(g + 1) * t]
        sng = sn[g * t:(g + 1) * t]
        m = jnp.maximum(jnp.maximum(jnp.max(scg, axis=-1, keepdims=True),
                                    jnp.max(sng, axis=-1, keepdims=True)), sink)
        ec = jnp.exp(scg - m)
        en = jnp.exp(sng - m)
        dens.append(jnp.sum(ec, axis=-1, keepdims=True) + jnp.sum(en, axis=-1, keepdims=True)
                    + jnp.exp(sink - m))
        ecs.append(ec)
        ens.append(en)
    vc = cv_ref[q, :, hs_]
    vn = z_ref[tok, OFF_VA + kh * HEAD_DIM:OFF_VA + (kh + 1) * HEAD_DIM]
    o4 = (_dot(jnp.concatenate(ecs, axis=0).astype(BF16), vc.astype(BF16))
          + _dot(jnp.concatenate(ens, axis=0), vn))
    yield
    outs = []
    for g in range(GROUP):
        h = kh * GROUP + g
        og = o4[g * t:(g + 1) * t] / dens[g]
        ga = z_ref[tok, OFF_GA + h * HEAD_DIM:OFF_GA + (h + 1) * HEAD_DIM]
        outs.append(og * _silu(ga))
    y_ref[tok, kh * GROUP * HEAD_DIM:(kh + 1) * GROUP * HEAD_DIM] = jnp.concatenate(outs, axis=-1).astype(BF16)


def _dec_hgrn2_unit(q, hh, z_ref, s0_ref, lb_ref, oml_ref, recw_ref, y_ref, ns_ref):
    t = DEC_SEQ
    tok = slice(q * t, (q + 1) * t)
    row = lax.broadcasted_iota(jnp.int32, (t, t), 0)
    col = lax.broadcasted_iota(jnp.int32, (t, t), 1)
    causal = col <= row
    cs = hh * REC_HEAD_DIM
    lb = lb_ref[:, cs:cs + REC_HEAD_DIM]
    oml = oml_ref[:, cs:cs + REC_HEAD_DIM]
    logf, kr = _gate_features(z_ref[tok, OFF_FR + cs:OFF_FR + cs + REC_HEAD_DIM], lb, oml)
    qv = _silu(z_ref[tok, OFF_QR + cs:OFF_QR + cs + REC_HEAD_DIM])
    v = z_ref[tok, OFF_IR + cs:OFF_IR + cs + REC_HEAD_DIM]
    a = _cumsum_rows(logf)
    a_end = a[t - 1:t, :]
    scores = _decay_scores_direct(qv, kr, a)
    qt = qv * jnp.exp(a)
    kt = kr * jnp.exp(a_end - a)
    s0 = s0_ref[q, hh]
    o_state = _dot(qt, s0)
    yield
    att = jnp.where(causal, scores, 0.0)
    o = _dot(att, v) + o_state
    ktd = jnp.concatenate([kt, jnp.exp(a_end), jnp.zeros((t - 1, REC_HEAD_DIM), F32)], axis=0).T
    ns_ref[q, hh] = s0 * ktd[:, t:t + 1] + _dot(ktd[:, 0:t], v)
    yield
    on = o * lax.rsqrt(jnp.mean(o * o, axis=-1, keepdims=True) + EPS)
    gr = z_ref[tok, OFF_GR + cs:OFF_GR + cs + REC_HEAD_DIM]
    yr = (on * recw_ref[:, cs:cs + REC_HEAD_DIM]) * _silu(gr)
    y_ref[tok, D_ATTN + cs:D_ATTN + cs + REC_HEAD_DIM] = yr.astype(BF16)


def _dec_mix_kernel(layer, sink_ref, z_ref, ck_ref, cv_ref, s0_ref, lb_ref, oml_ref, recw_ref, biasc_ref,
                    biasn_ref, out_k_hbm, out_v_hbm, out_s_hbm, y_ref, nk_ref, nv_ref, ns_ref):
    del out_k_hbm, out_v_hbm, out_s_hbm
    t = DEC_SEQ
    for q in range(DEC_SEQS_PER_STEP):
        tok = slice(q * t, (q + 1) * t)
        nk_ref[q, 0:WINDOW - t, :] = ck_ref[q, t:WINDOW, :]
        nk_ref[q, WINDOW - t:WINDOW, :] = z_ref[tok, OFF_KA:OFF_KA + KV_DIM]
        nv_ref[q, 0:WINDOW - t, :] = cv_ref[q, t:WINDOW, :]
        nv_ref[q, WINDOW - t:WINDOW, :] = z_ref[tok, OFF_VA:OFF_VA + KV_DIM]
    units = []
    for kh in range(N_KV):
        units += [_dec_attention_unit(layer, q, kh, sink_ref, z_ref, ck_ref, cv_ref, biasc_ref, biasn_ref, y_ref)
                  for q in range(DEC_SEQS_PER_STEP)]
    for hh in range(N_REC_HEADS):
        units += [_dec_hgrn2_unit(q, hh, z_ref, s0_ref, lb_ref, oml_ref, recw_ref, y_ref, ns_ref)
                  for q in range(DEC_SEQS_PER_STEP)]
    _emit_staggered(units, lag=DEC_LAG)


def _dec_mix(layer, z, ck, cv, s0, lb, oml, rec_w, sinks, bias_c, bias_n, out_k, out_v, out_s):
    t = DEC_SEQ
    nq = DEC_SEQS_PER_STEP
    const2 = lambda s: (0, 0)
    per_layer = lambda s: (layer, 0, 0)
    kv_in = pl.BlockSpec((None, nq, WINDOW, KV_DIM), lambda s: (layer, s, 0, 0))
    st_in = pl.BlockSpec((None, nq, N_REC_HEADS, REC_HEAD_DIM, REC_HEAD_DIM), lambda s: (layer, s, 0, 0, 0))
    return pl.pallas_call(
        functools.partial(_dec_mix_kernel, layer),
        grid=(DEC_BATCH // nq,),
        in_specs=[
            pl.BlockSpec(memory_space=pltpu.SMEM),
            pl.BlockSpec((nq * t, D_IN), lambda s: (s, 0)),
            kv_in,
            kv_in,
            st_in,
            pl.BlockSpec((None, 1, D_REC), per_layer),
            pl.BlockSpec((None, 1, D_REC), per_layer),
            pl.BlockSpec((None, 1, D_REC), per_layer),
            pl.BlockSpec((N_HEADS * t, WINDOW), const2),
            pl.BlockSpec((N_HEADS * t, t), const2),
            pl.BlockSpec(memory_space=pl.ANY),
            pl.BlockSpec(memory_space=pl.ANY),
            pl.BlockSpec(memory_space=pl.ANY),
        ],
        out_specs=[
            pl.BlockSpec((nq * t, D_MODEL), lambda s: (s, 0)),
            kv_in,
            kv_in,
            st_in,
        ],
        out_shape=[
            jax.ShapeDtypeStruct((N_DEC, D_MODEL), BF16),
            jax.ShapeDtypeStruct(out_k.shape, F32),
            jax.ShapeDtypeStruct(out_v.shape, F32),
            jax.ShapeDtypeStruct(out_s.shape, F32),
        ],
        input_output_aliases={10: 1, 11: 2, 12: 3},
        compiler_params=pltpu.CompilerParams(dimension_semantics=("arbitrary",)),
        name="dec_mix",
    )(sinks, z, ck, cv, s0, lb, oml, rec_w, bias_c, bias_n, out_k, out_v, out_s)


def _final_norm_kernel(x_ref, w_ref, o_ref):
    o_ref[...] = _rmsnorm_rows(x_ref[...], w_ref[...])


def _final_norm_dec(hs, w):
    return pl.pallas_call(
        _final_norm_kernel,
        out_shape=jax.ShapeDtypeStruct((N_DEC, D_MODEL), F32),
        name="final_norm_dec",
    )(hs, w)


def _t5_bucket(dist):
    max_exact = N_BUCKETS // 2
    d = jnp.maximum(dist, 0)
    df = jnp.maximum(d, 1).astype(F32)
    large = max_exact + (jnp.log(df / max_exact) / math.log(MAX_DISTANCE / max_exact)
                         * (N_BUCKETS - max_exact)).astype(jnp.int32)
    large = jnp.minimum(large, N_BUCKETS - 1)
    return jnp.where(d < max_exact, d, large)


def _bias_tables(rel_bias_table):
    bias_d = rel_bias_table[_t5_bucket(jnp.arange(WINDOW))].T.astype(F32)
    period = 3 * WINDOW + 1
    u = jnp.concatenate([jnp.full((N_HEADS, 1), NEG, F32), bias_d[:, ::-1],
                         jnp.full((N_HEADS, period - 1 - WINDOW), NEG, F32)], axis=1)
    rows = jnp.tile(u, (1, WINDOW))[:, :WINDOW * (period - 1)].reshape(N_HEADS, WINDOW, period - 1)
    prompt = rows[:, :, :2 * WINDOW].reshape(N_HEADS * WINDOW, 2 * WINDOW).astype(BF16)
    dec = rows[:, :DEC_SEQ, :WINDOW + DEC_SEQ].reshape(N_HEADS * DEC_SEQ, WINDOW + DEC_SEQ)
    return prompt, dec[:, :WINDOW], dec[:, WINDOW:]


def kernel(x_prompt, x_sample, cache_k, cache_v, state_h, meta_tokens, w_in, w_out, norm_w, final_norm_w,
           attn_sinks, rel_bias_table, hgrn_lb_logits, hgrn_norm_w):
    pl_ = jax.nn.softmax(hgrn_lb_logits.astype(F32), axis=0)
    lb = (jnp.cumsum(pl_, axis=0) - pl_[0:1]).reshape(DEPTH, 1, D_REC)
    oml = 1.0 - lb
    bias_p, bias_dc, bias_dn = _bias_tables(rel_bias_table)
    nw = norm_w.astype(F32).reshape(DEPTH, 1, D_MODEL)
    rw = hgrn_norm_w.astype(F32).reshape(DEPTH, 1, D_REC)
    fw = final_norm_w.astype(F32).reshape(1, D_MODEL)
    sinks = attn_sinks.astype(F32)

    h_meta = meta_tokens.astype(F32)
    h_main = x_prompt
    hs = x_sample.reshape(N_DEC, D_MODEL)
    ck = cache_k.reshape(DEPTH, DEC_BATCH, WINDOW, KV_DIM)
    cv = cache_v.reshape(DEPTH, DEC_BATCH, WINDOW, KV_DIM)

    pk, pv, ps = [], [], []
    sk = lax.empty(ck.shape, F32)
    sv = lax.empty(cv.shape, F32)
    ss = lax.empty(state_h.shape, F32)
    for l in range(DEPTH):
        z, w_in_b = _dec_proj(l, hs, nw, w_in)
        y, sk, sv, ss = _dec_mix(l, z, ck, cv, state_h, lb, oml, rw, sinks, bias_dc, bias_dn, sk, sv, ss)
        hs, w_out_b = _dec_out(l, hs, y, w_out)

        outs = _prompt_layer(l, h_meta, h_main, w_in_b, w_out_b, nw, lb, oml, rw, sinks, bias_p, fw)
        if l < DEPTH - 1:
            h_meta, h_main, k_l, v_l, s_l = outs
        else:
            h_main, k_l, v_l, s_l = outs
        pk.append(k_l)
        pv.append(v_l)
        ps.append(s_l)

    y_prompt = h_main
    y_sample = _final_norm_dec(hs, fw).reshape(DEC_BATCH, DEC_SEQ, D_MODEL)
    kv_shape = (DEPTH, -1, WINDOW, N_KV, HEAD_DIM)
    return (y_prompt, y_sample,
            jnp.stack(pk).reshape(kv_shape), jnp.stack(pv).reshape(kv_shape),
            jnp.swapaxes(jnp.stack(ps), -1, -2),
            sk.reshape(kv_shape), sv.reshape(kv_shape), ss)
```

```python
import functools
import math

import jax
import jax.numpy as jnp
from jax import lax
from jax.experimental import pallas as pl
from jax.experimental.pallas import tpu as pltpu

D_MODEL = 2048
BATCH = 4
SEQ = 2048
DEPTH = 4
DEC_BATCH = 32
DEC_SEQ = 8
N_META = 16
D_ATTN = 1024
D_REC = 1024
HEAD_DIM = 64
N_HEADS = 16
N_KV = 4
GROUP = 4
KV_DIM = 256
WINDOW = 128
N_BUCKETS = 32
MAX_DISTANCE = 128
REC_HEAD_DIM = 128
N_REC_HEADS = 8
EPS = 1e-6
D_IN = 2 * D_ATTN + 2 * KV_DIM + 4 * D_REC

OFF_KA = D_ATTN
OFF_VA = OFF_KA + KV_DIM
OFF_GA = OFF_VA + KV_DIM
OFF_QR = OFF_GA + D_ATTN
OFF_FR = OFF_QR + D_REC
OFF_IR = OFF_FR + D_REC
OFF_GR = OFF_IR + D_REC
ATT_COLS = OFF_QR

TILE = 256
QBLK = WINDOW
CHUNK = 128
HALF = CHUNK // 2
ROW_TILES = SEQ // TILE
PAD = TILE - N_META
PROJ_COLS = 512
REC_GROUP = PROJ_COLS // REC_HEAD_DIM
NEG = -1e30
MAX_FACTOR_EXPONENT = 80.0
N_DEC = DEC_BATCH * DEC_SEQ
V7X_LANES = 128
V7X_VMEM_BYTES = 64 * 1024 * 1024
MIB = 1024 * 1024
VMEM_LIMIT = V7X_VMEM_BYTES - MIB
DEC_VMEM_LIMIT = V7X_VMEM_BYTES - 8 * MIB

F32 = jnp.float32
BF16 = jnp.bfloat16


def _sigmoid_pair(x):
    e = jnp.exp(-jnp.abs(x))
    inv = 1.0 / (1.0 + e)
    pos = x >= 0
    return jnp.where(pos, inv, e * inv), jnp.where(pos, e * inv, inv)


def _silu(x):
    return x * (0.5 * jnp.tanh(0.5 * x) + 0.5)


def _cumsum_rows(x):
    n = x.shape[0]
    row = lax.broadcasted_iota(jnp.int32, x.shape, 0)
    k = 1
    while k < n:
        x = x + jnp.where(row >= k, pltpu.roll(x, k, axis=0), 0.0)
        k *= 2
    return x


def _rmsnorm_rows(x, w):
    ms = jnp.mean(x * x, axis=-1, keepdims=True)
    return (x * lax.rsqrt(ms + EPS)) * w


def _dot_nt(a, b):
    return lax.dot_general(a, b, (((1,), (1,)), ((), ())), preferred_element_type=F32)


def _dot_tn(a, b):
    return lax.dot_general(a, b, (((0,), (0,)), ((), ())), preferred_element_type=F32)


def _dot(a, b):
    return jnp.dot(a, b, preferred_element_type=F32)


def _gate_features(fr, lb, oml):
    sig, sneg = _sigmoid_pair(fr)
    logf = jnp.log(lb + oml * sig)
    return logf, oml * sneg


def _attention_unit(layer, qb, kh, za_ref, kv_ref, y_ref, bias_ref, sink_ref, c_tile):
    r0 = qb * QBLK
    kpos = c_tile + r0 - QBLK + lax.broadcasted_iota(jnp.int32, (1, 2 * QBLK), 1)
    kvalid = kpos >= 0
    kk = kv_ref[r0:r0 + 2 * QBLK, kh * HEAD_DIM:(kh + 1) * HEAD_DIM]
    q4 = jnp.concatenate(
        [za_ref[r0:r0 + QBLK, (kh * GROUP + g) * HEAD_DIM:(kh * GROUP + g + 1) * HEAD_DIM]
         for g in range(GROUP)], axis=0)
    s = _dot_nt((q4 * (HEAD_DIM ** -0.5)).astype(BF16), kk)
    yield
    s = s + bias_ref[kh * GROUP * QBLK:(kh + 1) * GROUP * QBLK, :].astype(F32)
    s = jnp.where(kvalid, s, NEG)
    es, dens = [], []
    for g in range(GROUP):
        sink = sink_ref[layer, kh * GROUP + g]
        sg = s[g * QBLK:(g + 1) * QBLK]
        m = jnp.maximum(jnp.max(sg, axis=-1, keepdims=True), sink)
        e = jnp.exp(sg - m)
        dens.append(jnp.sum(e, axis=-1, keepdims=True) + jnp.exp(sink - m))
        es.append(e.astype(BF16))
    vv = kv_ref[r0:r0 + 2 * QBLK, KV_DIM + kh * HEAD_DIM:KV_DIM + (kh + 1) * HEAD_DIM]
    o4 = _dot(jnp.concatenate(es, axis=0), vv)
    yield
    outs = []
    for g in range(GROUP):
        h = kh * GROUP + g
        og = o4[g * QBLK:(g + 1) * QBLK] / dens[g]
        ga = za_ref[r0:r0 + QBLK, OFF_GA + h * HEAD_DIM:OFF_GA + (h + 1) * HEAD_DIM]
        outs.append(og * _silu(ga))
    y_ref[r0:r0 + QBLK, kh * GROUP * HEAD_DIM:(kh + 1) * GROUP * HEAD_DIM] = (
        jnp.concatenate(outs, axis=-1).astype(BF16))


def _decay_scores_direct(q, kr, a):
    n = q.shape[0]
    row = lax.broadcasted_iota(jnp.int32, (n, n), 0)
    col = lax.broadcasted_iota(jnp.int32, (n, n), 1)

    def diagonal(d, acc):
        ks = pltpu.roll(kr, d, axis=0)
        a_s = pltpu.roll(a, d, axis=0)
        p = q * ks * jnp.exp(jnp.minimum(a - a_s, 0.0))
        return jnp.where(col == row - d, jnp.sum(p, axis=-1, keepdims=True), acc)

    acc = jnp.zeros((n, n), F32)
    if n <= DEC_SEQ:
        for d in range(n):
            acc = diagonal(d, acc)
        return acc
    return lax.fori_loop(0, n, diagonal, acc)


def _hgrn2_unit(ci, hh, zr_ref, st_ref, y_ref, lb_ref, oml_ref, recw_ref, c_tile, spans=None, heads_in_buf=None):
    heads_in_buf = REC_GROUP if heads_in_buf is None else heads_in_buf
    part = heads_in_buf * REC_HEAD_DIM
    row = lax.broadcasted_iota(jnp.int32, (CHUNK, CHUNK), 0)
    col = lax.broadcasted_iota(jnp.int32, (CHUNK, CHUNK), 1)
    causal = col <= row
    r0 = ci * CHUNK if isinstance(ci, int) else pl.multiple_of(ci * CHUNK, CHUNK)
    rows = pl.ds(r0, CHUNK)
    rpos = c_tile + r0 + lax.broadcasted_iota(jnp.int32, (CHUNK, 1), 0)
    rvalid = rpos >= 0
    if isinstance(hh, int):
        cs = hh * REC_HEAD_DIM
        zc = (hh % heads_in_buf) * REC_HEAD_DIM
    else:
        cs = pl.multiple_of(hh * REC_HEAD_DIM, REC_HEAD_DIM)
        zc = 0
    head_cols = pl.ds(cs, REC_HEAD_DIM)
    lb = lb_ref[:, head_cols]
    oml = oml_ref[:, head_cols]
    logf, kr = _gate_features(zr_ref[rows, part + zc:part + zc + REC_HEAD_DIM], lb, oml)
    q = _silu(zr_ref[rows, zc:zc + REC_HEAD_DIM])
    v = jnp.where(rvalid, zr_ref[rows, 2 * part + zc:2 * part + zc + REC_HEAD_DIM], 0.0)
    vb = v.astype(BF16)
    a = _cumsum_rows(logf)
    a_mid = a[HALF - 1:HALF, :]
    a_end = a[CHUNK - 1:CHUNK, :]
    if spans is None:
        scores = _decay_scores_direct(q, kr, a)
    else:
        qh = (q * jnp.exp(a - a_mid)).astype(BF16)
        kh_ = (kr * jnp.exp(a_mid - a)).astype(BF16)
        scores = _dot_nt(qh, kh_)
        spans.append(jnp.maximum(a[0:1, :] - a_mid, a_mid - a_end))
    qt = (q * jnp.exp(a)).astype(BF16)
    kt = (kr * jnp.exp(a_end - a)).astype(BF16)
    st = st_ref[hh]
    o_state = _dot_nt(qt, st.astype(BF16))
    yield
    att = jnp.where(causal, scores, 0.0)
    o = _dot(att.astype(BF16), vb) + o_state
    st_ref[hh] = st * jnp.exp(a_end) + _dot_tn(vb, kt)
    yield
    on = o * lax.rsqrt(jnp.mean(o * o, axis=-1, keepdims=True) + EPS)
    gr = zr_ref[rows, 3 * part + zc:3 * part + zc + REC_HEAD_DIM]
    yr = (on * recw_ref[:, head_cols]) * _silu(gr)
    y_ref[rows, pl.ds(D_ATTN + cs, REC_HEAD_DIM)] = yr.astype(BF16)


def _emit_staggered(units, fillers=(), lag=1):
    n = len(units)
    rounds = n + 2 * lag
    at_round = [[] for _ in range(rounds)]
    for k, f in enumerate(fillers):
        at_round[k * rounds // len(fillers)].append(f)
    for i in range(rounds):
        if i < n:
            next(units[i])
        for f in at_round[i]:
            f()
        if 0 <= i - lag < n:
            next(units[i - lag])
        if 0 <= i - 2 * lag < n:
            next(units[i - 2 * lag], None)


def _main_tile(s):
    t = jnp.maximum(s - 1, 0)
    return lax.div(t, ROW_TILES), lax.rem(t, ROW_TILES)


def _prompt_layer_kernel(layer, final, sink_ref, meta_ref, hin_ref, win_ref, wout_ref, normw_ref, lb_ref, oml_ref,
                         recw_ref, bias_ref, finw_ref, *refs):
    if final:
        hout_ref, pk_ref, pv_ref, ps_ref = refs[:4]
        refs = refs[4:]
    else:
        meta_out_ref, hout_ref, pk_ref, pv_ref, ps_ref = refs[:5]
        refs = refs[5:]
    xn_ref, za_ref, zr_ref, kv_ref, st_ref, y_ref, kv_meta_ref, st_meta_ref, st_start_ref, zhead_ref = refs
    s = pl.program_id(0)
    is_meta = s == 0
    _, j = _main_tile(s)

    @pl.when(is_meta)
    def _():
        kv_ref[0:QBLK, :] = jnp.zeros((QBLK, 2 * KV_DIM), BF16)
        st_ref[...] = jnp.zeros(st_ref.shape, F32)

    @pl.when(jnp.logical_and(jnp.logical_not(is_meta), j == 0))
    def _():
        kv_ref[0:QBLK, :] = kv_meta_ref[...]
        st_ref[...] = st_meta_ref[...]

    st_start_ref[...] = st_ref[...]

    def tile_input(cols=slice(None)):
        meta = meta_ref[:, cols]
        meta_rows = jnp.concatenate([jnp.zeros((PAD, meta.shape[1]), F32), meta], axis=0)
        return jnp.where(is_meta, meta_rows, hin_ref[0, :, cols])

    c_tile = jnp.where(is_meta, -PAD, j * TILE + N_META)
    xn_ref[...] = _rmsnorm_rows(tile_input(), normw_ref[...]).astype(BF16)

    def project(dst_ref, dst_col, src_col):
        dst_ref[:, pl.ds(dst_col, PROJ_COLS)] = _dot(xn_ref[...], win_ref[:, pl.ds(src_col, PROJ_COLS)])

    def project_rec(dst_ref, group):
        return [functools.partial(project, dst_ref, part * PROJ_COLS, off + group * PROJ_COLS)
                for part, off in enumerate((OFF_QR, OFF_FR, OFF_IR, OFF_GR))]

    def out_proj(k0, k1, c0, first_half):
        cols = pl.ds(c0, PROJ_COLS)
        base = tile_input(cols) if first_half else hout_ref[0, :, cols]
        hout_ref[0, :, cols] = base + _dot(y_ref[:, k0:k1], wout_ref[k0:k1, cols])

    def column_blocks(n, body):
        def step(i, carry):
            body(pl.multiple_of(i * PROJ_COLS, PROJ_COLS))
            return carry
        lax.fori_loop(0, n, step, 0)

    for c0 in range(0, ATT_COLS, PROJ_COLS):
        project(za_ref, c0, c0)
    kv_ref[QBLK:QBLK + TILE, :] = za_ref[:, OFF_KA:OFF_KA + 2 * KV_DIM].astype(BF16)
    pk_ref[0] = za_ref[TILE - WINDOW:TILE, OFF_KA:OFF_KA + KV_DIM]
    pv_ref[0] = za_ref[TILE - WINDOW:TILE, OFF_VA:OFF_VA + KV_DIM]

    att_units = [_attention_unit(layer, qb, kh, za_ref, kv_ref, y_ref, bias_ref, sink_ref, c_tile)
                 for qb in range(TILE // QBLK) for kh in range(N_KV)]
    _emit_staggered(att_units, project_rec(zr_ref, 0))
    kv_ref[0:QBLK, :] = kv_ref[TILE:TILE + QBLK, :]

    n_groups = N_REC_HEADS // REC_GROUP
    z_bufs = [zr_ref, za_ref]
    spans = []

    def rec_units(grp, state_ref, spans_):
        return [_hgrn2_unit(ci, grp * REC_GROUP + hl, z_bufs[grp % 2], state_ref, y_ref, lb_ref, oml_ref,
                            recw_ref, c_tile, spans_)
                for ci in range(TILE // CHUNK) for hl in range(REC_GROUP)]

    for grp in range(n_groups):
        if grp + 1 < n_groups:
            fillers = project_rec(z_bufs[(grp + 1) % 2], grp + 1)
        else:
            fillers = [functools.partial(out_proj, 0, D_ATTN, c0, True) for c0 in range(0, D_MODEL, PROJ_COLS)]
        _emit_staggered(rec_units(grp, st_ref, spans), fillers)
    widest = jnp.max(jnp.concatenate(spans, axis=0))
    for c0 in range(0, D_MODEL, PROJ_COLS):
        out_proj(D_ATTN, D_MODEL, c0, False)

    @pl.when(widest > MAX_FACTOR_EXPONENT)
    def _():
        def replay_chunk(hh, ci, carry):
            for _ in _hgrn2_unit(ci, hh, zhead_ref, st_start_ref, y_ref, lb_ref, oml_ref, recw_ref, c_tile,
                                 spans=None, heads_in_buf=1):
                pass
            return carry

        def replay_head(hh, carry):
            for part, off in enumerate((OFF_QR, OFF_FR, OFF_IR, OFF_GR)):
                src = pl.multiple_of(off + hh * REC_HEAD_DIM, REC_HEAD_DIM)
                zhead_ref[:, part * REC_HEAD_DIM:(part + 1) * REC_HEAD_DIM] = _dot(
                    xn_ref[...], win_ref[:, pl.ds(src, REC_HEAD_DIM)])
            return lax.fori_loop(0, TILE // CHUNK, functools.partial(replay_chunk, hh), carry)

        lax.fori_loop(0, N_REC_HEADS, replay_head, 0)
        column_blocks(D_MODEL // PROJ_COLS, lambda c0: out_proj(0, D_MODEL, c0, True))

    ps_ref[0] = st_ref[...]

    @pl.when(is_meta)
    def _():
        kv_meta_ref[...] = kv_ref[0:QBLK, :]
        st_meta_ref[...] = st_ref[...]
        if not final:
            meta_out_ref[...] = hout_ref[0, PAD:TILE, :]

    if final:
        @pl.when(jnp.logical_not(is_meta))
        def _():
            hout_ref[0] = _rmsnorm_rows(hout_ref[0], finw_ref[...])


def _layer_block(shape, layer):
    nd = len(shape)
    return pl.BlockSpec((None,) + tuple(shape), lambda s: (layer,) + (0,) * nd, pipeline_mode=pl.Buffered(1))


def _const_block(shape):
    nd = len(shape)
    return pl.BlockSpec(tuple(shape), lambda s: (0,) * nd, pipeline_mode=pl.Buffered(1))


def _prompt_layer(layer, h_meta, h_main, w_in, w_out, norm_w, lb, oml, rec_w, sinks, bias, fin_w):
    final = layer == DEPTH - 1
    main = pl.BlockSpec((1, TILE, D_MODEL), lambda s: _main_tile(s) + (0,))
    meta = pl.BlockSpec((N_META, D_MODEL), lambda s: (0, 0))
    kv_spec = pl.BlockSpec((1, WINDOW, KV_DIM), lambda s: (_main_tile(s)[0], 0, 0))
    st_spec = pl.BlockSpec((1, N_REC_HEADS, REC_HEAD_DIM, REC_HEAD_DIM), lambda s: (_main_tile(s)[0], 0, 0, 0))
    main_shape = jax.ShapeDtypeStruct((BATCH, SEQ, D_MODEL), F32)
    meta_shape = jax.ShapeDtypeStruct((N_META, D_MODEL), F32)
    return pl.pallas_call(
        functools.partial(_prompt_layer_kernel, layer, final),
        grid=(1 + BATCH * ROW_TILES,),
        in_specs=[
            pl.BlockSpec(memory_space=pltpu.SMEM),
            _const_block((N_META, D_MODEL)),
            main,
            _const_block((D_MODEL, D_IN)),
            _const_block((D_MODEL, D_MODEL)),
            _layer_block((1, D_MODEL), layer),
            _layer_block((1, D_REC), layer),
            _layer_block((1, D_REC), layer),
            _layer_block((1, D_REC), layer),
            _const_block((N_HEADS * QBLK, 2 * QBLK)),
            _const_block((1, D_MODEL)),
        ],
        out_specs=([] if final else [meta]) + [main, kv_spec, kv_spec, st_spec],
        out_shape=([] if final else [meta_shape]) + [
            main_shape,
            jax.ShapeDtypeStruct((BATCH, WINDOW, KV_DIM), F32),
            jax.ShapeDtypeStruct((BATCH, WINDOW, KV_DIM), F32),
            jax.ShapeDtypeStruct((BATCH, N_REC_HEADS, REC_HEAD_DIM, REC_HEAD_DIM), F32),
        ],
        scratch_shapes=[
            pltpu.VMEM((TILE, D_MODEL), BF16),
            pltpu.VMEM((TILE, ATT_COLS), F32),
            pltpu.VMEM((TILE, 4 * PROJ_COLS), F32),
            pltpu.VMEM((QBLK + TILE, 2 * KV_DIM), BF16),
            pltpu.VMEM((N_REC_HEADS, REC_HEAD_DIM, REC_HEAD_DIM), F32),
            pltpu.VMEM((TILE, D_MODEL), BF16),
            pltpu.VMEM((QBLK, 2 * KV_DIM), BF16),
            pltpu.VMEM((N_REC_HEADS, REC_HEAD_DIM, REC_HEAD_DIM), F32),
            pltpu.VMEM((N_REC_HEADS, REC_HEAD_DIM, REC_HEAD_DIM), F32),
            pltpu.VMEM((TILE, 4 * REC_HEAD_DIM), F32),
        ],
        compiler_params=pltpu.CompilerParams(
            dimension_semantics=("arbitrary",), vmem_limit_bytes=VMEM_LIMIT),
        name="prompt_layer",
    )(sinks, h_meta, h_main, w_in, w_out, norm_w, lb, oml, rec_w, bias, fin_w)


DEC_PROJ_COLS = D_IN // 4
DEC_OUT_COLS = D_MODEL // 2


def _dec_proj_kernel(hs_ref, normw_ref, win_ref, z_ref, wb_ref):
    xn = _rmsnorm_rows(hs_ref[...], normw_ref[...]).astype(BF16)
    for c0 in range(0, DEC_PROJ_COLS, V7X_LANES):
        cols = slice(c0, c0 + V7X_LANES)
        wb = win_ref[:, cols].astype(BF16)
        wb_ref[:, cols] = wb
        z_ref[:, cols] = _dot(xn, wb)


def _dec_proj(layer, hs, norm_w, w_in):
    return pl.pallas_call(
        _dec_proj_kernel,
        grid=(D_IN // DEC_PROJ_COLS,),
        in_specs=[
            pl.BlockSpec((N_DEC, D_MODEL), lambda n: (0, 0)),
            pl.BlockSpec((None, 1, D_MODEL), lambda n: (layer, 0, 0)),
            pl.BlockSpec((None, D_MODEL, DEC_PROJ_COLS), lambda n: (layer, 0, n)),
        ],
        out_specs=[pl.BlockSpec((N_DEC, DEC_PROJ_COLS), lambda n: (0, n)),
                   pl.BlockSpec((D_MODEL, DEC_PROJ_COLS), lambda n: (0, n))],
        out_shape=[jax.ShapeDtypeStruct((N_DEC, D_IN), F32),
                   jax.ShapeDtypeStruct((D_MODEL, D_IN), BF16)],
        compiler_params=pltpu.CompilerParams(dimension_semantics=("arbitrary",), vmem_limit_bytes=DEC_VMEM_LIMIT),
        name="dec_proj",
    )(hs, norm_w, w_in)


def _dec_out_kernel(hs_ref, y_ref, wout_ref, o_ref, wb_ref):
    for c0 in range(0, DEC_OUT_COLS, 2 * V7X_LANES):
        cols = slice(c0, c0 + 2 * V7X_LANES)
        wb = wout_ref[:, cols].astype(BF16)
        wb_ref[:, cols] = wb
        o_ref[:, cols] = hs_ref[:, cols] + _dot(y_ref[...], wb)


def _dec_out(layer, hs, y, w_out):
    return pl.pallas_call(
        _dec_out_kernel,
        grid=(D_MODEL // DEC_OUT_COLS,),
        in_specs=[
            pl.BlockSpec((N_DEC, DEC_OUT_COLS), lambda n: (0, n)),
            pl.BlockSpec((N_DEC, D_MODEL), lambda n: (0, 0)),
            pl.BlockSpec((None, D_MODEL, DEC_OUT_COLS), lambda n: (layer, 0, n)),
        ],
        out_specs=[pl.BlockSpec((N_DEC, DEC_OUT_COLS), lambda n: (0, n)),
                   pl.BlockSpec((D_MODEL, DEC_OUT_COLS), lambda n: (0, n))],
        out_shape=[jax.ShapeDtypeStruct((N_DEC, D_MODEL), F32),
                   jax.ShapeDtypeStruct((D_MODEL, D_MODEL), BF16)],
        compiler_params=pltpu.CompilerParams(dimension_semantics=("arbitrary",), vmem_limit_bytes=DEC_VMEM_LIMIT),
        name="dec_out",
    )(hs, y, w_out)


DEC_SEQS_PER_STEP = 4
DEC_LAG = DEC_SEQS_PER_STEP * (N_KV + N_REC_HEADS)


def _dec_attention_unit(layer, q, kh, sink_ref, z_ref, ck_ref, cv_ref, biasc_ref, biasn_ref, y_ref):
    t = DEC_SEQ
    tok = slice(q * t, (q + 1) * t)
    hs_ = slice(kh * HEAD_DIM, (kh + 1) * HEAD_DIM)
    kc = ck_ref[q, :, hs_].astype(BF16)
    kn = z_ref[tok, OFF_KA + kh * HEAD_DIM:OFF_KA + (kh + 1) * HEAD_DIM]
    q4 = jnp.concatenate(
        [z_ref[tok, (kh * GROUP + g) * HEAD_DIM:(kh * GROUP + g + 1) * HEAD_DIM] for g in range(GROUP)],
        axis=0) * (HEAD_DIM ** -0.5)
    rows = slice(kh * GROUP * t, (kh + 1) * GROUP * t)
    sc = _dot_nt(q4.astype(BF16), kc)
    sn = _dot_nt(q4, kn)
    yield
    sc = sc + biasc_ref[rows, :]
    sn = sn + biasn_ref[rows, :]
    ecs, ens, dens = [], [], []
    for g in range(GROUP):
        sink = sink_ref[layer, kh * GROUP + g]
        scg = sc[g * t:(g + 1) * t]
        sng = sn[g * t:(g + 1) * t]
        m = jnp.maximum(jnp.maximum(jnp.max(scg, axis=-1, keepdims=True),
                                    jnp.max(sng, axis=-1, keepdims=True)), sink)
        ec = jnp.exp(scg - m)
        en = jnp.exp(sng - m)
        dens.append(jnp.sum(ec, axis=-1, keepdims=True) + jnp.sum(en, axis=-1, keepdims=True)
                    + jnp.exp(sink - m))
        ecs.append(ec)
        ens.append(en)
    vc = cv_ref[q, :, hs_]
    vn = z_ref[tok, OFF_VA + kh * HEAD_DIM:OFF_VA + (kh + 1) * HEAD_DIM]
    o4 = (_dot(jnp.concatenate(ecs, axis=0).astype(BF16), vc.astype(BF16))
          + _dot(jnp.concatenate(ens, axis=0), vn))
    yield
    outs = []
    for g in range(GROUP):
        h = kh * GROUP + g
        og = o4[g * t:(g + 1) * t] / dens[g]
        ga = z_ref[tok, OFF_GA + h * HEAD_DIM:OFF_GA + (h + 1) * HEAD_DIM]
        outs.append(og * _silu(ga))
    y_ref[tok, kh * GROUP * HEAD_DIM:(kh + 1) * GROUP * HEAD_DIM] = jnp.concatenate(outs, axis=-1).astype(BF16)


def _dec_hgrn2_unit(q, hh, z_ref, s0_ref, lb_ref, oml_ref, recw_ref, y_ref, ns_ref):
    t = DEC_SEQ
    tok = slice(q * t, (q + 1) * t)
    row = lax.broadcasted_iota(jnp.int32, (t, t), 0)
    col = lax.broadcasted_iota(jnp.int32, (t, t), 1)
    causal = col <= row
    cs = hh * REC_HEAD_DIM
    lb = lb_ref[:, cs:cs + REC_HEAD_DIM]
    oml = oml_ref[:, cs:cs + REC_HEAD_DIM]
    logf, kr = _gate_features(z_ref[tok, OFF_FR + cs:OFF_FR + cs + REC_HEAD_DIM], lb, oml)
    qv = _silu(z_ref[tok, OFF_QR + cs:OFF_QR + cs + REC_HEAD_DIM])
    v = z_ref[tok, OFF_IR + cs:OFF_IR + cs + REC_HEAD_DIM]
    a = _cumsum_rows(logf)
    a_end = a[t - 1:t, :]
    scores = _decay_scores_direct(qv, kr, a)
    qt = qv * jnp.exp(a)
    kt = kr * jnp.exp(a_end - a)
    s0 = s0_ref[q, hh]
    o_state = _dot(qt.astype(BF16), s0.astype(BF16))
    yield
    att = jnp.where(causal, scores, 0.0)
    o = _dot(att, v) + o_state
    ktd = jnp.concatenate([kt, jnp.exp(a_end), jnp.zeros((t - 1, REC_HEAD_DIM), F32)], axis=0).T
    ns_ref[q, hh] = s0 * ktd[:, t:t + 1] + _dot(ktd[:, 0:t], v)
    yield
    on = o * lax.rsqrt(jnp.mean(o * o, axis=-1, keepdims=True) + EPS)
    gr = z_ref[tok, OFF_GR + cs:OFF_GR + cs + REC_HEAD_DIM]
    yr = (on * recw_ref[:, cs:cs + REC_HEAD_DIM]) * _silu(gr)
    y_ref[tok, D_ATTN + cs:D_ATTN + cs + REC_HEAD_DIM] = yr.astype(BF16)


def _dec_mix_kernel(layer, sink_ref, z_ref, ck_ref, cv_ref, s0_ref, lb_ref, oml_ref, recw_ref, biasc_ref,
                    biasn_ref, out_k_hbm, out_v_hbm, out_s_hbm, y_ref, nk_ref, nv_ref, ns_ref):
    del out_k_hbm, out_v_hbm, out_s_hbm
    t = DEC_SEQ
    for q in range(DEC_SEQS_PER_STEP):
        tok = slice(q * t, (q + 1) * t)
        nk_ref[q, 0:WINDOW - t, :] = ck_ref[q, t:WINDOW, :]
        nk_ref[q, WINDOW - t:WINDOW, :] = z_ref[tok, OFF_KA:OFF_KA + KV_DIM]
        nv_ref[q, 0:WINDOW - t, :] = cv_ref[q, t:WINDOW, :]
        nv_ref[q, WINDOW - t:WINDOW, :] = z_ref[tok, OFF_VA:OFF_VA + KV_DIM]
    units = []
    for kh in range(N_KV):
        units += [_dec_attention_unit(layer, q, kh, sink_ref, z_ref, ck_ref, cv_ref, biasc_ref, biasn_ref, y_ref)
                  for q in range(DEC_SEQS_PER_STEP)]
    for hh in range(N_REC_HEADS):
        units += [_dec_hgrn2_unit(q, hh, z_ref, s0_ref, lb_ref, oml_ref, recw_ref, y_ref, ns_ref)
                  for q in range(DEC_SEQS_PER_STEP)]
    _emit_staggered(units, lag=DEC_LAG)


def _dec_mix(layer, z, ck, cv, s0, lb, oml, rec_w, sinks, bias_c, bias_n, out_k, out_v, out_s):
    t = DEC_SEQ
    nq = DEC_SEQS_PER_STEP
    const2 = lambda s: (0, 0)
    per_layer = lambda s: (layer, 0, 0)
    kv_in = pl.BlockSpec((None, nq, WINDOW, KV_DIM), lambda s: (layer, s, 0, 0))
    st_in = pl.BlockSpec((None, nq, N_REC_HEADS, REC_HEAD_DIM, REC_HEAD_DIM), lambda s: (layer, s, 0, 0, 0))
    return pl.pallas_call(
        functools.partial(_dec_mix_kernel, layer),
        grid=(DEC_BATCH // nq,),
        in_specs=[
            pl.BlockSpec(memory_space=pltpu.SMEM),
            pl.BlockSpec((nq * t, D_IN), lambda s: (s, 0)),
            kv_in,
            kv_in,
            st_in,
            pl.BlockSpec((None, 1, D_REC), per_layer),
            pl.BlockSpec((None, 1, D_REC), per_layer),
            pl.BlockSpec((None, 1, D_REC), per_layer),
            pl.BlockSpec((N_HEADS * t, WINDOW), const2),
            pl.BlockSpec((N_HEADS * t, t), const2),
            pl.BlockSpec(memory_space=pl.ANY),
            pl.BlockSpec(memory_space=pl.ANY),
            pl.BlockSpec(memory_space=pl.ANY),
        ],
        out_specs=[
            pl.BlockSpec((nq * t, D_MODEL), lambda s: (s, 0)),
            kv_in,
            kv_in,
            st_in,
        ],
        out_shape=[
            jax.ShapeDtypeStruct((N_DEC, D_MODEL), BF16),
            jax.ShapeDtypeStruct(out_k.shape, F32),
            jax.ShapeDtypeStruct(out_v.shape, F32),
            jax.ShapeDtypeStruct(out_s.shape, F32),
        ],
        input_output_aliases={10: 1, 11: 2, 12: 3},
        compiler_params=pltpu.CompilerParams(dimension_semantics=("arbitrary",)),
        name="dec_mix",
    )(sinks, z, ck, cv, s0, lb, oml, rec_w, bias_c, bias_n, out_k, out_v, out_s)


def _final_norm_kernel(x_ref, w_ref, o_ref):
    o_ref[...] = _rmsnorm_rows(x_ref[...], w_ref[...])


def _final_norm_dec(hs, w):
    return pl.pallas_call(
        _final_norm_kernel,
        out_shape=jax.ShapeDtypeStruct((N_DEC, D_MODEL), F32),
        name="final_norm_dec",
    )(hs, w)


def _t5_bucket(dist):
    max_exact = N_BUCKETS // 2
    d = jnp.maximum(dist, 0)
    df = jnp.maximum(d, 1).astype(F32)
    large = max_exact + (jnp.log(df / max_exact) / math.log(MAX_DISTANCE / max_exact)
                         * (N_BUCKETS - max_exact)).astype(jnp.int32)
    large = jnp.minimum(large, N_BUCKETS - 1)
    return jnp.where(d < max_exact, d, large)


def _bias_tables(rel_bias_table):
    bias_d = rel_bias_table[_t5_bucket(jnp.arange(WINDOW))].T.astype(F32)
    period = 3 * WINDOW + 1
    u = jnp.concatenate([jnp.full((N_HEADS, 1), NEG, F32), bias_d[:, ::-1],
                         jnp.full((N_HEADS, period - 1 - WINDOW), NEG, F32)], axis=1)
    rows = jnp.tile(u, (1, WINDOW))[:, :WINDOW * (period - 1)].reshape(N_HEADS, WINDOW, period - 1)
    prompt = rows[:, :, :2 * WINDOW].reshape(N_HEADS * WINDOW, 2 * WINDOW).astype(BF16)
    dec = rows[:, :DEC_SEQ, :WINDOW + DEC_SEQ].reshape(N_HEADS * DEC_SEQ, WINDOW + DEC_SEQ)
    return prompt, dec[:, :WINDOW], dec[:, WINDOW:]


def kernel(x_prompt, x_sample, cache_k, cache_v, state_h, meta_tokens, w_in, w_out, norm_w, final_norm_w,
           attn_sinks, rel_bias_table, hgrn_lb_logits, hgrn_norm_w):
    pl_ = jax.nn.softmax(hgrn_lb_logits.astype(F32), axis=0)
    lb = (jnp.cumsum(pl_, axis=0) - pl_[0:1]).reshape(DEPTH, 1, D_REC)
    oml = 1.0 - lb
    bias_p, bias_dc, bias_dn = _bias_tables(rel_bias_table)
    nw = norm_w.astype(F32).reshape(DEPTH, 1, D_MODEL)
    rw = hgrn_norm_w.astype(F32).reshape(DEPTH, 1, D_REC)
    fw = final_norm_w.astype(F32).reshape(1, D_MODEL)
    sinks = attn_sinks.astype(F32)

    h_meta = meta_tokens.astype(F32)
    h_main = x_prompt
    hs = x_sample.reshape(N_DEC, D_MODEL)
    ck = cache_k.reshape(DEPTH, DEC_BATCH, WINDOW, KV_DIM)
    cv = cache_v.reshape(DEPTH, DEC_BATCH, WINDOW, KV_DIM)

    pk, pv, ps = [], [], []
    sk = lax.empty(ck.shape, F32)
    sv = lax.empty(cv.shape, F32)
    ss = lax.empty(state_h.shape, F32)
    for l in range(DEPTH):
        z, w_in_b = _dec_proj(l, hs, nw, w_in)
        y, sk, sv, ss = _dec_mix(l, z, ck, cv, state_h, lb, oml, rw, sinks, bias_dc, bias_dn, sk, sv, ss)
        hs, w_out_b = _dec_out(l, hs, y, w_out)

        outs = _prompt_layer(l, h_meta, h_main, w_in_b, w_out_b, nw, lb, oml, rw, sinks, bias_p, fw)
        if l < DEPTH - 1:
            h_meta, h_main, k_l, v_l, s_l = outs
        else:
            h_main, k_l, v_l, s_l = outs
        pk.append(k_l)
        pv.append(v_l)
        ps.append(s_l)

    y_prompt = h_main
    y_sample = _final_norm_dec(hs, fw).reshape(DEC_BATCH, DEC_SEQ, D_MODEL)
    kv_shape = (DEPTH, -1, WINDOW, N_KV, HEAD_DIM)
    return (y_prompt, y_sample,
            jnp.stack(pk).reshape(kv_shape), jnp.stack(pv).reshape(kv_shape),
            jnp.swapaxes(jnp.stack(ps), -1, -2),
            sk.reshape(kv_shape), sv.reshape(kv_shape), ss)
```
